```python
import math
import jax, jax.numpy as jnp
from jax import lax
import numpy as np

D_MODEL = 1024
BATCH = 32
SEQ = 2048
DEPTH = 1

RW_HEADS = 8
RW_HEAD = 64
RW_WIDTH = RW_HEADS * RW_HEAD
DECAY_LORA = 64
ICLR_LORA = 64
GATE_LORA = 128
DA_HEADS = 4
DA_HEAD = 64
DA_VDIM = 2 * DA_HEAD
DA_QK_WIDTH = DA_HEADS * 2 * DA_HEAD
DA_WIDTH = DA_HEADS * DA_VDIM
Q_BLOCK = 128
PK_HEADS = 8
PK_NKEYS = 128
PK_QDIM = 256
PK_TOPK = 16
PK_EXPERTS = PK_NKEYS * PK_NKEYS
PK_CHUNK = 128
NORM_EPS = 1e-6
GN_EPS = 64e-5
SUBLN_EPS = 1e-5
SHIFT_COLS = 3 * RW_WIDTH + DECAY_LORA + ICLR_LORA + GATE_LORA
GATE_COLS = 2 * D_MODEL
IN_COLS = SHIFT_COLS + 2 * DA_QK_WIDTH + DA_WIDTH + GATE_COLS

kernel_name = 'hybrid_rwkv7_diffattn_peer_block'


def rmsnorm(x, w, eps=NORM_EPS):
    xf = x.astype(jnp.float32)
    y = xf * lax.rsqrt(jnp.mean(xf * xf, axis=-1, keepdims=True) + eps)
    return (y * w.astype(jnp.float32)).astype(x.dtype)


def token_shift(z, mu):
    z_prev = jnp.pad(z, ((0, 0), (1, 0), (0, 0)))[:, :-1]
    return z + (z_prev - z) * mu


def rwkv7_time_mix(p, w0, w2, a0, a2, g2, k_k, k_a, r_k, lnx_w, lnx_b):
    B, S, _ = p.shape
    f32 = jnp.float32
    c = [RW_WIDTH, 2 * RW_WIDTH, 3 * RW_WIDTH, 3 * RW_WIDTH + DECAY_LORA, 3 * RW_WIDTH + DECAY_LORA + ICLR_LORA]
    r, k, v, wl, al, gl = jnp.split(p.astype(f32), c, axis=-1)
    w = -jax.nn.softplus(-(w0 + jnp.tanh(wl) @ w2)) - 0.5
    decay = jnp.exp(-jnp.exp(w))
    a = jax.nn.sigmoid(a0 + al @ a2)
    g = jax.nn.sigmoid(gl) @ g2
    heads = lambda t: t.reshape(B, S, RW_HEADS, RW_HEAD)
    kk = heads(k * k_k)
    kk = kk / jnp.maximum(jnp.sqrt(jnp.sum(kk * kk, axis=-1, keepdims=True)), 1e-12)
    k = k * (1.0 + (a - 1.0) * k_a)
    r, k, v, a, decay = heads(r), heads(k), heads(v), heads(a), heads(decay)

    def step(state, inp):
        r_t, w_t, k_t, v_t, kk_t, a_t = inp
        sa = jnp.einsum('bhvk,bhk->bhv', state, -kk_t)
        state = (state * w_t[:, :, None, :]
                 + sa[..., None] * (kk_t * a_t)[:, :, None, :]
                 + v_t[..., None] * k_t[:, :, None, :])
        return state, jnp.einsum('bhvk,bhk->bhv', state, r_t)

    tm = lambda t: jnp.moveaxis(t, 1, 0)
    state0 = jnp.zeros((B, RW_HEADS, RW_HEAD, RW_HEAD), f32)
    _, y = lax.scan(step, state0, (tm(r), tm(decay), tm(k), tm(v), tm(kk), tm(a)))
    y = jnp.moveaxis(y, 0, 1)
    mu = jnp.mean(y, axis=-1, keepdims=True)
    var = jnp.mean(jnp.square(y - mu), axis=-1, keepdims=True)
    y = ((y - mu) * lax.rsqrt(var + GN_EPS)).reshape(B, S, RW_WIDTH) * lnx_w + lnx_b
    bonus = jnp.sum(r * k * r_k, axis=-1, keepdims=True) * v
    return (y + bonus.reshape(B, S, RW_WIDTH)) * g


def diff_attention(q, k, v, lam_q1, lam_k1, lam_q2, lam_k2, subln_w, lam_init):
    B, S, _ = q.shape
    f32 = jnp.float32
    q = q.reshape(B, S, DA_HEADS, 2, DA_HEAD).astype(f32)
    k = k.reshape(B, S, DA_HEADS, 2, DA_HEAD).astype(f32)
    v = v.reshape(B, S, DA_HEADS, DA_VDIM).astype(f32)
    lam = (jnp.exp(jnp.sum(lam_q1.astype(f32) * lam_k1.astype(f32)))
           - jnp.exp(jnp.sum(lam_q2.astype(f32) * lam_k2.astype(f32))) + lam_init)
    slopes = jnp.asarray(2.0 ** (-8.0 * np.arange(1, DA_HEADS + 1) / DA_HEADS), dtype=f32)
    scale = 1.0 / math.sqrt(DA_HEAD)
    nb = S // Q_BLOCK
    qb = jnp.moveaxis(q.reshape(B, nb, Q_BLOCK, DA_HEADS, 2, DA_HEAD), 1, 0)
    kpos = jnp.arange(S)

    def block(args):
        qi, bi = args
        scores = jnp.einsum('bqhcd,bkhcd->bhcqk', qi, k) * scale
        qpos = bi * Q_BLOCK + jnp.arange(Q_BLOCK)
        dist = (qpos[:, None] - kpos[None, :]).astype(f32)
        scores = scores - (slopes[:, None, None] * dist)[None, :, None]
        scores = jnp.where((dist >= 0)[None, None, None], scores, -jnp.inf)
        pr = jax.nn.softmax(scores, axis=-1)
        attn = pr[:, :, 0] - lam * pr[:, :, 1]
        return jnp.einsum('bhqk,bkhe->bqhe', attn, v)

    o = lax.map(block, (qb, jnp.arange(nb)))
    o = jnp.moveaxis(o, 0, 1).reshape(B, S, DA_HEADS, DA_VDIM)
    o = rmsnorm(o, subln_w, SUBLN_EPS) * (1.0 - lam_init)
    return o.reshape(B, S, DA_WIDTH)


def peer_ffn(h, w_q, sub_keys, u_tab, v_tab):
    B, S, D = h.shape
    T = B * S
    hf = h.reshape(T, D)
    q = (hf @ w_q).reshape(T, PK_HEADS, 2, PK_QDIM // 2).astype(jnp.float32)
    s = jnp.einsum('thcd,hcnd->thcn', q, sub_keys.astype(jnp.float32))
    v1, i1 = lax.top_k(s[:, :, 0], PK_TOPK)
    v2, i2 = lax.top_k(s[:, :, 1], PK_TOPK)
    cand = (v1[..., :, None] + v2[..., None, :]).reshape(T, PK_HEADS, PK_TOPK * PK_TOPK)
    vals, ci = lax.top_k(cand, PK_TOPK)
    e1 = jnp.take_along_axis(i1, ci // PK_TOPK, axis=-1)
    e2 = jnp.take_along_axis(i2, ci % PK_TOPK, axis=-1)
    ids = (e1 * PK_NKEYS + e2).reshape(T, PK_HEADS * PK_TOPK)
    gates = jax.nn.softmax(vals, axis=-1).reshape(T, PK_HEADS * PK_TOPK).astype(h.dtype)
    nc = T // PK_CHUNK

    def chunk(args):
        xc, idc, gc = args
        act = jnp.einsum('ced,cd->ce', u_tab[idc], xc)
        act = jax.nn.gelu(act, approximate=False) * gc
        return jnp.einsum('ce,ced->cd', act, v_tab[idc])

    y = lax.map(chunk, (hf.reshape(nc, PK_CHUNK, D),
                        ids.reshape(nc, PK_CHUNK, PK_HEADS * PK_TOPK),
                        gates.reshape(nc, PK_CHUNK, PK_HEADS * PK_TOPK)))
    return y.reshape(B, S, D)


def setup_inputs(seed: int = 0) -> dict:
    key = jax.random.key(seed)
    ks = jax.random.split(key, 32)
    L = DEPTH
    f32 = jnp.float32
    nrm = lambda k, shape, sc: jax.random.normal(k, shape, f32) * sc
    return {
        'x': nrm(ks[0], (BATCH, SEQ, D_MODEL), 1.0),
        'norm_mix_w': 1.0 + nrm(ks[1], (L, D_MODEL), 0.1),
        'w_in': nrm(ks[2], (L, D_MODEL, IN_COLS), D_MODEL ** -0.5),
        'shift_mu': jax.random.uniform(ks[3], (L, SHIFT_COLS), f32, 0.1, 0.9),
        'w0': -1.0 + nrm(ks[4], (L, RW_WIDTH), 1.0),
        'w2': nrm(ks[5], (L, DECAY_LORA, RW_WIDTH), 0.1),
        'a0': nrm(ks[6], (L, RW_WIDTH), 0.1),
        'a2': nrm(ks[7], (L, ICLR_LORA, RW_WIDTH), 0.1),
        'g2': nrm(ks[8], (L, GATE_LORA, RW_WIDTH), GATE_LORA ** -0.5),
        'k_k': 0.85 + nrm(ks[9], (L, RW_WIDTH), 0.05),
        'k_a': 1.0 + nrm(ks[10], (L, RW_WIDTH), 0.05),
        'r_k': nrm(ks[11], (L, RW_HEADS, RW_HEAD), 0.1),
        'lnx_w': 1.0 + nrm(ks[12], (L, RW_WIDTH), 0.1),
        'lnx_b': nrm(ks[13], (L, RW_WIDTH), 0.02),
        'lam_q1': nrm(ks[14], (L, DA_HEAD), 0.1),
        'lam_k1': nrm(ks[15], (L, DA_HEAD), 0.1),
        'lam_q2': nrm(ks[16], (L, DA_HEAD), 0.1),
        'lam_k2': nrm(ks[17], (L, DA_HEAD), 0.1),
        'subln_w': 1.0 + nrm(ks[18], (L, DA_VDIM), 0.1),
        'proj_a': nrm(ks[19], (L, RW_WIDTH, D_MODEL), RW_WIDTH ** -0.5),
        'proj_b': nrm(ks[20], (L, DA_WIDTH, D_MODEL), DA_WIDTH ** -0.5),
        'w_out': nrm(ks[21], (L, D_MODEL, D_MODEL), D_MODEL ** -0.5),
        'norm_ffn_w': 1.0 + nrm(ks[22], (L, D_MODEL), 0.1),
        'peer_wq': nrm(ks[23], (L, D_MODEL, PK_HEADS * PK_QDIM), D_MODEL ** -0.5),
        'peer_keys': nrm(ks[24], (L, PK_HEADS, 2, PK_NKEYS, PK_QDIM // 2), (PK_QDIM // 2) ** -0.5),
        'peer_u': nrm(ks[25], (L, PK_EXPERTS, D_MODEL), D_MODEL ** -0.5),
        'peer_v': nrm(ks[26], (L, PK_EXPERTS, D_MODEL), 0.3),
        'final_norm_w': 1.0 + nrm(ks[27], (D_MODEL,), 0.1),
    }


def reference(x, norm_mix_w, w_in, shift_mu, w0, w2, a0, a2, g2, k_k, k_a, r_k, lnx_w, lnx_b,
              lam_q1, lam_k1, lam_q2, lam_k2, subln_w, proj_a, proj_b, w_out,
              norm_ffn_w, peer_wq, peer_keys, peer_u, peer_v, final_norm_w):
    c1 = SHIFT_COLS
    c2 = c1 + DA_QK_WIDTH
    c3 = c2 + DA_QK_WIDTH
    c4 = c3 + DA_WIDTH
    for l in range(DEPTH):
        lam_init = 0.8 - 0.6 * math.exp(-0.3 * l)
        h = rmsnorm(x, norm_mix_w[l])
        z = h @ w_in[l]
        zs, zq, zk, zv, zg = jnp.split(z, [c1, c2, c3, c4], axis=-1)
        zs = token_shift(zs, shift_mu[l])
        ya = rwkv7_time_mix(zs, w0[l], w2[l], a0[l], a2[l], g2[l], k_k[l], k_a[l], r_k[l],
                            lnx_w[l], lnx_b[l]).astype(x.dtype)
        yb = diff_attention(zq, zk, zv, lam_q1[l], lam_k1[l], lam_q2[l], lam_k2[l], subln_w[l],
                            lam_init).astype(x.dtype)
        ga, gb = jnp.split(zg, 2, axis=-1)
        merged = jax.nn.sigmoid(ga) * (ya @ proj_a[l]) + jax.nn.sigmoid(gb) * (yb @ proj_b[l])
        x = x + merged @ w_out[l]
        x = x + peer_ffn(rmsnorm(x, norm_ffn_w[l]), peer_wq[l], peer_keys[l], peer_u[l], peer_v[l])
    return rmsnorm(x, final_norm_w)
```

```python
import functools
import math

import jax
import jax.numpy as jnp
from jax import lax
from jax.experimental import pallas as pl
from jax.experimental.pallas import tpu as pltpu

F32 = jnp.float32
BF16 = jnp.bfloat16

D_MODEL = 1024
RW_HEADS = 8
RW_HEAD = 64
RW_WIDTH = RW_HEADS * RW_HEAD
DECAY_LORA = 64
ICLR_LORA = 64
GATE_LORA = 128
DA_HEADS = 4
DA_HEAD = 64
DA_VDIM = 2 * DA_HEAD
DA_QK_WIDTH = DA_HEADS * 2 * DA_HEAD
DA_WIDTH = DA_HEADS * DA_VDIM
PK_HEADS = 8
PK_NKEYS = 128
PK_QDIM = 256
PK_TOPK = 16
PK_EXPERTS = PK_NKEYS * PK_NKEYS
NORM_EPS = 1e-6
GN_EPS = 64e-5
SUBLN_EPS = 1e-5
SHIFT_COLS = 3 * RW_WIDTH + DECAY_LORA + ICLR_LORA + GATE_LORA
QKV_COLS = 2 * DA_QK_WIDTH + DA_WIDTH
GATE_COLS = 2 * D_MODEL
LAM_INIT = 0.8 - 0.6 * math.exp(0.0)

LANES = 128
CHUNK = 64
PAIR = 2 * RW_HEAD
NEG_BIG = -1e30
VMEM_LIMIT = 56 * 1024 * 1024

NT_DIMS = (((1,), (1,)), ((), ()))
TN_DIMS = (((0,), (0,)), ((), ()))


def _dot(a, b):
    return jnp.dot(a, b, preferred_element_type=F32)


def _dot_nt(a, b):
    return lax.dot_general(a, b, NT_DIMS, preferred_element_type=F32)


def _dot_tn(a, b):
    return lax.dot_general(a, b, TN_DIMS, preferred_element_type=F32)


def _sigmoid(x):
    return 1.0 / (1.0 + jnp.exp(-x))


def _split3(x):
    hi = x.astype(BF16)
    r1 = x - hi.astype(F32)
    mid = r1.astype(BF16)
    lo = (r1 - mid.astype(F32)).astype(BF16)
    return hi, mid, lo


def _dot_exact_rhs(a_bf16, x):
    hi, mid, lo = _split3(x)
    return _dot(a_bf16, hi) + _dot(a_bf16, mid) + _dot(a_bf16, lo)


def _dot_exact_lhs(x, b_bf16):
    hi, mid, lo = _split3(x)
    return _dot(hi, b_bf16) + _dot(mid, b_bf16) + _dot(lo, b_bf16)


def _inproj_kernel(x_ref, nw_ref, w_ref, zs_ref, zqkv_ref, zg_ref, *, col_chunk):
    x = x_ref[...]
    ms = jnp.mean(x * x, axis=-1, keepdims=True)
    h = (x * lax.rsqrt(ms + NORM_EPS) * nw_ref[...]).astype(BF16)
    c0 = 0
    for out_ref in (zs_ref, zqkv_ref, zg_ref):
        width = out_ref.shape[-1]
        for j in range(0, width, col_chunk):
            z = _dot(h, w_ref[:, c0 + j:c0 + j + col_chunk])
            out_ref[:, j:j + col_chunk] = z.astype(out_ref.dtype)
        c0 += width


def _inproj(x2, norm_w, w_in_bf16, tm=256, col_chunk=256):
    T = x2.shape[0]
    in_cols = w_in_bf16.shape[1]
    return pl.pallas_call(
        functools.partial(_inproj_kernel, col_chunk=col_chunk),
        grid=(T // tm,),
        in_specs=[
            pl.BlockSpec((tm, D_MODEL), lambda i: (i, 0)),
            pl.BlockSpec((1, D_MODEL), lambda i: (0, 0)),
            pl.BlockSpec((D_MODEL, in_cols), lambda i: (0, 0)),
        ],
        out_specs=[
            pl.BlockSpec((tm, SHIFT_COLS), lambda i: (i, 0)),
            pl.BlockSpec((tm, QKV_COLS), lambda i: (i, 0)),
            pl.BlockSpec((tm, GATE_COLS), lambda i: (i, 0)),
        ],
        out_shape=[
            jax.ShapeDtypeStruct((T, SHIFT_COLS), F32),
            jax.ShapeDtypeStruct((T, QKV_COLS), BF16),
            jax.ShapeDtypeStruct((T, GATE_COLS), F32),
        ],
        compiler_params=pltpu.CompilerParams(
            dimension_semantics=("parallel",), vmem_limit_bytes=VMEM_LIMIT),
        name="inproj",
    )(x2, norm_w, w_in_bf16)


def _rwkv_kernel(zs_ref, mu_ref, w0_ref, a0_ref, kk_ref, ka_ref, rk_ref, lnw_ref, lnb_ref,
                 wlora_ref, g2_ref, ones_ref, tri_ref, ya_ref,
                 state_ref, prev_ref, r_s, k_s, v_s, a_s, b_s, ld_s, l_s, y_s, *, ts):
    t = pl.program_id(1)

    @pl.when(t == 0)
    def _():
        state_ref[...] = jnp.zeros_like(state_ref)
        prev_ref[...] = jnp.zeros_like(prev_ref)

    z = zs_ref[0]
    row = lax.broadcasted_iota(jnp.int32, (ts, 1), 0)
    zprev = jnp.where(row == 0, prev_ref[...], pltpu.roll(z, 1, axis=0))
    prev_ref[...] = z[ts - 1:ts, :]
    zz = z + (zprev - z) * mu_ref[...]

    W = RW_WIDTH
    r = zz[:, 0:W]
    k = zz[:, W:2 * W]
    v = zz[:, 2 * W:3 * W]
    wa = zz[:, 3 * W:3 * W + LANES]
    gl = zz[:, 3 * W + LANES:3 * W + 2 * LANES]
    lane = lax.broadcasted_iota(jnp.int32, (1, LANES), 1)
    wa_act = jnp.where(lane < DECAY_LORA, jnp.tanh(wa), wa).astype(BF16)
    lora = _dot(wa_act, wlora_ref[...])
    u = -(w0_ref[...] + lora[:, 0:W])
    softplus = jnp.maximum(u, 0.0) + jnp.log1p(jnp.exp(-jnp.abs(u)))
    wlog = -softplus - 0.5
    ld = -jnp.exp(wlog)
    a = _sigmoid(a0_ref[...] + lora[:, W:2 * W])
    g = _dot(_sigmoid(gl).astype(BF16), g2_ref[...])

    ones_blk = ones_ref[...]
    kk = k * kk_ref[...]
    ss = _dot_exact_lhs(kk * kk, ones_blk)
    kk = kk / jnp.maximum(jnp.sqrt(ss), 1e-12)
    k2 = k * (1.0 + (a - 1.0) * ka_ref[...])
    bonus = _dot_exact_lhs(r * k2 * rk_ref[...], ones_blk) * v

    r_s[...] = r
    k_s[...] = k2
    v_s[...] = v
    a_s[...] = -kk
    b_s[...] = kk * a
    ld_s[...] = ld
    l_s[...] = _dot_exact_rhs(tri_ref[...], ld)

    rowi = lax.broadcasted_iota(jnp.int32, (PAIR, PAIR), 0)
    coli = lax.broadcasted_iota(jnp.int32, (PAIR, PAIR), 1)
    strict_lower = rowi > coli
    lower = rowi >= coli
    eye = rowi == coli
    head0 = lane < RW_HEAD

    def stack(xp):
        return jnp.concatenate([jnp.where(head0, xp, 0.0), jnp.where(head0, 0.0, xp)], axis=0)

    def unstack(xs):
        return xs[0:CHUNK] + xs[CHUNK:2 * CHUNK]

    mid = CHUNK // 2 - 1

    def chunk_body(c, carry):
        r0 = pl.multiple_of(c * CHUNK, CHUNK)
        rows = pl.ds(r0, CHUNK)
        lc = l_s[rows, :]
        ldc = ld_s[rows, :]
        cm = l_s[pl.ds(r0 + mid, 1), :]
        lend = l_s[pl.ds(r0 + CHUNK - 1, 1), :]
        e_pos = jnp.exp(lc - cm)
        e_neg = jnp.exp(cm - lc)
        e_prev = jnp.exp(lc - ldc - cm)
        e_cm = jnp.exp(cm)
        e_end = e_neg * jnp.exp(lend - cm)
        p_end = jnp.exp(lend)
        rc = r_s[rows, :]
        ac = a_s[rows, :]
        bc = b_s[rows, :]
        kc = k_s[rows, :]
        vc = v_s[rows, :]
        r_cen = rc * e_pos
        r_tru = r_cen * e_cm
        a_cen = ac * e_prev
        a_tru = a_cen * e_cm
        b_cen = bc * e_neg
        k_cen = kc * e_neg
        b_end = bc * e_end
        k_end = kc * e_end

        for p in range(RW_HEADS // 2):
            ls = slice(p * PAIR, (p + 1) * PAIR)
            lhs = jnp.concatenate([stack(a_cen[:, ls]), stack(r_cen[:, ls])], axis=0).astype(BF16)
            rhs = jnp.concatenate([stack(b_cen[:, ls]), stack(k_cen[:, ls])], axis=0).astype(BF16)
            sc = _dot_nt(lhs, rhs)
            a_ab = jnp.where(strict_lower, sc[0:PAIR, 0:PAIR], 0.0)
            a_ak = jnp.where(strict_lower, sc[0:PAIR, PAIR:2 * PAIR], 0.0)
            m_rb = jnp.where(lower, sc[PAIR:2 * PAIR, 0:PAIR], 0.0)
            m_rk = jnp.where(lower, sc[PAIR:2 * PAIR, PAIR:2 * PAIR], 0.0)
            v_st = stack(vc[:, ls]).astype(BF16)
            x = jnp.concatenate([stack(a_tru[:, ls]), _dot(a_ak.astype(BF16), v_st)], axis=1)
            n = a_ab
            steps = int(math.log2(CHUNK))
            for i in range(steps):
                nb = n.astype(BF16)
                x = x + _dot(nb, x.astype(BF16))
                if i + 1 < steps:
                    n = _dot(nb, nb)
            xb = x.astype(BF16)
            ry = _dot(m_rb.astype(BF16), xb)
            r_new = unstack(stack(r_tru[:, ls]) + ry[:, 0:PAIR])
            y0 = unstack(ry[:, PAIR:2 * PAIR] + _dot(m_rk.astype(BF16), v_st))
            b_st = stack(b_end[:, ls]).astype(BF16)
            k_st = stack(k_end[:, ls]).astype(BF16)
            gh = _dot_tn(xb, b_st)
            g_t = gh[0:PAIR]
            h_t = gh[PAIR:2 * PAIR] + _dot_tn(v_st, k_st)
            s_old = state_ref[p]
            sb = s_old.astype(BF16)
            y = _dot_nt(r_new.astype(BF16), sb) + y0
            s_new = s_old * p_end[:, ls] + _dot(sb, g_t.astype(BF16)) + h_t
            state_ref[p] = s_new
            y_s[rows, ls] = y
        return carry

    lax.fori_loop(0, ts // CHUNK, chunk_body, 0)

    y = y_s[...]
    inv_n = 1.0 / RW_HEAD
    mean = _dot_exact_lhs(y, ones_blk) * inv_n
    yc = y - mean
    var = _dot_exact_lhs(yc * yc, ones_blk) * inv_n
    yn = yc * lax.rsqrt(var + GN_EPS) * lnw_ref[...] + lnb_ref[...]
    ya_ref[0] = ((yn + bonus) * g).astype(ya_ref.dtype)


def _rwkv(zs3, mu, w0, a0, k_k, k_a, r_k, lnx_w, lnx_b, wlora, g2, ts=256):
    B, S, _ = zs3.shape
    W = RW_WIDTH
    ones_blk = (jnp.arange(W)[:, None] // RW_HEAD == jnp.arange(W)[None, :] // RW_HEAD).astype(BF16)
    ti = jnp.arange(ts)
    tri = ((ti[:, None] // CHUNK == ti[None, :] // CHUNK) & (ti[:, None] >= ti[None, :])).astype(BF16)
    vec = lambda n: pl.BlockSpec((1, n), lambda b, t: (0, 0))
    full = lambda a: pl.BlockSpec(a.shape, lambda b, t: (0,) * a.ndim)
    return pl.pallas_call(
        functools.partial(_rwkv_kernel, ts=ts),
        grid=(B, S // ts),
        in_specs=[
            pl.BlockSpec((1, ts, SHIFT_COLS), lambda b, t: (b, t, 0)),
            vec(SHIFT_COLS), vec(W), vec(W), vec(W), vec(W), vec(W), vec(W), vec(W),
            full(wlora), full(g2), full(ones_blk), full(tri),
        ],
        out_specs=pl.BlockSpec((1, ts, W), lambda b, t: (b, t, 0)),
        out_shape=jax.ShapeDtypeStruct((B, S, W), BF16),
        scratch_shapes=[
            pltpu.VMEM((RW_HEADS // 2, PAIR, PAIR), F32),
            pltpu.VMEM((1, SHIFT_COLS), F32),
        ] + [pltpu.VMEM((ts, W), F32) for _ in range(8)],
        compiler_params=pltpu.CompilerParams(
            dimension_semantics=("parallel", "arbitrary"), vmem_limit_bytes=VMEM_LIMIT),
        name="rwkv",
    )(zs3, mu, w0, a0, k_k, k_a, r_k, lnx_w, lnx_b, wlora, g2, ones_blk, tri)


def _attn_kernel(slope_ref, lq1_ref, lk1_ref, lq2_ref, lk2_ref, sw_ref, q_ref, k_ref, v_ref, o_ref,
                 m_s, l_s, acc_s, *, tq, tk):
    h = pl.program_id(1)
    qi = pl.program_id(2)
    slope = slope_ref[h]
    lam = (jnp.exp(jnp.sum(lq1_ref[...] * lk1_ref[...], axis=-1, keepdims=True))
           - jnp.exp(jnp.sum(lq2_ref[...] * lk2_ref[...], axis=-1, keepdims=True)) + LAM_INIT)

    lane = lax.broadcasted_iota(jnp.int32, (1, LANES), 1)
    map0 = lane < DA_HEAD
    q = q_ref[0] * (1.0 / math.sqrt(DA_HEAD))
    zero = jnp.zeros_like(q)
    qst = jnp.concatenate([jnp.where(map0, q, zero), jnp.where(map0, zero, q)], axis=0)

    m_s[...] = jnp.full_like(m_s, NEG_BIG)
    l_s[...] = jnp.zeros_like(l_s)
    acc_s[...] = jnp.zeros_like(acc_s)

    rowq = lax.broadcasted_iota(jnp.int32, (2 * tq, tk), 0)
    rowq = jnp.where(rowq >= tq, rowq - tq, rowq)
    rel = (rowq - lax.broadcasted_iota(jnp.int32, (2 * tq, tk), 1)).astype(F32)

    def body(j, carry):
        k0 = pl.multiple_of(j * tk, tk)
        kb = k_ref[0, pl.ds(k0, tk), :]
        vb = v_ref[0, pl.ds(k0, tk), :]
        s = _dot_nt(qst, kb)
        dist = rel + (qi * tq - j * tk).astype(F32)
        s = jnp.where(dist >= 0.0, s - slope * dist, NEG_BIG)
        m_old = m_s[...]
        m_new = jnp.maximum(m_old, jnp.max(s, axis=1, keepdims=True))
        alpha = jnp.exp(m_old - m_new)
        p = jnp.exp(s - m_new[:, 0:1])
        l_s[...] = alpha * l_s[...] + jnp.sum(p, axis=1, keepdims=True)
        acc_s[...] = alpha * acc_s[...] + _dot(p.astype(BF16), vb)
        m_s[...] = m_new
        return carry

    nk = (qi * tq + tq + tk - 1) // tk
    lax.fori_loop(0, nk, body, 0)

    o = acc_s[...] / l_s[...]
    o = o[0:tq] - lam * o[tq:2 * tq]
    o = o * lax.rsqrt(jnp.mean(o * o, axis=-1, keepdims=True) + SUBLN_EPS) * sw_ref[...]
    o_ref[0] = (o * (1.0 - LAM_INIT)).astype(o_ref.dtype)


def _attn(zqkv3, lam_q1, lam_k1, lam_q2, lam_k2, subln_w, tq=256, tk=256):
    B, S, _ = zqkv3.shape
    slopes = jnp.asarray([2.0 ** (-8.0 * (i + 1) / DA_HEADS) for i in range(DA_HEADS)], F32)
    nqk = DA_QK_WIDTH // LANES
    vec = lambda n: pl.BlockSpec((1, n), lambda b, h, i: (0, 0))
    return pl.pallas_call(
        functools.partial(_attn_kernel, tq=tq, tk=tk),
        grid=(B, DA_HEADS, S // tq),
        in_specs=[
            pl.BlockSpec(memory_space=pltpu.SMEM),
            vec(DA_HEAD), vec(DA_HEAD), vec(DA_HEAD), vec(DA_HEAD), vec(DA_VDIM),
            pl.BlockSpec((1, tq, LANES), lambda b, h, i: (b, i, h)),
            pl.BlockSpec((1, S, LANES), lambda b, h, i: (b, 0, nqk + h)),
            pl.BlockSpec((1, S, LANES), lambda b, h, i: (b, 0, 2 * nqk + h)),
        ],
        out_specs=pl.BlockSpec((1, tq, LANES), lambda b, h, i: (b, i, h)),
        out_shape=jax.ShapeDtypeStruct((B, S, DA_WIDTH), BF16),
        scratch_shapes=[
            pltpu.VMEM((2 * tq, LANES), F32),
            pltpu.VMEM((2 * tq, LANES), F32),
            pltpu.VMEM((2 * tq, LANES), F32),
        ],
        compiler_params=pltpu.CompilerParams(
            dimension_semantics=("parallel", "parallel", "arbitrary"), vmem_limit_bytes=VMEM_LIMIT),
        name="attn",
    )(slopes, lam_q1, lam_k1, lam_q2, lam_k2, subln_w, zqkv3, zqkv3, zqkv3)


def _merge_kernel(ya_ref, yb_ref, zg_ref, x_ref, pa_ref, pb_ref, wo_ref, nw_ref, wq_ref,
                  x1_ref, h2_ref, q_ref):
    pa = _dot(ya_ref[...], pa_ref[...])
    pb = _dot(yb_ref[...], pb_ref[...])
    ga = zg_ref[:, 0:D_MODEL]
    gb = zg_ref[:, D_MODEL:2 * D_MODEL]
    merged = _sigmoid(ga) * pa + _sigmoid(gb) * pb
    x1 = x_ref[...] + _dot(merged.astype(BF16), wo_ref[...])
    x1_ref[...] = x1
    ms = jnp.mean(x1 * x1, axis=-1, keepdims=True)
    h2 = (x1 * lax.rsqrt(ms + NORM_EPS) * nw_ref[...]).astype(BF16)
    h2_ref[...] = h2
    q_ref[...] = _dot(h2, wq_ref[...]).astype(q_ref.dtype)


def _merge(ya2, yb2, zg, x2, proj_a, proj_b, w_out, norm_w, wq, tm=256):
    T = x2.shape[0]
    qw = wq.shape[1]
    row = lambda n: pl.BlockSpec((tm, n), lambda i: (i, 0))
    full = lambda a: pl.BlockSpec(a.shape, lambda i: (0,) * a.ndim)
    return pl.pallas_call(
        _merge_kernel,
        grid=(T // tm,),
        in_specs=[row(RW_WIDTH), row(DA_WIDTH), row(GATE_COLS), row(D_MODEL),
                  full(proj_a), full(proj_b), full(w_out), full(norm_w), full(wq)],
        out_specs=[row(D_MODEL), row(D_MODEL), row(qw)],
        out_shape=[
            jax.ShapeDtypeStruct((T, D_MODEL), F32),
            jax.ShapeDtypeStruct((T, D_MODEL), BF16),
            jax.ShapeDtypeStruct((T, qw), BF16),
        ],
        compiler_params=pltpu.CompilerParams(
            dimension_semantics=("parallel",), vmem_limit_bytes=VMEM_LIMIT),
        name="merge",
    )(ya2, yb2, zg, x2, proj_a, proj_b, w_out, norm_w, wq)


def _route_kernel(q_ref, keys_ref, r2_ref, e2_ref, n1_ref, c1_ref, v1_s, v2_s):
    half = PK_QDIM // 2
    q = q_ref[...]
    s1 = _dot_nt(keys_ref[0, 0], q[:, 0:half])
    s2 = _dot_nt(keys_ref[0, 1], q[:, half:2 * half])

    def top_ranks(s, v_s):
        rank = jnp.full(s.shape, float(PK_TOPK), F32)
        work = s
        for i in range(PK_TOPK):
            m = jnp.max(work, axis=0, keepdims=True)
            v_s[i:i + 1, :] = m
            hit = work == m
            rank = jnp.where(hit, float(i), rank)
            work = jnp.where(hit, -jnp.inf, work)
        return rank

    rank1 = top_ranks(s1, v1_s)
    rank2 = top_ranks(s2, v2_s)
    v1 = v1_s[...]
    v2 = v2_s[...]
    tm = v1.shape[1]
    cand = (v1[:, None, :] + v2[None, :, :]).reshape(PK_TOPK * PK_TOPK, tm)
    work = cand
    tau = None
    for i in range(PK_TOPK):
        tau = jnp.max(work, axis=0, keepdims=True)
        work = jnp.where(work == tau, -jnp.inf, work)
    cmax = v1[0:1] + v2[0:1]
    sel = cand >= tau
    z = jnp.sum(jnp.where(sel, jnp.exp(cand - cmax), 0.0), axis=0, keepdims=True)
    cnt = jnp.sum(jnp.where(sel, 1.0, 0.0).reshape(PK_TOPK, PK_TOPK, tm), axis=1)
    n1 = jnp.zeros_like(s1)
    for i in range(PK_TOPK):
        n1 = jnp.where(rank1 == float(i), cnt[i:i + 1], n1)
    r2_ref[0] = rank2.astype(r2_ref.dtype)
    e2_ref[0] = jnp.exp(s2 - v2[0:1]).astype(e2_ref.dtype)
    n1_ref[0] = n1
    c1_ref[0] = jnp.exp(s1 - v1[0:1]) / z


def _route(q2, keys, tm=256):
    T = q2.shape[0]
    blk = lambda: pl.BlockSpec((1, PK_NKEYS, tm), lambda i, h: (h, 0, i))
    shp = lambda dt: jax.ShapeDtypeStruct((PK_HEADS, PK_NKEYS, T), dt)
    return pl.pallas_call(
        _route_kernel,
        grid=(T // tm, PK_HEADS),
        in_specs=[
            pl.BlockSpec((tm, PK_QDIM), lambda i, h: (i, h)),
            pl.BlockSpec((1, 2, PK_NKEYS, PK_QDIM // 2), lambda i, h: (h, 0, 0, 0)),
        ],
        out_specs=[blk(), blk(), blk(), blk()],
        out_shape=[shp(BF16), shp(BF16), shp(F32), shp(F32)],
        scratch_shapes=[pltpu.VMEM((PK_TOPK, tm), F32), pltpu.VMEM((PK_TOPK, tm), F32)],
        compiler_params=pltpu.CompilerParams(
            dimension_semantics=("parallel", "arbitrary"), vmem_limit_bytes=VMEM_LIMIT),
        name="route",
    )(q2, keys)


def _peer_kernel(h_ref, u_ref, vt_ref, r2_ref, e2_ref, n1_ref, c1_ref, x1_ref, nw_ref, o_ref,
                 acc_s, aw_s, *, te):
    j = pl.program_id(1)

    @pl.when(j == 0)
    def _():
        acc_s[...] = jnp.zeros_like(acc_s)

    h = h_ref[...]
    ne = te // PK_NKEYS

    def body(e, carry):
        e1 = j * ne + e
        r0 = pl.multiple_of(e * PK_NKEYS, PK_NKEYS)
        act = _dot_nt(u_ref[pl.ds(r0, PK_NKEYS), :], h)
        gelu = 0.5 * act * (1.0 + lax.erf(act * (1.0 / math.sqrt(2.0))))
        w = jnp.zeros(act.shape, BF16)
        for hd in range(PK_HEADS):
            n = n1_ref[hd, pl.ds(e1, 1), :].astype(BF16)
            c = c1_ref[hd, pl.ds(e1, 1), :].astype(BF16)
            w = w + jnp.where(r2_ref[hd] < n, e2_ref[hd] * c, jnp.zeros_like(w))
        aw_s[pl.ds(r0, PK_NKEYS), :] = (gelu * w.astype(F32)).astype(BF16)
        return carry

    lax.fori_loop(0, ne, body, 0)
    acc_s[...] += _dot(vt_ref[...], aw_s[...])

    @pl.when(j == pl.num_programs(1) - 1)
    def _():
        x2 = x1_ref[...] + acc_s[...].T
        ms = jnp.mean(x2 * x2, axis=-1, keepdims=True)
        o_ref[...] = x2 * lax.rsqrt(ms + NORM_EPS) * nw_ref[...]


def _peer(h2, u_bf16, vt_bf16, r2, e2, n1, c1, x1, final_w, tm=512, te=2048):
    T = h2.shape[0]
    rt = lambda: pl.BlockSpec((PK_HEADS, PK_NKEYS, tm), lambda i, j: (0, 0, i))
    return pl.pallas_call(
        functools.partial(_peer_kernel, te=te),
        grid=(T // tm, PK_EXPERTS // te),
        in_specs=[
            pl.BlockSpec((tm, D_MODEL), lambda i, j: (i, 0)),
            pl.BlockSpec((te, D_MODEL), lambda i, j: (j, 0)),
            pl.BlockSpec((D_MODEL, te), lambda i, j: (0, j)),
            rt(), rt(), rt(), rt(),
            pl.BlockSpec((tm, D_MODEL), lambda i, j: (i, 0)),
            pl.BlockSpec((1, D_MODEL), lambda i, j: (0, 0)),
        ],
        out_specs=pl.BlockSpec((tm, D_MODEL), lambda i, j: (i, 0)),
        out_shape=jax.ShapeDtypeStruct((T, D_MODEL), F32),
        scratch_shapes=[pltpu.VMEM((D_MODEL, tm), F32), pltpu.VMEM((te, tm), BF16)],
        compiler_params=pltpu.CompilerParams(
            dimension_semantics=("parallel", "arbitrary"), vmem_limit_bytes=VMEM_LIMIT),
        name="peer",
    )(h2, u_bf16, vt_bf16, r2, e2, n1, c1, x1, final_w)


def _lora_blockdiag(w2, a2):
    z = jnp.zeros_like(w2)
    return jnp.concatenate([jnp.concatenate([w2, z], axis=1), jnp.concatenate([z, a2], axis=1)], axis=0)


def kernel(x, norm_mix_w, w_in, shift_mu, w0, w2, a0, a2, g2, k_k, k_a, r_k, lnx_w, lnx_b, lam_q1, lam_k1, lam_q2, lam_k2, subln_w, proj_a, proj_b, w_out, norm_ffn_w, peer_wq, peer_keys, peer_u, peer_v, final_norm_w):
    B, S, D = x.shape
    T = B * S
    depth = w_in.shape[0]
    assert depth == 1 and D == D_MODEL
    l = 0
    x2 = x.reshape(T, D)
    zs, zqkv, zg = _inproj(x2, norm_mix_w[l][None], w_in[l].astype(BF16))
    ya = _rwkv(zs.reshape(B, S, SHIFT_COLS), shift_mu[l][None], w0[l][None], a0[l][None], k_k[l][None],
               k_a[l][None], r_k[l].reshape(1, RW_WIDTH), lnx_w[l][None], lnx_b[l][None],
               _lora_blockdiag(w2[l], a2[l]).astype(BF16), g2[l].astype(BF16))
    yb = _attn(zqkv.reshape(B, S, QKV_COLS), lam_q1[l][None], lam_k1[l][None], lam_q2[l][None],
               lam_k2[l][None], subln_w[l][None])
    x1, h2, q = _merge(ya.reshape(T, RW_WIDTH), yb.reshape(T, DA_WIDTH), zg, x2,
                       proj_a[l].astype(BF16), proj_b[l].astype(BF16), w_out[l].astype(BF16),
                       norm_ffn_w[l][None], peer_wq[l].astype(BF16))
    r2, e2, n1, c1 = _route(q, peer_keys[l].astype(BF16))
    out = _peer(h2, peer_u[l].astype(BF16), peer_v[l].T.astype(BF16), r2, e2, n1, c1, x1,
                final_norm_w[None])
    return out.reshape(B, S, D)
```

```python
import functools
import math

import jax
import jax.numpy as jnp
from jax import lax
from jax.experimental import pallas as pl
from jax.experimental.pallas import tpu as pltpu

F32 = jnp.float32
BF16 = jnp.bfloat16

D_MODEL = 1024
RW_HEADS = 8
RW_HEAD = 64
RW_WIDTH = RW_HEADS * RW_HEAD
DECAY_LORA = 64
ICLR_LORA = 64
GATE_LORA = 128
DA_HEADS = 4
DA_HEAD = 64
DA_VDIM = 2 * DA_HEAD
DA_QK_WIDTH = DA_HEADS * 2 * DA_HEAD
DA_WIDTH = DA_HEADS * DA_VDIM
PK_HEADS = 8
PK_NKEYS = 128
PK_QDIM = 256
PK_TOPK = 16
PK_EXPERTS = PK_NKEYS * PK_NKEYS
NORM_EPS = 1e-6
GN_EPS = 64e-5
SUBLN_EPS = 1e-5
SHIFT_COLS = 3 * RW_WIDTH + DECAY_LORA + ICLR_LORA + GATE_LORA
QKV_COLS = 2 * DA_QK_WIDTH + DA_WIDTH
GATE_COLS = 2 * D_MODEL
LAM_INIT = 0.8 - 0.6 * math.exp(0.0)

LANES = 128
CHUNK = 64
PAIR = 2 * RW_HEAD
NEG_BIG = -1e30
VMEM_LIMIT = 56 * 1024 * 1024

NT_DIMS = (((1,), (1,)), ((), ()))
TN_DIMS = (((0,), (0,)), ((), ()))


def _dot(a, b):
    return jnp.dot(a, b, preferred_element_type=F32)


def _dot_nt(a, b):
    return lax.dot_general(a, b, NT_DIMS, preferred_element_type=F32)


def _dot_tn(a, b):
    return lax.dot_general(a, b, TN_DIMS, preferred_element_type=F32)


def _sigmoid(x):
    return 1.0 / (1.0 + jnp.exp(-x))


def _split3(x):
    hi = x.astype(BF16)
    r1 = x - hi.astype(F32)
    mid = r1.astype(BF16)
    lo = (r1 - mid.astype(F32)).astype(BF16)
    return hi, mid, lo


def _dot_exact_rhs(a_bf16, x):
    hi, mid, lo = _split3(x)
    return _dot(a_bf16, hi) + _dot(a_bf16, mid) + _dot(a_bf16, lo)


def _dot_exact_lhs(x, b_bf16):
    hi, mid, lo = _split3(x)
    return _dot(hi, b_bf16) + _dot(mid, b_bf16) + _dot(lo, b_bf16)


def _inproj_kernel(x_ref, nw_ref, w_ref, zs_ref, zqkv_ref, zg_ref, *, col_chunk):
    x = x_ref[...]
    ms = jnp.mean(x * x, axis=-1, keepdims=True)
    h = (x * lax.rsqrt(ms + NORM_EPS) * nw_ref[...]).astype(BF16)
    c0 = 0
    for out_ref in (zs_ref, zqkv_ref, zg_ref):
        width = out_ref.shape[-1]
        for j in range(0, width, col_chunk):
            z = _dot(h, w_ref[:, c0 + j:c0 + j + col_chunk])
            out_ref[:, j:j + col_chunk] = z.astype(out_ref.dtype)
        c0 += width


def _inproj(x2, norm_w, w_in_bf16, tm=256, col_chunk=256):
    T = x2.shape[0]
    in_cols = w_in_bf16.shape[1]
    return pl.pallas_call(
        functools.partial(_inproj_kernel, col_chunk=col_chunk),
        grid=(T // tm,),
        in_specs=[
            pl.BlockSpec((tm, D_MODEL), lambda i: (i, 0)),
            pl.BlockSpec((1, D_MODEL), lambda i: (0, 0)),
            pl.BlockSpec((D_MODEL, in_cols), lambda i: (0, 0)),
        ],
        out_specs=[
            pl.BlockSpec((tm, SHIFT_COLS), lambda i: (i, 0)),
            pl.BlockSpec((tm, QKV_COLS), lambda i: (i, 0)),
            pl.BlockSpec((tm, GATE_COLS), lambda i: (i, 0)),
        ],
        out_shape=[
            jax.ShapeDtypeStruct((T, SHIFT_COLS), F32),
            jax.ShapeDtypeStruct((T, QKV_COLS), BF16),
            jax.ShapeDtypeStruct((T, GATE_COLS), F32),
        ],
        compiler_params=pltpu.CompilerParams(
            dimension_semantics=("parallel",), vmem_limit_bytes=VMEM_LIMIT),
        name="inproj",
    )(x2, norm_w, w_in_bf16)


def _rwkv_kernel(zs_ref, mu_ref, w0_ref, a0_ref, kk_ref, ka_ref, rk_ref, lnw_ref, lnb_ref,
                 wlora_ref, g2_ref, ones_ref, tri_ref, ya_ref,
                 state_ref, prev_ref, r_s, k_s, v_s, a_s, b_s, ld_s, l_s, y_s, *, ts):
    t = pl.program_id(1)

    @pl.when(t == 0)
    def _():
        state_ref[...] = jnp.zeros_like(state_ref)
        prev_ref[...] = jnp.zeros_like(prev_ref)

    z = zs_ref[0]
    row = lax.broadcasted_iota(jnp.int32, (ts, 1), 0)
    zprev = jnp.where(row == 0, prev_ref[...], pltpu.roll(z, 1, axis=0))
    prev_ref[...] = z[ts - 1:ts, :]
    zz = z + (zprev - z) * mu_ref[...]

    W = RW_WIDTH
    r = zz[:, 0:W]
    k = zz[:, W:2 * W]
    v = zz[:, 2 * W:3 * W]
    wa = zz[:, 3 * W:3 * W + LANES]
    gl = zz[:, 3 * W + LANES:3 * W + 2 * LANES]
    lane = lax.broadcasted_iota(jnp.int32, (1, LANES), 1)
    wa_act = jnp.where(lane < DECAY_LORA, jnp.tanh(wa), wa).astype(BF16)
    lora = _dot(wa_act, wlora_ref[...])
    u = -(w0_ref[...] + lora[:, 0:W])
    softplus = jnp.maximum(u, 0.0) + jnp.log1p(jnp.exp(-jnp.abs(u)))
    wlog = -softplus - 0.5
    ld = -jnp.exp(wlog)
    a = _sigmoid(a0_ref[...] + lora[:, W:2 * W])
    g = _dot(_sigmoid(gl).astype(BF16), g2_ref[...])

    ones_blk = ones_ref[...]
    kk = k * kk_ref[...]
    ss = _dot_exact_lhs(kk * kk, ones_blk)
    kk = kk / jnp.maximum(jnp.sqrt(ss), 1e-12)
    k2 = k * (1.0 + (a - 1.0) * ka_ref[...])
    bonus = _dot_exact_lhs(r * k2 * rk_ref[...], ones_blk) * v

    r_s[...] = r
    k_s[...] = k2
    v_s[...] = v
    a_s[...] = -kk
    b_s[...] = kk * a
    ld_s[...] = ld
    l_s[...] = _dot_exact_rhs(tri_ref[...], ld)

    rowi = lax.broadcasted_iota(jnp.int32, (PAIR, PAIR), 0)
    coli = lax.broadcasted_iota(jnp.int32, (PAIR, PAIR), 1)
    strict_lower = rowi > coli
    lower = rowi >= coli
    eye = rowi == coli
    head0 = lane < RW_HEAD

    def stack(xp):
        return jnp.concatenate([jnp.where(head0, xp, 0.0), jnp.where(head0, 0.0, xp)], axis=0)

    def unstack(xs):
        return xs[0:CHUNK] + xs[CHUNK:2 * CHUNK]

    mid = CHUNK // 2 - 1

    def chunk_body(c, carry):
        r0 = pl.multiple_of(c * CHUNK, CHUNK)
        rows = pl.ds(r0, CHUNK)
        lc = l_s[rows, :]
        ldc = ld_s[rows, :]
        cm = l_s[pl.ds(r0 + mid, 1), :]
        lend = l_s[pl.ds(r0 + CHUNK - 1, 1), :]
        e_pos = jnp.exp(lc - cm)
        e_neg = jnp.exp(cm - lc)
        e_prev = jnp.exp(lc - ldc - cm)
        e_cm = jnp.exp(cm)
        e_end = e_neg * jnp.exp(lend - cm)
        p_end = jnp.exp(lend)
        rc = r_s[rows, :]
        ac = a_s[rows, :]
        bc = b_s[rows, :]
        kc = k_s[rows, :]
        vc = v_s[rows, :]
        r_cen = rc * e_pos
        r_tru = r_cen * e_cm
        a_cen = ac * e_prev
        a_tru = a_cen * e_cm
        b_cen = bc * e_neg
        k_cen = kc * e_neg
        b_end = bc * e_end
        k_end = kc * e_end

        pairs = range(RW_HEADS // 2)
        sl = [slice(p * PAIR, (p + 1) * PAIR) for p in pairs]
        sc = [_dot_nt(jnp.concatenate([stack(a_cen[:, ls]), stack(r_cen[:, ls])], axis=0).astype(BF16),
                      jnp.concatenate([stack(b_cen[:, ls]), stack(k_cen[:, ls])], axis=0).astype(BF16))
              for ls in sl]
        a_ab = [jnp.where(strict_lower, s_[0:PAIR, 0:PAIR], 0.0) for s_ in sc]
        a_ak = [jnp.where(strict_lower, s_[0:PAIR, PAIR:2 * PAIR], 0.0).astype(BF16) for s_ in sc]
        m_rb = [jnp.where(lower, s_[PAIR:2 * PAIR, 0:PAIR], 0.0).astype(BF16) for s_ in sc]
        m_rk = [jnp.where(lower, s_[PAIR:2 * PAIR, PAIR:2 * PAIR], 0.0).astype(BF16) for s_ in sc]
        v_st = [stack(vc[:, ls]).astype(BF16) for ls in sl]
        x = [jnp.concatenate([stack(a_tru[:, ls]), _dot(a_ak[p], v_st[p])], axis=1) for p, ls in enumerate(sl)]
        n = a_ab
        steps = int(math.log2(CHUNK))
        for i in range(steps):
            nb = [n_.astype(BF16) for n_ in n]
            x = [x[p] + _dot(nb[p], x[p].astype(BF16)) for p in pairs]
            if i + 1 < steps:
                n = [_dot(nb_, nb_) for nb_ in nb]
        xb = [x_.astype(BF16) for x_ in x]
        ry = [_dot(m_rb[p], xb[p]) for p in pairs]
        r_new = [unstack(stack(r_tru[:, ls]) + ry[p][:, 0:PAIR]).astype(BF16) for p, ls in enumerate(sl)]
        y0 = [unstack(ry[p][:, PAIR:2 * PAIR] + _dot(m_rk[p], v_st[p])) for p in pairs]
        b_st = [stack(b_end[:, ls]).astype(BF16) for ls in sl]
        k_st = [stack(k_end[:, ls]).astype(BF16) for ls in sl]
        gh = [_dot_tn(xb[p], b_st[p]) for p in pairs]
        h_t = [gh[p][PAIR:2 * PAIR] + _dot_tn(v_st[p], k_st[p]) for p in pairs]
        for p, ls in enumerate(sl):
            s_old = state_ref[p]
            sb = s_old.astype(BF16)
            y_s[rows, ls] = _dot_nt(r_new[p], sb) + y0[p]
            state_ref[p] = s_old * p_end[:, ls] + _dot(sb, gh[p][0:PAIR].astype(BF16)) + h_t[p]
        return carry

    lax.fori_loop(0, ts // CHUNK, chunk_body, 0)

    y = y_s[...]
    inv_n = 1.0 / RW_HEAD
    mean = _dot_exact_lhs(y, ones_blk) * inv_n
    yc = y - mean
    var = _dot_exact_lhs(yc * yc, ones_blk) * inv_n
    yn = yc * lax.rsqrt(var + GN_EPS) * lnw_ref[...] + lnb_ref[...]
    ya_ref[0] = ((yn + bonus) * g).astype(ya_ref.dtype)


def _rwkv(zs3, mu, w0, a0, k_k, k_a, r_k, lnx_w, lnx_b, wlora, g2, ts=256):
    B, S, _ = zs3.shape
    W = RW_WIDTH
    ones_blk = (jnp.arange(W)[:, None] // RW_HEAD == jnp.arange(W)[None, :] // RW_HEAD).astype(BF16)
    ti = jnp.arange(ts)
    tri = ((ti[:, None] // CHUNK == ti[None, :] // CHUNK) & (ti[:, None] >= ti[None, :])).astype(BF16)
    vec = lambda n: pl.BlockSpec((1, n), lambda b, t: (0, 0))
    full = lambda a: pl.BlockSpec(a.shape, lambda b, t: (0,) * a.ndim)
    return pl.pallas_call(
        functools.partial(_rwkv_kernel, ts=ts),
        grid=(B, S // ts),
        in_specs=[
            pl.BlockSpec((1, ts, SHIFT_COLS), lambda b, t: (b, t, 0)),
            vec(SHIFT_COLS), vec(W), vec(W), vec(W), vec(W), vec(W), vec(W), vec(W),
            full(wlora), full(g2), full(ones_blk), full(tri),
        ],
        out_specs=pl.BlockSpec((1, ts, W), lambda b, t: (b, t, 0)),
        out_shape=jax.ShapeDtypeStruct((B, S, W), BF16),
        scratch_shapes=[
            pltpu.VMEM((RW_HEADS // 2, PAIR, PAIR), F32),
            pltpu.VMEM((1, SHIFT_COLS), F32),
        ] + [pltpu.VMEM((ts, W), F32) for _ in range(8)],
        compiler_params=pltpu.CompilerParams(
            dimension_semantics=("parallel", "arbitrary"), vmem_limit_bytes=VMEM_LIMIT),
        name="rwkv",
    )(zs3, mu, w0, a0, k_k, k_a, r_k, lnx_w, lnx_b, wlora, g2, ones_blk, tri)


def _attn_kernel(slope_ref, lq1_ref, lk1_ref, lq2_ref, lk2_ref, sw_ref, q_ref, k_ref, v_ref, o_ref,
                 m_s, l_s, acc_s, *, tq, tk):
    h = pl.program_id(1)
    qi = pl.program_id(2)
    slope = slope_ref[h]
    lam = (jnp.exp(jnp.sum(lq1_ref[...] * lk1_ref[...], axis=-1, keepdims=True))
           - jnp.exp(jnp.sum(lq2_ref[...] * lk2_ref[...], axis=-1, keepdims=True)) + LAM_INIT)

    lane = lax.broadcasted_iota(jnp.int32, (1, LANES), 1)
    map0 = lane < DA_HEAD
    q = q_ref[0] * (1.0 / math.sqrt(DA_HEAD))
    zero = jnp.zeros_like(q)
    qst = jnp.concatenate([jnp.where(map0, q, zero), jnp.where(map0, zero, q)], axis=0)

    m_s[...] = jnp.full_like(m_s, NEG_BIG)
    l_s[...] = jnp.zeros_like(l_s)
    acc_s[...] = jnp.zeros_like(acc_s)

    rowq = lax.broadcasted_iota(jnp.int32, (2 * tq, tk), 0)
    rowq = jnp.where(rowq >= tq, rowq - tq, rowq)
    rel = (rowq - lax.broadcasted_iota(jnp.int32, (2 * tq, tk), 1)).astype(F32)

    def body(j, carry):
        k0 = pl.multiple_of(j * tk, tk)
        kb = k_ref[0, pl.ds(k0, tk), :]
        vb = v_ref[0, pl.ds(k0, tk), :]
        s = _dot_nt(qst, kb)
        dist = rel + (qi * tq - j * tk).astype(F32)
        s = jnp.where(dist >= 0.0, s - slope * dist, NEG_BIG)
        m_old = m_s[...]
        m_new = jnp.maximum(m_old, jnp.max(s, axis=1, keepdims=True))
        alpha = jnp.exp(m_old - m_new)
        p = jnp.exp(s - m_new[:, 0:1])
        l_s[...] = alpha * l_s[...] + jnp.sum(p, axis=1, keepdims=True)
        acc_s[...] = alpha * acc_s[...] + _dot(p.astype(BF16), vb)
        m_s[...] = m_new
        return carry

    nk = (qi * tq + tq + tk - 1) // tk
    lax.fori_loop(0, nk, body, 0)

    o = acc_s[...] / l_s[...]
    o = o[0:tq] - lam * o[tq:2 * tq]
    o = o * lax.rsqrt(jnp.mean(o * o, axis=-1, keepdims=True) + SUBLN_EPS) * sw_ref[...]
    o_ref[0] = (o * (1.0 - LAM_INIT)).astype(o_ref.dtype)


def _attn(zqkv3, lam_q1, lam_k1, lam_q2, lam_k2, subln_w, tq=512, tk=512):
    B, S, _ = zqkv3.shape
    slopes = jnp.asarray([2.0 ** (-8.0 * (i + 1) / DA_HEADS) for i in range(DA_HEADS)], F32)
    nqk = DA_QK_WIDTH // LANES
    vec = lambda n: pl.BlockSpec((1, n), lambda b, h, i: (0, 0))
    return pl.pallas_call(
        functools.partial(_attn_kernel, tq=tq, tk=tk),
        grid=(B, DA_HEADS, S // tq),
        in_specs=[
            pl.BlockSpec(memory_space=pltpu.SMEM),
            vec(DA_HEAD), vec(DA_HEAD), vec(DA_HEAD), vec(DA_HEAD), vec(DA_VDIM),
            pl.BlockSpec((1, tq, LANES), lambda b, h, i: (b, i, h)),
            pl.BlockSpec((1, S, LANES), lambda b, h, i: (b, 0, nqk + h)),
            pl.BlockSpec((1, S, LANES), lambda b, h, i: (b, 0, 2 * nqk + h)),
        ],
        out_specs=pl.BlockSpec((1, tq, LANES), lambda b, h, i: (b, i, h)),
        out_shape=jax.ShapeDtypeStruct((B, S, DA_WIDTH), BF16),
        scratch_shapes=[
            pltpu.VMEM((2 * tq, LANES), F32),
            pltpu.VMEM((2 * tq, LANES), F32),
            pltpu.VMEM((2 * tq, LANES), F32),
        ],
        compiler_params=pltpu.CompilerParams(
            dimension_semantics=("parallel", "parallel", "arbitrary"), vmem_limit_bytes=VMEM_LIMIT),
        name="attn",
    )(slopes, lam_q1, lam_k1, lam_q2, lam_k2, subln_w, zqkv3, zqkv3, zqkv3)


def _merge_kernel(ya_ref, yb_ref, zg_ref, x_ref, pa_ref, pb_ref, wo_ref, nw_ref, wq_ref,
                  x1_ref, h2_ref, q_ref):
    pa = _dot(ya_ref[...], pa_ref[...])
    pb = _dot(yb_ref[...], pb_ref[...])
    ga = zg_ref[:, 0:D_MODEL]
    gb = zg_ref[:, D_MODEL:2 * D_MODEL]
    merged = _sigmoid(ga) * pa + _sigmoid(gb) * pb
    x1 = x_ref[...] + _dot(merged.astype(BF16), wo_ref[...])
    x1_ref[...] = x1
    ms = jnp.mean(x1 * x1, axis=-1, keepdims=True)
    h2 = (x1 * lax.rsqrt(ms + NORM_EPS) * nw_ref[...]).astype(BF16)
    h2_ref[...] = h2
    q_ref[...] = _dot(h2, wq_ref[...]).astype(q_ref.dtype)


def _merge(ya2, yb2, zg, x2, proj_a, proj_b, w_out, norm_w, wq, tm=256):
    T = x2.shape[0]
    qw = wq.shape[1]
    row = lambda n: pl.BlockSpec((tm, n), lambda i: (i, 0))
    full = lambda a: pl.BlockSpec(a.shape, lambda i: (0,) * a.ndim)
    return pl.pallas_call(
        _merge_kernel,
        grid=(T // tm,),
        in_specs=[row(RW_WIDTH), row(DA_WIDTH), row(GATE_COLS), row(D_MODEL),
                  full(proj_a), full(proj_b), full(w_out), full(norm_w), full(wq)],
        out_specs=[row(D_MODEL), row(D_MODEL), row(qw)],
        out_shape=[
            jax.ShapeDtypeStruct((T, D_MODEL), F32),
            jax.ShapeDtypeStruct((T, D_MODEL), BF16),
            jax.ShapeDtypeStruct((T, qw), BF16),
        ],
        compiler_params=pltpu.CompilerParams(
            dimension_semantics=("parallel",), vmem_limit_bytes=VMEM_LIMIT),
        name="merge",
    )(ya2, yb2, zg, x2, proj_a, proj_b, w_out, norm_w, wq)


def _route_kernel(q_ref, keys_ref, r2_ref, e2_ref, n1_ref, c1_ref, v1_s, v2_s):
    half = PK_QDIM // 2
    q = q_ref[...]
    s1 = _dot_nt(keys_ref[0, 0], q[:, 0:half])
    s2 = _dot_nt(keys_ref[0, 1], q[:, half:2 * half])

    def top_ranks(s, v_s):
        rank = jnp.full(s.shape, float(PK_TOPK), F32)
        work = s
        for i in range(PK_TOPK):
            m = jnp.max(work, axis=0, keepdims=True)
            v_s[i:i + 1, :] = m
            hit = work == m
            rank = jnp.where(hit, float(i), rank)
            work = jnp.where(hit, -jnp.inf, work)
        return rank

    rank1 = top_ranks(s1, v1_s)
    rank2 = top_ranks(s2, v2_s)
    v1 = v1_s[...]
    v2 = v2_s[...]
    tm = v1.shape[1]
    cand = (v1[:, None, :] + v2[None, :, :]).reshape(PK_TOPK * PK_TOPK, tm)
    work = cand
    tau = None
    for i in range(PK_TOPK):
        tau = jnp.max(work, axis=0, keepdims=True)
        work = jnp.where(work == tau, -jnp.inf, work)
    cmax = v1[0:1] + v2[0:1]
    sel = cand >= tau
    z = jnp.sum(jnp.where(sel, jnp.exp(cand - cmax), 0.0), axis=0, keepdims=True)
    cnt = jnp.sum(jnp.where(sel, 1.0, 0.0).reshape(PK_TOPK, PK_TOPK, tm), axis=1)
    n1 = jnp.zeros_like(s1)
    for i in range(PK_TOPK):
        n1 = jnp.where(rank1 == float(i), cnt[i:i + 1], n1)
    r2_ref[0] = rank2.astype(r2_ref.dtype)
    e2_ref[0] = jnp.exp(s2 - v2[0:1]).astype(e2_ref.dtype)
    n1_ref[0] = n1
    c1_ref[0] = jnp.exp(s1 - v1[0:1]) / z


def _route(q2, keys, tm=256):
    T = q2.shape[0]
    blk = lambda: pl.BlockSpec((1, PK_NKEYS, tm), lambda i, h: (h, 0, i))
    shp = lambda dt: jax.ShapeDtypeStruct((PK_HEADS, PK_NKEYS, T), dt)
    return pl.pallas_call(
        _route_kernel,
        grid=(T // tm, PK_HEADS),
        in_specs=[
            pl.BlockSpec((tm, PK_QDIM), lambda i, h: (i, h)),
            pl.BlockSpec((1, 2, PK_NKEYS, PK_QDIM // 2), lambda i, h: (h, 0, 0, 0)),
        ],
        out_specs=[blk(), blk(), blk(), blk()],
        out_shape=[shp(BF16), shp(BF16), shp(F32), shp(F32)],
        scratch_shapes=[pltpu.VMEM((PK_TOPK, tm), F32), pltpu.VMEM((PK_TOPK, tm), F32)],
        compiler_params=pltpu.CompilerParams(
            dimension_semantics=("parallel", "arbitrary"), vmem_limit_bytes=VMEM_LIMIT),
        name="route",
    )(q2, keys)


def _peer_kernel(h_ref, u_ref, vt_ref, r2_ref, e2_ref, n1_ref, c1_ref, x1_ref, nw_ref, o_ref,
                 acc_s, aw0_s, aw1_s, act0_s, act1_s, *, te, nst):
    s = pl.program_id(0)
    j = lax.rem(s, nst)
    jp = lax.rem(s + nst - 1, nst)

    @pl.when(s == 0)
    def _():
        acc_s[...] = jnp.zeros_like(acc_s)
        aw1_s[...] = jnp.zeros_like(aw1_s)

    group = 2 * PK_NKEYS
    npiece = te // group
    drows = D_MODEL // npiece
    restart = jp == 0
    act_bufs = (act0_s, act1_s)

    def u_proj(i):
        act_bufs[i % 2][...] = _dot_nt(u_ref[i * group:(i + 1) * group, :], h_ref[...])

    def step(aw_build, aw_drain):
        u_proj(0)
        for i in range(npiece):
            if i + 1 < npiece:
                u_proj(i + 1)
            for half in range(group // PK_NKEYS):
                e = i * (group // PK_NKEYS) + half
                e1 = j * (te // PK_NKEYS) + e
                a = act_bufs[i % 2][half * PK_NKEYS:(half + 1) * PK_NKEYS, :]
                gelu = 0.5 * a * (1.0 + lax.erf(a * (1.0 / math.sqrt(2.0))))
                w = jnp.zeros(a.shape, BF16)
                for hd in range(PK_HEADS):
                    n = n1_ref[hd, pl.ds(e1, 1), :].astype(BF16)
                    c = c1_ref[hd, pl.ds(e1, 1), :].astype(BF16)
                    w = w + jnp.where(r2_ref[hd] < n, e2_ref[hd] * c, jnp.zeros_like(w))
                aw_build[e * PK_NKEYS:(e + 1) * PK_NKEYS, :] = (gelu * w.astype(F32)).astype(BF16)
            rs = slice(i * drows, (i + 1) * drows)
            part = _dot(vt_ref[rs, :], aw_drain[...])
            acc_s[rs, :] = jnp.where(restart, 0.0, acc_s[rs, :]) + part

    parity = lax.rem(s, 2)

    @pl.when(parity == 0)
    def _():
        step(aw0_s, aw1_s)

    @pl.when(parity == 1)
    def _():
        step(aw1_s, aw0_s)

    @pl.when(jnp.logical_and(jp == nst - 1, s > 0))
    def _():
        x2 = x1_ref[...] + acc_s[...].T
        ms = jnp.mean(x2 * x2, axis=-1, keepdims=True)
        o_ref[...] = x2 * lax.rsqrt(ms + NORM_EPS) * nw_ref[...]


def _peer(h2, u_bf16, vt_bf16, r2, e2, n1, c1, x1, final_w, tm=512, te=2048):
    T = h2.shape[0]
    ntile = T // tm
    nst = PK_EXPERTS // te
    tile_ab = lambda s: jnp.minimum(s // nst, ntile - 1)
    tile_c = lambda s: jnp.maximum(s - 1, 0) // nst
    rt = lambda: pl.BlockSpec((PK_HEADS, PK_NKEYS, tm), lambda s: (0, 0, tile_ab(s)))
    return pl.pallas_call(
        functools.partial(_peer_kernel, te=te, nst=nst),
        grid=(ntile * nst + 1,),
        in_specs=[
            pl.BlockSpec((tm, D_MODEL), lambda s: (tile_ab(s), 0)),
            pl.BlockSpec((te, D_MODEL), lambda s: (s % nst, 0)),
            pl.BlockSpec((D_MODEL, te), lambda s: (0, jnp.maximum(s - 1, 0) % nst)),
            rt(), rt(), rt(), rt(),
            pl.BlockSpec((tm, D_MODEL), lambda s: (tile_c(s), 0)),
            pl.BlockSpec((1, D_MODEL), lambda s: (0, 0)),
        ],
        out_specs=pl.BlockSpec((tm, D_MODEL), lambda s: (tile_c(s), 0)),
        out_shape=jax.ShapeDtypeStruct((T, D_MODEL), F32),
        scratch_shapes=[pltpu.VMEM((D_MODEL, tm), F32),
                        pltpu.VMEM((te, tm), BF16), pltpu.VMEM((te, tm), BF16),
                        pltpu.VMEM((2 * PK_NKEYS, tm), F32), pltpu.VMEM((2 * PK_NKEYS, tm), F32)],
        compiler_params=pltpu.CompilerParams(
            dimension_semantics=("arbitrary",), vmem_limit_bytes=VMEM_LIMIT),
        name="peer",
    )(h2, u_bf16, vt_bf16, r2, e2, n1, c1, x1, final_w)


def _lora_blockdiag(w2, a2):
    z = jnp.zeros_like(w2)
    return jnp.concatenate([jnp.concatenate([w2, z], axis=1), jnp.concatenate([z, a2], axis=1)], axis=0)


def kernel(x, norm_mix_w, w_in, shift_mu, w0, w2, a0, a2, g2, k_k, k_a, r_k, lnx_w, lnx_b, lam_q1, lam_k1, lam_q2, lam_k2, subln_w, proj_a, proj_b, w_out, norm_ffn_w, peer_wq, peer_keys, peer_u, peer_v, final_norm_w):
    B, S, D = x.shape
    T = B * S
    depth = w_in.shape[0]
    assert depth == 1 and D == D_MODEL
    l = 0
    x2 = x.reshape(T, D)
    zs, zqkv, zg = _inproj(x2, norm_mix_w[l][None], w_in[l].astype(BF16))
    ya = _rwkv(zs.reshape(B, S, SHIFT_COLS), shift_mu[l][None], w0[l][None], a0[l][None], k_k[l][None],
               k_a[l][None], r_k[l].reshape(1, RW_WIDTH), lnx_w[l][None], lnx_b[l][None],
               _lora_blockdiag(w2[l], a2[l]).astype(BF16), g2[l].astype(BF16))
    yb = _attn(zqkv.reshape(B, S, QKV_COLS), lam_q1[l][None], lam_k1[l][None], lam_q2[l][None],
               lam_k2[l][None], subln_w[l][None])
    x1, h2, q = _merge(ya.reshape(T, RW_WIDTH), yb.reshape(T, DA_WIDTH), zg, x2,
                       proj_a[l].astype(BF16), proj_b[l].astype(BF16), w_out[l].astype(BF16),
                       norm_ffn_w[l][None], peer_wq[l].astype(BF16))
    r2, e2, n1, c1 = _route(q, peer_keys[l].astype(BF16))
    out = _peer(h2, peer_u[l].astype(BF16), peer_v[l].T.astype(BF16), r2, e2, n1, c1, x1,
                final_norm_w[None])
    return out.reshape(B, S, D)
```

```python
import functools
import math

import jax
import jax.numpy as jnp
from jax import lax
from jax.experimental import pallas as pl
from jax.experimental.pallas import tpu as pltpu

F32 = jnp.float32
BF16 = jnp.bfloat16

D_MODEL = 1024
RW_HEADS = 8
RW_HEAD = 64
RW_WIDTH = RW_HEADS * RW_HEAD
DECAY_LORA = 64
ICLR_LORA = 64
GATE_LORA = 128
DA_HEADS = 4
DA_HEAD = 64
DA_VDIM = 2 * DA_HEAD
DA_QK_WIDTH = DA_HEADS * 2 * DA_HEAD
DA_WIDTH = DA_HEADS * DA_VDIM
PK_HEADS = 8
PK_NKEYS = 128
PK_QDIM = 256
PK_TOPK = 16
PK_EXPERTS = PK_NKEYS * PK_NKEYS
NORM_EPS = 1e-6
GN_EPS = 64e-5
SUBLN_EPS = 1e-5
SHIFT_COLS = 3 * RW_WIDTH + DECAY_LORA + ICLR_LORA + GATE_LORA
QKV_COLS = 2 * DA_QK_WIDTH + DA_WIDTH
GATE_COLS = 2 * D_MODEL
LAM_INIT = 0.8 - 0.6 * math.exp(0.0)

LANES = 128
CHUNK = 64
PAIR = 2 * RW_HEAD
ROW_CHUNK = 256
NEG_BIG = -1e30
VMEM_LIMIT = 56 * 1024 * 1024

NT_DIMS = (((1,), (1,)), ((), ()))
TN_DIMS = (((0,), (0,)), ((), ()))


def _dot(a, b):
    return jnp.dot(a, b, preferred_element_type=F32)


def _dot_nt(a, b):
    return lax.dot_general(a, b, NT_DIMS, preferred_element_type=F32)


def _dot_tn(a, b):
    return lax.dot_general(a, b, TN_DIMS, preferred_element_type=F32)


def _sigmoid(x):
    return 1.0 / (1.0 + jnp.exp(-x))


def _split_terms(x, terms):
    parts = []
    rest = x
    for i in range(terms):
        p = rest.astype(BF16)
        parts.append(p)
        if i + 1 < terms:
            rest = rest - p.astype(F32)
    return parts


def _dot_exact_rhs(a_bf16, x, terms=3):
    n = x.shape[1]
    y = _dot(a_bf16, jnp.concatenate(_split_terms(x, terms), axis=1))
    return sum(y[:, i * n:(i + 1) * n] for i in range(terms))


def _dot_exact_lhs(x, b_bf16, terms=2):
    m = x.shape[0]
    y = _dot(jnp.concatenate(_split_terms(x, terms), axis=0), b_bf16)
    return sum(y[i * m:(i + 1) * m] for i in range(terms))


def _inproj_kernel(x_ref, nw_ref, w_ref, zs_ref, zqkv_ref, zg_ref, *, col_chunk):
    x = x_ref[...]
    ms = jnp.mean(x * x, axis=-1, keepdims=True)
    h = (x * lax.rsqrt(ms + NORM_EPS) * nw_ref[...]).astype(BF16)
    c0 = 0
    for out_ref in (zs_ref, zqkv_ref, zg_ref):
        width = out_ref.shape[-1]
        for j in range(0, width, col_chunk):
            z = _dot(h, w_ref[:, c0 + j:c0 + j + col_chunk])
            out_ref[:, j:j + col_chunk] = z.astype(out_ref.dtype)
        c0 += width


def _inproj(x2, norm_w, w_in_bf16, tm=256, col_chunk=256):
    T = x2.shape[0]
    in_cols = w_in_bf16.shape[1]
    return pl.pallas_call(
        functools.partial(_inproj_kernel, col_chunk=col_chunk),
        grid=(T // tm,),
        in_specs=[
            pl.BlockSpec((tm, D_MODEL), lambda i: (i, 0)),
            pl.BlockSpec((1, D_MODEL), lambda i: (0, 0)),
            pl.BlockSpec((D_MODEL, in_cols), lambda i: (0, 0)),
        ],
        out_specs=[
            pl.BlockSpec((tm, SHIFT_COLS), lambda i: (i, 0)),
            pl.BlockSpec((tm, QKV_COLS), lambda i: (i, 0)),
            pl.BlockSpec((tm, GATE_COLS), lambda i: (i, 0)),
        ],
        out_shape=[
            jax.ShapeDtypeStruct((T, SHIFT_COLS), F32),
            jax.ShapeDtypeStruct((T, QKV_COLS), BF16),
            jax.ShapeDtypeStruct((T, GATE_COLS), F32),
        ],
        compiler_params=pltpu.CompilerParams(
            dimension_semantics=("parallel",), vmem_limit_bytes=VMEM_LIMIT),
        name="inproj",
    )(x2, norm_w, w_in_bf16)


def _rwkv_kernel(zs_ref, mu_ref, w0_ref, a0_ref, kk_ref, ka_ref, rk_ref, lnw_ref, lnb_ref,
                 wlora_ref, g2_ref, ones_ref, tri_ref, ya_ref,
                 state_ref, prev_ref, r_s, k_s, v_s, a_s, b_s, ld_s, l_s, y_s, *, ts, group):
    t = pl.program_id(1)

    @pl.when(t == 0)
    def _():
        state_ref[...] = jnp.zeros_like(state_ref)
        prev_ref[...] = jnp.zeros_like(prev_ref)

    z = zs_ref[0]
    row = lax.broadcasted_iota(jnp.int32, (ts, 1), 0)
    zprev = jnp.where(row == 0, prev_ref[...], pltpu.roll(z, 1, axis=0))
    prev_ref[...] = z[ts - 1:ts, :]
    zz = z + (zprev - z) * mu_ref[...]

    W = RW_WIDTH
    r = zz[:, 0:W]
    k = zz[:, W:2 * W]
    v = zz[:, 2 * W:3 * W]
    wa = zz[:, 3 * W:3 * W + LANES]
    gl = zz[:, 3 * W + LANES:3 * W + 2 * LANES]
    lane = lax.broadcasted_iota(jnp.int32, (1, LANES), 1)
    wa_act = jnp.where(lane < DECAY_LORA, jnp.tanh(wa), wa).astype(BF16)
    lora = _dot(wa_act, wlora_ref[...])
    u = -(w0_ref[...] + lora[:, 0:W])
    softplus = jnp.maximum(u, 0.0) + jnp.log1p(jnp.exp(-jnp.abs(u)))
    wlog = -softplus - 0.5
    ld = -jnp.exp(wlog)
    a = _sigmoid(a0_ref[...] + lora[:, W:2 * W])
    g = _dot(_sigmoid(gl).astype(BF16), g2_ref[...])

    ones_blk = ones_ref[...]
    kk = k * kk_ref[...]
    ss = _dot_exact_lhs(kk * kk, ones_blk)
    kk = kk / jnp.maximum(jnp.sqrt(ss), 1e-12)
    k2 = k * (1.0 + (a - 1.0) * ka_ref[...])
    bonus = _dot_exact_lhs(r * k2 * rk_ref[...], ones_blk) * v

    r_s[...] = r
    k_s[...] = k2
    v_s[...] = v
    a_s[...] = -kk
    b_s[...] = kk * a
    ld_s[...] = ld
    l_s[...] = _dot_exact_rhs(tri_ref[...], ld)

    rowi = lax.broadcasted_iota(jnp.int32, (PAIR, PAIR), 0)
    coli = lax.broadcasted_iota(jnp.int32, (PAIR, PAIR), 1)
    strict_lower = rowi > coli
    lower = rowi >= coli
    eye = rowi == coli
    head0 = lane < RW_HEAD

    def stack(xp):
        return jnp.concatenate([jnp.where(head0, xp, 0.0), jnp.where(head0, 0.0, xp)], axis=0)

    def unstack(xs):
        return xs[0:CHUNK] + xs[CHUNK:2 * CHUNK]

    mid = CHUNK // 2 - 1

    def chunk_terms(r0):
        rows = pl.ds(r0, CHUNK)
        lc = l_s[rows, :]
        cm = l_s[pl.ds(r0 + mid, 1), :]
        lend = l_s[pl.ds(r0 + CHUNK - 1, 1), :]
        e_pos = jnp.exp(lc - cm)
        e_neg = jnp.exp(cm - lc)
        e_cm = jnp.exp(cm)
        e_end = e_neg * jnp.exp(lend - cm)
        r_cen = r_s[rows, :] * e_pos
        a_cen = a_s[rows, :] * jnp.exp(lc - ld_s[rows, :] - cm)
        bc = b_s[rows, :]
        kc = k_s[rows, :]
        return dict(rows=rows, p_end=jnp.exp(lend), v=v_s[rows, :],
                    r_cen=r_cen, r_tru=r_cen * e_cm, a_cen=a_cen, a_tru=a_cen * e_cm,
                    b_cen=bc * e_neg, k_cen=kc * e_neg, b_end=bc * e_end, k_end=kc * e_end)

    def chunk_body(c, carry):
        terms = [chunk_terms(pl.multiple_of((c * group + ci) * CHUNK, CHUNK)) for ci in range(group)]
        items = [(t, slice(p * PAIR, (p + 1) * PAIR)) for t in terms for p in range(RW_HEADS // 2)]
        idx = range(len(items))
        sc = [_dot_nt(jnp.concatenate([stack(t["a_cen"][:, ls]), stack(t["r_cen"][:, ls])], axis=0).astype(BF16),
                      jnp.concatenate([stack(t["b_cen"][:, ls]), stack(t["k_cen"][:, ls])], axis=0).astype(BF16))
              for t, ls in items]
        a_ab = [jnp.where(strict_lower, s_[0:PAIR, 0:PAIR], 0.0) for s_ in sc]
        a_ak = [jnp.where(strict_lower, s_[0:PAIR, PAIR:2 * PAIR], 0.0).astype(BF16) for s_ in sc]
        m_rb = [jnp.where(lower, s_[PAIR:2 * PAIR, 0:PAIR], 0.0).astype(BF16) for s_ in sc]
        m_rk = [jnp.where(lower, s_[PAIR:2 * PAIR, PAIR:2 * PAIR], 0.0).astype(BF16) for s_ in sc]
        v_st = [stack(t["v"][:, ls]).astype(BF16) for t, ls in items]
        x = [jnp.concatenate([stack(t["a_tru"][:, ls]), _dot(a_ak[i], v_st[i])], axis=1)
             for i, (t, ls) in enumerate(items)]
        n = a_ab
        steps = int(math.log2(CHUNK))
        for k in range(steps):
            nb = [n_.astype(BF16) for n_ in n]
            x = [x[i] + _dot(nb[i], x[i].astype(BF16)) for i in idx]
            if k + 1 < steps:
                n = [_dot(nb_, nb_) for nb_ in nb]
        xb = [x_.astype(BF16) for x_ in x]
        ry = [_dot(m_rb[i], xb[i]) for i in idx]
        r_new = [unstack(stack(t["r_tru"][:, ls]) + ry[i][:, 0:PAIR]).astype(BF16)
                 for i, (t, ls) in enumerate(items)]
        y0 = [unstack(ry[i][:, PAIR:2 * PAIR] + _dot(m_rk[i], v_st[i])) for i in idx]
        b_st = [stack(t["b_end"][:, ls]).astype(BF16) for t, ls in items]
        k_st = [stack(t["k_end"][:, ls]).astype(BF16) for t, ls in items]
        gh = [_dot_tn(xb[i], b_st[i]) for i in idx]
        h_t = [gh[i][PAIR:2 * PAIR] + _dot_tn(v_st[i], k_st[i]) for i in idx]
        g_t = [gh[i][0:PAIR].astype(BF16) for i in idx]
        for i, (t, ls) in enumerate(items):
            p = i % (RW_HEADS // 2)
            s_old = state_ref[p]
            sb = s_old.astype(BF16)
            y_s[t["rows"], ls] = _dot_nt(r_new[i], sb) + y0[i]
            state_ref[p] = s_old * t["p_end"][:, ls] + _dot(sb, g_t[i]) + h_t[i]
        return carry

    lax.fori_loop(0, ts // (CHUNK * group), chunk_body, 0)

    y = y_s[...]
    inv_n = 1.0 / RW_HEAD
    mean = _dot_exact_lhs(y, ones_blk) * inv_n
    yc = y - mean
    var = _dot_exact_lhs(yc * yc, ones_blk) * inv_n
    yn = yc * lax.rsqrt(var + GN_EPS) * lnw_ref[...] + lnb_ref[...]
    ya_ref[0] = ((yn + bonus) * g).astype(ya_ref.dtype)


def _rwkv(zs3, mu, w0, a0, k_k, k_a, r_k, lnx_w, lnx_b, wlora, g2, ts=256):
    B, S, _ = zs3.shape
    W = RW_WIDTH
    ones_blk = (jnp.arange(W)[:, None] // RW_HEAD == jnp.arange(W)[None, :] // RW_HEAD).astype(BF16)
    ti = jnp.arange(ts)
    tri = ((ti[:, None] // CHUNK == ti[None, :] // CHUNK) & (ti[:, None] >= ti[None, :])).astype(BF16)
    vec = lambda n: pl.BlockSpec((1, n), lambda b, t: (0, 0))
    full = lambda a: pl.BlockSpec(a.shape, lambda b, t: (0,) * a.ndim)
    return pl.pallas_call(
        functools.partial(_rwkv_kernel, ts=ts, group=2),
        grid=(B, S // ts),
        in_specs=[
            pl.BlockSpec((1, ts, SHIFT_COLS), lambda b, t: (b, t, 0)),
            vec(SHIFT_COLS), vec(W), vec(W), vec(W), vec(W), vec(W), vec(W), vec(W),
            full(wlora), full(g2), full(ones_blk), full(tri),
        ],
        out_specs=pl.BlockSpec((1, ts, W), lambda b, t: (b, t, 0)),
        out_shape=jax.ShapeDtypeStruct((B, S, W), BF16),
        scratch_shapes=[
            pltpu.VMEM((RW_HEADS // 2, PAIR, PAIR), F32),
            pltpu.VMEM((1, SHIFT_COLS), F32),
        ] + [pltpu.VMEM((ts, W), F32) for _ in range(8)],
        compiler_params=pltpu.CompilerParams(
            dimension_semantics=("parallel", "arbitrary"), vmem_limit_bytes=VMEM_LIMIT),
        name="rwkv",
    )(zs3, mu, w0, a0, k_k, k_a, r_k, lnx_w, lnx_b, wlora, g2, ones_blk, tri)


def _attn_kernel(slope_ref, lq1_ref, lk1_ref, lq2_ref, lk2_ref, sw_ref, q_ref, k_ref, v_ref, o_ref,
                 m_s, l_s, acc_s, bias_s, *, tq, tk):
    h = pl.program_id(1)
    qi = pl.program_id(2)
    slope = slope_ref[h]
    lam = (jnp.exp(jnp.sum(lq1_ref[...] * lk1_ref[...], axis=-1, keepdims=True))
           - jnp.exp(jnp.sum(lq2_ref[...] * lk2_ref[...], axis=-1, keepdims=True)) + LAM_INIT)

    lane = lax.broadcasted_iota(jnp.int32, (1, LANES), 1)
    map0 = lane < DA_HEAD
    q = q_ref[0] * (1.0 / math.sqrt(DA_HEAD))
    zero = jnp.zeros_like(q)
    qst = jnp.concatenate([jnp.where(map0, q, zero), jnp.where(map0, zero, q)], axis=0)

    m_s[...] = jnp.full_like(m_s, NEG_BIG)
    l_s[...] = jnp.zeros_like(l_s)
    acc_s[...] = jnp.zeros_like(acc_s)

    @pl.when(qi == 0)
    def _():
        rowq = lax.broadcasted_iota(jnp.int32, (2 * tq, tk), 0)
        rowq = jnp.where(rowq >= tq, rowq - tq, rowq)
        rel = (rowq - lax.broadcasted_iota(jnp.int32, (2 * tq, tk), 1)).astype(F32)
        bias_s[...] = -slope * rel

    def block(j, diagonal):
        k0 = pl.multiple_of(j * tk, tk)
        kb = k_ref[0, pl.ds(k0, tk), :]
        vb = v_ref[0, pl.ds(k0, tk), :]
        shift = slope * ((qi - j) * tq).astype(F32)
        rows = [slice(i * ROW_CHUNK, (i + 1) * ROW_CHUNK) for i in range(2 * tq // ROW_CHUNK)]
        bias = [bias_s[r, :] for r in rows]
        s = [_dot_nt(qst[r], kb) + b for r, b in zip(rows, bias)]
        if diagonal:
            s = [jnp.where(b <= 0.0, x, NEG_BIG) for x, b in zip(s, bias)]
        m_old = [m_s[r, :] for r in rows]
        m_new = [jnp.maximum(mo, jnp.max(x, axis=1, keepdims=True) - shift) for mo, x in zip(m_old, s)]
        alpha = [jnp.exp(mo - mn) for mo, mn in zip(m_old, m_new)]
        p = [jnp.exp(x - (mn[:, 0:1] + shift)) for x, mn in zip(s, m_new)]
        pv = [_dot(x.astype(BF16), vb) for x in p]
        for r, a, x, mn, y in zip(rows, alpha, p, m_new, pv):
            l_s[r, :] = a * l_s[r, :] + jnp.sum(x, axis=1, keepdims=True)
            acc_s[r, :] = a * acc_s[r, :] + y
            m_s[r, :] = mn

    def body(j, carry):
        block(j, False)
        return carry

    lax.fori_loop(0, qi, body, 0)
    block(qi, True)

    o = acc_s[...] / l_s[...]
    o = o[0:tq] - lam * o[tq:2 * tq]
    o = o * lax.rsqrt(jnp.mean(o * o, axis=-1, keepdims=True) + SUBLN_EPS) * sw_ref[...]
    o_ref[0] = (o * (1.0 - LAM_INIT)).astype(o_ref.dtype)


def _attn(zqkv3, lam_q1, lam_k1, lam_q2, lam_k2, subln_w, tq=512, tk=512):
    B, S, _ = zqkv3.shape
    assert tq == tk, "the kernel masks only the diagonal block of aligned square tiles"
    slopes = jnp.asarray([2.0 ** (-8.0 * (i + 1) / DA_HEADS) for i in range(DA_HEADS)], F32)
    nqk = DA_QK_WIDTH // LANES
    vec = lambda n: pl.BlockSpec((1, n), lambda b, h, i: (0, 0))
    return pl.pallas_call(
        functools.partial(_attn_kernel, tq=tq, tk=tk),
        grid=(B, DA_HEADS, S // tq),
        in_specs=[
            pl.BlockSpec(memory_space=pltpu.SMEM),
            vec(DA_HEAD), vec(DA_HEAD), vec(DA_HEAD), vec(DA_HEAD), vec(DA_VDIM),
            pl.BlockSpec((1, tq, LANES), lambda b, h, i: (b, i, h)),
            pl.BlockSpec((1, S, LANES), lambda b, h, i: (b, 0, nqk + h)),
            pl.BlockSpec((1, S, LANES), lambda b, h, i: (b, 0, 2 * nqk + h)),
        ],
        out_specs=pl.BlockSpec((1, tq, LANES), lambda b, h, i: (b, i, h)),
        out_shape=jax.ShapeDtypeStruct((B, S, DA_WIDTH), BF16),
        scratch_shapes=[
            pltpu.VMEM((2 * tq, LANES), F32),
            pltpu.VMEM((2 * tq, LANES), F32),
            pltpu.VMEM((2 * tq, LANES), F32),
            pltpu.VMEM((2 * tq, tk), F32),
        ],
        compiler_params=pltpu.CompilerParams(
            dimension_semantics=("parallel", "parallel", "arbitrary"), vmem_limit_bytes=VMEM_LIMIT),
        name="attn",
    )(slopes, lam_q1, lam_k1, lam_q2, lam_k2, subln_w, zqkv3, zqkv3, zqkv3)


def _merge_kernel(ya_ref, yb_ref, zg_ref, x_ref, pa_ref, pb_ref, wo_ref, nw_ref, wq_ref,
                  x1_ref, h2t_ref, q_ref):
    pa = _dot(ya_ref[...], pa_ref[...])
    pb = _dot(yb_ref[...], pb_ref[...])
    ga = zg_ref[:, 0:D_MODEL]
    gb = zg_ref[:, D_MODEL:2 * D_MODEL]
    merged = _sigmoid(ga) * pa + _sigmoid(gb) * pb
    x1 = x_ref[...] + _dot(merged.astype(BF16), wo_ref[...])
    x1_ref[...] = x1
    ms = jnp.mean(x1 * x1, axis=-1, keepdims=True)
    h2 = x1 * lax.rsqrt(ms + NORM_EPS) * nw_ref[...]
    h2t_ref[...] = h2.T.astype(BF16)
    q_ref[...] = _dot(h2.astype(BF16), wq_ref[...]).astype(q_ref.dtype)


def _merge(ya2, yb2, zg, x2, proj_a, proj_b, w_out, norm_w, wq, tm=256):
    T = x2.shape[0]
    qw = wq.shape[1]
    row = lambda n: pl.BlockSpec((tm, n), lambda i: (i, 0))
    full = lambda a: pl.BlockSpec(a.shape, lambda i: (0,) * a.ndim)
    return pl.pallas_call(
        _merge_kernel,
        grid=(T // tm,),
        in_specs=[row(RW_WIDTH), row(DA_WIDTH), row(GATE_COLS), row(D_MODEL),
                  full(proj_a), full(proj_b), full(w_out), full(norm_w), full(wq)],
        out_specs=[row(D_MODEL), pl.BlockSpec((D_MODEL, tm), lambda i: (0, i)), row(qw)],
        out_shape=[
            jax.ShapeDtypeStruct((T, D_MODEL), F32),
            jax.ShapeDtypeStruct((D_MODEL, T), BF16),
            jax.ShapeDtypeStruct((T, qw), BF16),
        ],
        compiler_params=pltpu.CompilerParams(
            dimension_semantics=("parallel",), vmem_limit_bytes=VMEM_LIMIT),
        name="merge",
    )(ya2, yb2, zg, x2, proj_a, proj_b, w_out, norm_w, wq)


_STAIR = tuple(PK_TOPK // (i + 1) for i in range(PK_TOPK))


def _route_kernel(q_ref, keys_ref, r2_ref, e2_ref, n1_ref, c1_ref, v1_s, v2_s):
    half = PK_QDIM // 2
    q = q_ref[...]
    s1 = _dot_nt(keys_ref[0, 0], q[:, 0:half])
    s2 = _dot_nt(keys_ref[0, 1], q[:, half:2 * half])

    rank2 = jnp.full(s2.shape, float(PK_TOPK), F32)
    w1, w2 = s1, s2
    for i in range(PK_TOPK):
        m1 = jnp.max(w1, axis=0, keepdims=True)
        m2 = jnp.max(w2, axis=0, keepdims=True)
        v1_s[i:i + 1, :] = m1
        v2_s[i:i + 1, :] = m2
        w1 = jnp.where(w1 == m1, -jnp.inf, w1)
        hit2 = w2 == m2
        rank2 = jnp.where(hit2, float(i), rank2)
        w2 = jnp.where(hit2, -jnp.inf, w2)
    v1 = v1_s[...]
    v2 = v2_s[...]

    def stair(i, rows):
        jrow = lax.broadcasted_iota(jnp.int32, (rows, 1), 0)
        return jnp.where(jrow < _STAIR[i], v1[i:i + 1] + v2[0:rows], -jnp.inf)

    cand = jnp.concatenate([stair(i, PK_TOPK) for i in range(4)]
                           + [stair(i, 8) for i in range(4, 8)]
                           + [v1[8:PK_TOPK] + v2[0:1]], axis=0)
    work = cand
    tau = None
    for i in range(PK_TOPK):
        tau = jnp.max(work, axis=0, keepdims=True)
        work = jnp.where(work == tau, -jnp.inf, work)
    cmax = v1[0:1] + v2[0:1]
    z = jnp.sum(jnp.where(cand >= tau, jnp.exp(cand - cmax), 0.0), axis=0, keepdims=True)
    n1 = jnp.zeros_like(s1)
    for jj in range(PK_TOPK):
        n1 = jnp.where(s1 + v2[jj:jj + 1] >= tau, float(jj + 1), n1)
    r2_ref[0] = rank2.astype(r2_ref.dtype)
    e2_ref[0] = jnp.exp(s2 - v2[0:1]).astype(e2_ref.dtype)
    n1_ref[0] = n1
    c1_ref[0] = jnp.exp(s1 - v1[0:1]) * (1.0 / z)


def _route(q2, keys, tm=512):
    T = q2.shape[0]
    blk = lambda: pl.BlockSpec((1, PK_NKEYS, tm), lambda i, h: (h, 0, i))
    shp = lambda dt: jax.ShapeDtypeStruct((PK_HEADS, PK_NKEYS, T), dt)
    return pl.pallas_call(
        _route_kernel,
        grid=(T // tm, PK_HEADS),
        in_specs=[
            pl.BlockSpec((tm, PK_QDIM), lambda i, h: (i, h)),
            pl.BlockSpec((1, 2, PK_NKEYS, PK_QDIM // 2), lambda i, h: (h, 0, 0, 0)),
        ],
        out_specs=[blk(), blk(), blk(), blk()],
        out_shape=[shp(BF16), shp(BF16), shp(F32), shp(F32)],
        scratch_shapes=[pltpu.VMEM((PK_TOPK, tm), F32), pltpu.VMEM((PK_TOPK, tm), F32)],
        compiler_params=pltpu.CompilerParams(
            dimension_semantics=("parallel", "arbitrary"), vmem_limit_bytes=VMEM_LIMIT),
        name="route",
    )(q2, keys)


def _peer_kernel(ht_ref, u_ref, vt_ref, r2_ref, e2_ref, n1_ref, c1_ref, x1_ref, nw_ref, o_ref,
                 acc_s, aw0_s, aw1_s, act0_s, act1_s, *, te, nst):
    s = pl.program_id(0)
    j = lax.rem(s, jnp.int32(nst))
    jp = lax.rem(s + (nst - 1), jnp.int32(nst))

    @pl.when(s == 0)
    def _():
        acc_s[...] = jnp.zeros_like(acc_s)
        aw1_s[...] = jnp.zeros_like(aw1_s)

    group = 2 * PK_NKEYS
    npiece = te // group
    drows = D_MODEL // npiece
    restart = jp == 0
    act_bufs = (act0_s, act1_s)

    def u_proj(i):
        act_bufs[i % 2][...] = _dot(u_ref[i * group:(i + 1) * group, :], ht_ref[...])

    def step(aw_build, aw_drain):
        u_proj(0)
        for i in range(npiece):
            if i + 1 < npiece:
                u_proj(i + 1)
            for half in range(group // PK_NKEYS):
                e = i * (group // PK_NKEYS) + half
                e1 = j * (te // PK_NKEYS) + e
                a = act_bufs[i % 2][half * PK_NKEYS:(half + 1) * PK_NKEYS, :].astype(BF16)
                gelu = a * (0.5 + 0.5 * lax.erf(a * (1.0 / math.sqrt(2.0))))
                w = None
                for hd in range(PK_HEADS):
                    n = n1_ref[hd, pl.ds(e1, 1), :].astype(BF16)
                    c = c1_ref[hd, pl.ds(e1, 1), :].astype(BF16)
                    wh = jnp.where(r2_ref[hd] < n, e2_ref[hd] * c, jnp.zeros(a.shape, BF16))
                    w = wh if w is None else w + wh
                aw_build[e * PK_NKEYS:(e + 1) * PK_NKEYS, :] = gelu * w
            rs = slice(i * drows, (i + 1) * drows)
            part = _dot(vt_ref[rs, :], aw_drain[...])
            acc_s[rs, :] = jnp.where(restart, 0.0, acc_s[rs, :]) + part

    parity = lax.rem(s, jnp.int32(2))

    @pl.when(parity == 0)
    def _():
        step(aw0_s, aw1_s)

    @pl.when(parity == 1)
    def _():
        step(aw1_s, aw0_s)

    @pl.when(jnp.logical_and(jp == nst - 1, s > 0))
    def _():
        x2 = x1_ref[...] + acc_s[...].T
        ms = jnp.mean(x2 * x2, axis=-1, keepdims=True)
        o_ref[...] = x2 * lax.rsqrt(ms + NORM_EPS) * nw_ref[...]


def _peer(h2t, u_bf16, vt_bf16, r2, e2, n1, c1, x1, final_w, tm=512, te=2048):
    T = h2t.shape[1]
    ntile = T // tm
    nst = PK_EXPERTS // te
    tile_ab = lambda s: jnp.minimum(s // nst, ntile - 1)
    tile_c = lambda s: jnp.maximum(s - 1, 0) // nst
    rt = lambda: pl.BlockSpec((PK_HEADS, PK_NKEYS, tm), lambda s: (0, 0, tile_ab(s)))
    return pl.pallas_call(
        functools.partial(_peer_kernel, te=te, nst=nst),
        grid=(ntile * nst + 1,),
        in_specs=[
            pl.BlockSpec((D_MODEL, tm), lambda s: (0, tile_ab(s))),
            pl.BlockSpec((te, D_MODEL), lambda s: (s % nst, 0)),
            pl.BlockSpec((D_MODEL, te), lambda s: (0, jnp.maximum(s - 1, 0) % nst)),
            rt(), rt(), rt(), rt(),
            pl.BlockSpec((tm, D_MODEL), lambda s: (tile_c(s), 0)),
            pl.BlockSpec((1, D_MODEL), lambda s: (0, 0)),
        ],
        out_specs=pl.BlockSpec((tm, D_MODEL), lambda s: (tile_c(s), 0)),
        out_shape=jax.ShapeDtypeStruct((T, D_MODEL), F32),
        scratch_shapes=[pltpu.VMEM((D_MODEL, tm), F32),
                        pltpu.VMEM((te, tm), BF16), pltpu.VMEM((te, tm), BF16),
                        pltpu.VMEM((2 * PK_NKEYS, tm), F32), pltpu.VMEM((2 * PK_NKEYS, tm), F32)],
        compiler_params=pltpu.CompilerParams(
            dimension_semantics=("arbitrary",), vmem_limit_bytes=VMEM_LIMIT),
        name="peer",
    )(h2t, u_bf16, vt_bf16, r2, e2, n1, c1, x1, final_w)


def _lora_blockdiag(w2, a2):
    z = jnp.zeros_like(w2)
    return jnp.concatenate([jnp.concatenate([w2, z], axis=1), jnp.concatenate([z, a2], axis=1)], axis=0)


def kernel(x, norm_mix_w, w_in, shift_mu, w0, w2, a0, a2, g2, k_k, k_a, r_k, lnx_w, lnx_b, lam_q1, lam_k1, lam_q2, lam_k2, subln_w, proj_a, proj_b, w_out, norm_ffn_w, peer_wq, peer_keys, peer_u, peer_v, final_norm_w):
    B, S, D = x.shape
    T = B * S
    depth = w_in.shape[0]
    assert depth == 1 and D == D_MODEL
    l = 0
    x2 = x.reshape(T, D)
    zs, zqkv, zg = _inproj(x2, norm_mix_w[l][None], w_in[l].astype(BF16))
    ya = _rwkv(zs.reshape(B, S, SHIFT_COLS), shift_mu[l][None], w0[l][None], a0[l][None], k_k[l][None],
               k_a[l][None], r_k[l].reshape(1, RW_WIDTH), lnx_w[l][None], lnx_b[l][None],
               _lora_blockdiag(w2[l], a2[l]).astype(BF16), g2[l].astype(BF16))
    yb = _attn(zqkv.reshape(B, S, QKV_COLS), lam_q1[l][None], lam_k1[l][None], lam_q2[l][None],
               lam_k2[l][None], subln_w[l][None])
    x1, h2t, q = _merge(ya.reshape(T, RW_WIDTH), yb.reshape(T, DA_WIDTH), zg, x2,
                       proj_a[l].astype(BF16), proj_b[l].astype(BF16), w_out[l].astype(BF16),
                       norm_ffn_w[l][None], peer_wq[l].astype(BF16))
    r2, e2, n1, c1 = _route(q, peer_keys[l].astype(BF16))
    out = _peer(h2t, peer_u[l].astype(BF16), peer_v[l].T.astype(BF16), r2, e2, n1, c1, x1,
                final_norm_w[None])
    return out.reshape(B, S, D)
```

```python
import functools
import math

import jax
import jax.numpy as jnp
from jax import lax
from jax.experimental import pallas as pl
from jax.experimental.pallas import tpu as pltpu

F32 = jnp.float32
BF16 = jnp.bfloat16

D_MODEL = 1024
RW_HEADS = 8
RW_HEAD = 64
RW_WIDTH = RW_HEADS * RW_HEAD
DECAY_LORA = 64
ICLR_LORA = 64
GATE_LORA = 128
DA_HEADS = 4
DA_HEAD = 64
DA_VDIM = 2 * DA_HEAD
DA_QK_WIDTH = DA_HEADS * 2 * DA_HEAD
DA_WIDTH = DA_HEADS * DA_VDIM
PK_HEADS = 8
PK_NKEYS = 128
PK_QDIM = 256
PK_TOPK = 16
PK_EXPERTS = PK_NKEYS * PK_NKEYS
NORM_EPS = 1e-6
GN_EPS = 64e-5
SUBLN_EPS = 1e-5
SHIFT_COLS = 3 * RW_WIDTH + DECAY_LORA + ICLR_LORA + GATE_LORA
QKV_COLS = 2 * DA_QK_WIDTH + DA_WIDTH
GATE_COLS = 2 * D_MODEL
LAM_INIT = 0.8 - 0.6 * math.exp(0.0)

LANES = 128
CHUNK = 64
PAIR = 2 * RW_HEAD
ROW_CHUNK = 256
PEER_PIECE_ELEMS = 128 * 1024
PEER_TM = 512
NEG_BIG = -1e30
VMEM_LIMIT = 56 * 1024 * 1024

NT_DIMS = (((1,), (1,)), ((), ()))
TN_DIMS = (((0,), (0,)), ((), ()))


def _dot(a, b):
    return jnp.dot(a, b, preferred_element_type=F32)


def _dot_nt(a, b):
    return lax.dot_general(a, b, NT_DIMS, preferred_element_type=F32)


def _dot_tn(a, b):
    return lax.dot_general(a, b, TN_DIMS, preferred_element_type=F32)


def _sigmoid(x):
    return 1.0 / (1.0 + jnp.exp(-x))


def _split_terms(x, terms):
    parts = []
    rest = x
    for i in range(terms):
        p = rest.astype(BF16)
        parts.append(p)
        if i + 1 < terms:
            rest = rest - p.astype(F32)
    return parts


def _dot_exact_rhs(a_bf16, x, terms=3):
    n = x.shape[1]
    y = _dot(a_bf16, jnp.concatenate(_split_terms(x, terms), axis=1))
    return sum(y[:, i * n:(i + 1) * n] for i in range(terms))


def _dot_exact_lhs(x, b_bf16, terms=2):
    m = x.shape[0]
    y = _dot(jnp.concatenate(_split_terms(x, terms), axis=0), b_bf16)
    return sum(y[i * m:(i + 1) * m] for i in range(terms))


def _inproj_kernel(x_ref, nw_ref, w_ref, zs_ref, zqkv_ref, zg_ref, *, col_chunk):
    x = x_ref[...]
    ms = jnp.mean(x * x, axis=-1, keepdims=True)
    h = (x * lax.rsqrt(ms + NORM_EPS) * nw_ref[...]).astype(BF16)
    c0 = 0
    for out_ref in (zs_ref, zqkv_ref, zg_ref):
        width = out_ref.shape[-1]
        for j in range(0, width, col_chunk):
            z = _dot(h, w_ref[:, c0 + j:c0 + j + col_chunk])
            out_ref[:, j:j + col_chunk] = z.astype(out_ref.dtype)
        c0 += width


def _inproj(x2, norm_w, w_in_bf16, tm=256, col_chunk=256):
    T = x2.shape[0]
    in_cols = w_in_bf16.shape[1]
    return pl.pallas_call(
        functools.partial(_inproj_kernel, col_chunk=col_chunk),
        grid=(T // tm,),
        in_specs=[
            pl.BlockSpec((tm, D_MODEL), lambda i: (i, 0)),
            pl.BlockSpec((1, D_MODEL), lambda i: (0, 0)),
            pl.BlockSpec((D_MODEL, in_cols), lambda i: (0, 0)),
        ],
        out_specs=[
            pl.BlockSpec((tm, SHIFT_COLS), lambda i: (i, 0)),
            pl.BlockSpec((tm, QKV_COLS), lambda i: (i, 0)),
            pl.BlockSpec((tm, GATE_COLS), lambda i: (i, 0)),
        ],
        out_shape=[
            jax.ShapeDtypeStruct((T, SHIFT_COLS), F32),
            jax.ShapeDtypeStruct((T, QKV_COLS), BF16),
            jax.ShapeDtypeStruct((T, GATE_COLS), F32),
        ],
        compiler_params=pltpu.CompilerParams(
            dimension_semantics=("parallel",), vmem_limit_bytes=VMEM_LIMIT),
        name="inproj",
    )(x2, norm_w, w_in_bf16)


def _rwkv_kernel(zs_ref, mu_ref, w0_ref, a0_ref, kk_ref, ka_ref, rk_ref, lnw_ref, lnb_ref,
                 wlora_ref, g2_ref, ones_ref, tri_ref, ya_ref,
                 state_ref, prev_ref, r_s, k_s, v_s, a_s, b_s, ld_s, l_s, y_s, *, ts, group):
    t = pl.program_id(1)

    @pl.when(t == 0)
    def _():
        state_ref[...] = jnp.zeros_like(state_ref)
        prev_ref[...] = jnp.zeros_like(prev_ref)

    z = zs_ref[0]
    row = lax.broadcasted_iota(jnp.int32, (ts, 1), 0)
    zprev = jnp.where(row == 0, prev_ref[...], pltpu.roll(z, 1, axis=0))
    prev_ref[...] = z[ts - 1:ts, :]
    zz = z + (zprev - z) * mu_ref[...]

    W = RW_WIDTH
    r = zz[:, 0:W]
    k = zz[:, W:2 * W]
    v = zz[:, 2 * W:3 * W]
    wa = zz[:, 3 * W:3 * W + LANES]
    gl = zz[:, 3 * W + LANES:3 * W + 2 * LANES]
    lane = lax.broadcasted_iota(jnp.int32, (1, LANES), 1)
    wa_act = jnp.where(lane < DECAY_LORA, jnp.tanh(wa), wa).astype(BF16)
    lora = _dot(wa_act, wlora_ref[...])
    u = -(w0_ref[...] + lora[:, 0:W])
    softplus = jnp.maximum(u, 0.0) + jnp.log1p(jnp.exp(-jnp.abs(u)))
    wlog = -softplus - 0.5
    ld = -jnp.exp(wlog)
    a = _sigmoid(a0_ref[...] + lora[:, W:2 * W])
    g = _dot(_sigmoid(gl).astype(BF16), g2_ref[...])

    ones_blk = ones_ref[...]
    kk = k * kk_ref[...]
    ss = _dot_exact_lhs(kk * kk, ones_blk)
    kk = kk / jnp.maximum(jnp.sqrt(ss), 1e-12)
    k2 = k * (1.0 + (a - 1.0) * ka_ref[...])
    bonus = _dot_exact_lhs(r * k2 * rk_ref[...], ones_blk) * v

    r_s[...] = r
    k_s[...] = k2
    v_s[...] = v
    a_s[...] = -kk
    b_s[...] = kk * a
    ld_s[...] = ld
    l_s[...] = _dot_exact_rhs(tri_ref[...], ld)

    rowi = lax.broadcasted_iota(jnp.int32, (PAIR, PAIR), 0)
    coli = lax.broadcasted_iota(jnp.int32, (PAIR, PAIR), 1)
    strict_lower = rowi > coli
    lower = rowi >= coli
    eye = rowi == coli
    head0 = lane < RW_HEAD

    def stack(xp):
        return jnp.concatenate([jnp.where(head0, xp, 0.0), jnp.where(head0, 0.0, xp)], axis=0)

    def unstack(xs):
        return xs[0:CHUNK] + xs[CHUNK:2 * CHUNK]

    mid = CHUNK // 2 - 1

    def chunk_terms(r0):
        rows = pl.ds(r0, CHUNK)
        lc = l_s[rows, :]
        cm = l_s[pl.ds(r0 + mid, 1), :]
        lend = l_s[pl.ds(r0 + CHUNK - 1, 1), :]
        e_pos = jnp.exp(lc - cm)
        e_neg = jnp.exp(cm - lc)
        e_cm = jnp.exp(cm)
        e_end = e_neg * jnp.exp(lend - cm)
        r_cen = r_s[rows, :] * e_pos
        a_cen = a_s[rows, :] * jnp.exp(lc - ld_s[rows, :] - cm)
        bc = b_s[rows, :]
        kc = k_s[rows, :]
        return dict(rows=rows, p_end=jnp.exp(lend), v=v_s[rows, :],
                    r_cen=r_cen, r_tru=r_cen * e_cm, a_cen=a_cen, a_tru=a_cen * e_cm,
                    b_cen=bc * e_neg, k_cen=kc * e_neg, b_end=bc * e_end, k_end=kc * e_end)

    def chunk_body(c, carry):
        terms = [chunk_terms(pl.multiple_of((c * group + ci) * CHUNK, CHUNK)) for ci in range(group)]
        items = [(t, slice(p * PAIR, (p + 1) * PAIR)) for t in terms for p in range(RW_HEADS // 2)]
        idx = range(len(items))
        sc = [_dot_nt(jnp.concatenate([stack(t["a_cen"][:, ls]), stack(t["r_cen"][:, ls])], axis=0).astype(BF16),
                      jnp.concatenate([stack(t["b_cen"][:, ls]), stack(t["k_cen"][:, ls])], axis=0).astype(BF16))
              for t, ls in items]
        a_ab = [jnp.where(strict_lower, s_[0:PAIR, 0:PAIR], 0.0) for s_ in sc]
        a_ak = [jnp.where(strict_lower, s_[0:PAIR, PAIR:2 * PAIR], 0.0).astype(BF16) for s_ in sc]
        m_rb = [jnp.where(lower, s_[PAIR:2 * PAIR, 0:PAIR], 0.0).astype(BF16) for s_ in sc]
        m_rk = [jnp.where(lower, s_[PAIR:2 * PAIR, PAIR:2 * PAIR], 0.0).astype(BF16) for s_ in sc]
        v_st = [stack(t["v"][:, ls]).astype(BF16) for t, ls in items]
        x = [jnp.concatenate([stack(t["a_tru"][:, ls]), _dot(a_ak[i], v_st[i])], axis=1)
             for i, (t, ls) in enumerate(items)]
        n = a_ab
        steps = int(math.log2(CHUNK))
        for k in range(steps):
            nb = [n_.astype(BF16) for n_ in n]
            x = [x[i] + _dot(nb[i], x[i].astype(BF16)) for i in idx]
            if k + 1 < steps:
                n = [_dot(nb_, nb_) for nb_ in nb]
        xb = [x_.astype(BF16) for x_ in x]
        ry = [_dot(m_rb[i], xb[i]) for i in idx]
        r_new = [unstack(stack(t["r_tru"][:, ls]) + ry[i][:, 0:PAIR]).astype(BF16)
                 for i, (t, ls) in enumerate(items)]
        y0 = [unstack(ry[i][:, PAIR:2 * PAIR] + _dot(m_rk[i], v_st[i])) for i in idx]
        b_st = [stack(t["b_end"][:, ls]).astype(BF16) for t, ls in items]
        k_st = [stack(t["k_end"][:, ls]).astype(BF16) for t, ls in items]
        gh = [_dot_tn(xb[i], b_st[i]) for i in idx]
        h_t = [gh[i][PAIR:2 * PAIR] + _dot_tn(v_st[i], k_st[i]) for i in idx]
        g_t = [gh[i][0:PAIR].astype(BF16) for i in idx]
        for i, (t, ls) in enumerate(items):
            p = i % (RW_HEADS // 2)
            s_old = state_ref[p]
            sb = s_old.astype(BF16)
            y_s[t["rows"], ls] = _dot_nt(r_new[i], sb) + y0[i]
            state_ref[p] = s_old * t["p_end"][:, ls] + _dot(sb, g_t[i]) + h_t[i]
        return carry

    lax.fori_loop(0, ts // (CHUNK * group), chunk_body, 0)

    y = y_s[...]
    inv_n = 1.0 / RW_HEAD
    mean = _dot_exact_lhs(y, ones_blk) * inv_n
    yc = y - mean
    var = _dot_exact_lhs(yc * yc, ones_blk) * inv_n
    yn = yc * lax.rsqrt(var + GN_EPS) * lnw_ref[...] + lnb_ref[...]
    ya_ref[0] = ((yn + bonus) * g).astype(ya_ref.dtype)


def _rwkv(zs3, mu, w0, a0, k_k, k_a, r_k, lnx_w, lnx_b, wlora, g2, ts=256):
    B, S, _ = zs3.shape
    W = RW_WIDTH
    ones_blk = (jnp.arange(W)[:, None] // RW_HEAD == jnp.arange(W)[None, :] // RW_HEAD).astype(BF16)
    ti = jnp.arange(ts)
    tri = ((ti[:, None] // CHUNK == ti[None, :] // CHUNK) & (ti[:, None] >= ti[None, :])).astype(BF16)
    vec = lambda n: pl.BlockSpec((1, n), lambda b, t: (0, 0))
    full = lambda a: pl.BlockSpec(a.shape, lambda b, t: (0,) * a.ndim)
    return pl.pallas_call(
        functools.partial(_rwkv_kernel, ts=ts, group=2),
        grid=(B, S // ts),
        in_specs=[
            pl.BlockSpec((1, ts, SHIFT_COLS), lambda b, t: (b, t, 0)),
            vec(SHIFT_COLS), vec(W), vec(W), vec(W), vec(W), vec(W), vec(W), vec(W),
            full(wlora), full(g2), full(ones_blk), full(tri),
        ],
        out_specs=pl.BlockSpec((1, ts, W), lambda b, t: (b, t, 0)),
        out_shape=jax.ShapeDtypeStruct((B, S, W), BF16),
        scratch_shapes=[
            pltpu.VMEM((RW_HEADS // 2, PAIR, PAIR), F32),
            pltpu.VMEM((1, SHIFT_COLS), F32),
        ] + [pltpu.VMEM((ts, W), F32) for _ in range(8)],
        compiler_params=pltpu.CompilerParams(
            dimension_semantics=("parallel", "arbitrary"), vmem_limit_bytes=VMEM_LIMIT),
        name="rwkv",
    )(zs3, mu, w0, a0, k_k, k_a, r_k, lnx_w, lnx_b, wlora, g2, ones_blk, tri)


def _attn_kernel(slope_ref, lq1_ref, lk1_ref, lq2_ref, lk2_ref, sw_ref, q_ref, k_ref, v_ref, o_ref,
                 m_s, l_s, acc_s, bias_s, *, tq, tk):
    h = pl.program_id(1)
    qi = pl.program_id(2)
    slope = slope_ref[h]
    lam = (jnp.exp(jnp.sum(lq1_ref[...] * lk1_ref[...], axis=-1, keepdims=True))
           - jnp.exp(jnp.sum(lq2_ref[...] * lk2_ref[...], axis=-1, keepdims=True)) + LAM_INIT)

    lane = lax.broadcasted_iota(jnp.int32, (1, LANES), 1)
    map0 = lane < DA_HEAD
    q = q_ref[0] * (1.0 / math.sqrt(DA_HEAD))
    zero = jnp.zeros_like(q)
    qst = jnp.concatenate([jnp.where(map0, q, zero), jnp.where(map0, zero, q)], axis=0)

    m_s[...] = jnp.full_like(m_s, NEG_BIG)
    l_s[...] = jnp.zeros_like(l_s)
    acc_s[...] = jnp.zeros_like(acc_s)

    @pl.when(qi == 0)
    def _():
        rowq = lax.broadcasted_iota(jnp.int32, (2 * tq, tk), 0)
        rowq = jnp.where(rowq >= tq, rowq - tq, rowq)
        rel = (rowq - lax.broadcasted_iota(jnp.int32, (2 * tq, tk), 1)).astype(F32)
        bias_s[...] = -slope * rel

    def block(j, diagonal):
        k0 = pl.multiple_of(j * tk, tk)
        kb = k_ref[0, pl.ds(k0, tk), :]
        vb = v_ref[0, pl.ds(k0, tk), :]
        shift = slope * ((qi - j) * tq).astype(F32)
        rows = [slice(i * ROW_CHUNK, (i + 1) * ROW_CHUNK) for i in range(2 * tq // ROW_CHUNK)]
        bias = [bias_s[r, :] for r in rows]
        s = [_dot_nt(qst[r], kb) + b for r, b in zip(rows, bias)]
        if diagonal:
            s = [jnp.where(b <= 0.0, x, NEG_BIG) for x, b in zip(s, bias)]
        m_old = [m_s[r, :] for r in rows]
        m_new = [jnp.maximum(mo, jnp.max(x, axis=1, keepdims=True) - shift) for mo, x in zip(m_old, s)]
        alpha = [jnp.exp(mo - mn) for mo, mn in zip(m_old, m_new)]
        p = [jnp.exp(x - (mn[:, 0:1] + shift)) for x, mn in zip(s, m_new)]
        pv = [_dot(x.astype(BF16), vb) for x in p]
        for r, a, x, mn, y in zip(rows, alpha, p, m_new, pv):
            l_s[r, :] = a * l_s[r, :] + jnp.sum(x, axis=1, keepdims=True)
            acc_s[r, :] = a * acc_s[r, :] + y
            m_s[r, :] = mn

    def body(j, carry):
        block(j, False)
        return carry

    lax.fori_loop(0, qi, body, 0)
    block(qi, True)

    o = acc_s[...] / l_s[...]
    o = o[0:tq] - lam * o[tq:2 * tq]
    o = o * lax.rsqrt(jnp.mean(o * o, axis=-1, keepdims=True) + SUBLN_EPS) * sw_ref[...]
    o_ref[0] = (o * (1.0 - LAM_INIT)).astype(o_ref.dtype)


def _attn(zqkv3, lam_q1, lam_k1, lam_q2, lam_k2, subln_w, tq=512, tk=512):
    B, S, _ = zqkv3.shape
    assert tq == tk, "the kernel masks only the diagonal block of aligned square tiles"
    slopes = jnp.asarray([2.0 ** (-8.0 * (i + 1) / DA_HEADS) for i in range(DA_HEADS)], F32)
    nqk = DA_QK_WIDTH // LANES
    vec = lambda n: pl.BlockSpec((1, n), lambda b, h, i: (0, 0))
    return pl.pallas_call(
        functools.partial(_attn_kernel, tq=tq, tk=tk),
        grid=(B, DA_HEADS, S // tq),
        in_specs=[
            pl.BlockSpec(memory_space=pltpu.SMEM),
            vec(DA_HEAD), vec(DA_HEAD), vec(DA_HEAD), vec(DA_HEAD), vec(DA_VDIM),
            pl.BlockSpec((1, tq, LANES), lambda b, h, i: (b, i, h)),
            pl.BlockSpec((1, S, LANES), lambda b, h, i: (b, 0, nqk + h)),
            pl.BlockSpec((1, S, LANES), lambda b, h, i: (b, 0, 2 * nqk + h)),
        ],
        out_specs=pl.BlockSpec((1, tq, LANES), lambda b, h, i: (b, i, h)),
        out_shape=jax.ShapeDtypeStruct((B, S, DA_WIDTH), BF16),
        scratch_shapes=[
            pltpu.VMEM((2 * tq, LANES), F32),
            pltpu.VMEM((2 * tq, LANES), F32),
            pltpu.VMEM((2 * tq, LANES), F32),
            pltpu.VMEM((2 * tq, tk), F32),
        ],
        compiler_params=pltpu.CompilerParams(
            dimension_semantics=("parallel", "parallel", "arbitrary"), vmem_limit_bytes=VMEM_LIMIT),
        name="attn",
    )(slopes, lam_q1, lam_k1, lam_q2, lam_k2, subln_w, zqkv3, zqkv3, zqkv3)


def _merge_kernel(ya_ref, yb_ref, zg_ref, x_ref, pa_ref, pb_ref, wo_ref, nw_ref, wq_ref,
                  x1_ref, h2t_ref, q_ref):
    pa = _dot(ya_ref[...], pa_ref[...])
    pb = _dot(yb_ref[...], pb_ref[...])
    ga = zg_ref[:, 0:D_MODEL]
    gb = zg_ref[:, D_MODEL:2 * D_MODEL]
    merged = _sigmoid(ga) * pa + _sigmoid(gb) * pb
    x1 = x_ref[...] + _dot(merged.astype(BF16), wo_ref[...])
    x1_ref[...] = x1
    ms = jnp.mean(x1 * x1, axis=-1, keepdims=True)
    h2 = x1 * lax.rsqrt(ms + NORM_EPS) * nw_ref[...]
    h2t_ref[0] = h2.T.astype(BF16)
    q = _dot(h2.astype(BF16), wq_ref[...]).astype(q_ref.dtype)
    for hd in range(PK_HEADS):
        q_ref[hd] = q[:, hd * PK_QDIM:(hd + 1) * PK_QDIM]


def _merge(ya2, yb2, zg, x2, proj_a, proj_b, w_out, norm_w, wq, tm=PEER_TM):
    T = x2.shape[0]
    row = lambda n: pl.BlockSpec((tm, n), lambda i: (i, 0))
    full = lambda a: pl.BlockSpec(a.shape, lambda i: (0,) * a.ndim)
    return pl.pallas_call(
        _merge_kernel,
        grid=(T // tm,),
        in_specs=[row(RW_WIDTH), row(DA_WIDTH), row(GATE_COLS), row(D_MODEL),
                  full(proj_a), full(proj_b), full(w_out), full(norm_w), full(wq)],
        out_specs=[row(D_MODEL),
                   pl.BlockSpec((1, D_MODEL, tm), lambda i: (i, 0, 0)),
                   pl.BlockSpec((PK_HEADS, tm, PK_QDIM), lambda i: (0, i, 0))],
        out_shape=[
            jax.ShapeDtypeStruct((T, D_MODEL), F32),
            jax.ShapeDtypeStruct((T // tm, D_MODEL, tm), BF16),
            jax.ShapeDtypeStruct((PK_HEADS, T, PK_QDIM), BF16),
        ],
        compiler_params=pltpu.CompilerParams(
            dimension_semantics=("parallel",), vmem_limit_bytes=VMEM_LIMIT),
        name="merge",
    )(ya2, yb2, zg, x2, proj_a, proj_b, w_out, norm_w, wq)


_STAIR = tuple(PK_TOPK // (i + 1) for i in range(PK_TOPK))


def _route_kernel(q_ref, keys_ref, r2_ref, e2_ref, n1_ref, c1_ref, v1_s, v2_s):
    half = PK_QDIM // 2
    q = q_ref[0]
    s1 = _dot_nt(keys_ref[0, 0], q[:, 0:half])
    s2 = _dot_nt(keys_ref[0, 1], q[:, half:2 * half])

    rank2 = jnp.full(s2.shape, float(PK_TOPK), F32)
    w1, w2 = s1, s2
    for i in range(PK_TOPK):
        m1 = jnp.max(w1, axis=0, keepdims=True)
        m2 = jnp.max(w2, axis=0, keepdims=True)
        v1_s[i:i + 1, :] = m1
        v2_s[i:i + 1, :] = m2
        w1 = jnp.where(w1 == m1, -jnp.inf, w1)
        hit2 = w2 == m2
        rank2 = jnp.where(hit2, float(i), rank2)
        w2 = jnp.where(hit2, -jnp.inf, w2)
    v1 = v1_s[...]
    v2 = v2_s[...]

    def stair(i, rows):
        jrow = lax.broadcasted_iota(jnp.int32, (rows, 1), 0)
        return jnp.where(jrow < _STAIR[i], v1[i:i + 1] + v2[0:rows], -jnp.inf)

    cand = jnp.concatenate([stair(i, PK_TOPK) for i in range(4)]
                           + [stair(i, 8) for i in range(4, 8)]
                           + [v1[8:PK_TOPK] + v2[0:1]], axis=0)
    work = cand
    tau = None
    for i in range(PK_TOPK):
        tau = jnp.max(work, axis=0, keepdims=True)
        work = jnp.where(work == tau, -jnp.inf, work)
    cmax = v1[0:1] + v2[0:1]
    z = jnp.sum(jnp.where(cand >= tau, jnp.exp(cand - cmax), 0.0), axis=0, keepdims=True)
    n1 = jnp.zeros_like(s1)
    for jj in range(PK_TOPK):
        n1 = jnp.where(s1 + v2[jj:jj + 1] >= tau, float(jj + 1), n1)
    r2_ref[0, 0] = rank2.astype(r2_ref.dtype)
    e2_ref[0, 0] = jnp.exp(s2 - v2[0:1]).astype(e2_ref.dtype)
    n1_ref[0, 0] = n1
    c1_ref[0, 0] = jnp.exp(s1 - v1[0:1]) * (1.0 / z)


def _route(q3, keys, tm=PEER_TM):
    T = q3.shape[1]
    blk = lambda: pl.BlockSpec((1, 1, PK_NKEYS, tm), lambda i, h: (i, h, 0, 0))
    shp = lambda dt: jax.ShapeDtypeStruct((T // tm, PK_HEADS, PK_NKEYS, tm), dt)
    return pl.pallas_call(
        _route_kernel,
        grid=(T // tm, PK_HEADS),
        in_specs=[
            pl.BlockSpec((1, tm, PK_QDIM), lambda i, h: (h, i, 0)),
            pl.BlockSpec((1, 2, PK_NKEYS, PK_QDIM // 2), lambda i, h: (h, 0, 0, 0)),
        ],
        out_specs=[blk(), blk(), blk(), blk()],
        out_shape=[shp(BF16), shp(BF16), shp(F32), shp(F32)],
        scratch_shapes=[pltpu.VMEM((PK_TOPK, tm), F32), pltpu.VMEM((PK_TOPK, tm), F32)],
        compiler_params=pltpu.CompilerParams(
            dimension_semantics=("parallel", "arbitrary"), vmem_limit_bytes=VMEM_LIMIT),
        name="route",
    )(q3, keys)


def _peer_kernel(ht_ref, u_ref, vt_ref, r2_ref, e2_ref, n1_ref, c1_ref, x1_ref, nw_ref, o_ref,
                 acc_s, aw0_s, aw1_s, act0_s, act1_s, *, te, nst):
    s = pl.program_id(0)
    jp = lax.rem(s + (nst - 1), jnp.int32(nst))

    @pl.when(s == 0)
    def _():
        acc_s[...] = jnp.zeros_like(acc_s)
        aw1_s[...] = jnp.zeros_like(aw1_s)

    group = act0_s.shape[0]
    npiece = te // group
    drows = D_MODEL // npiece
    restart = jp == 0
    act_bufs = (act0_s, act1_s)

    def u_proj(i):
        act_bufs[i % 2][...] = _dot(u_ref[i * group:(i + 1) * group, :], ht_ref[0])

    def step(aw_build, aw_drain):
        u_proj(0)
        for i in range(npiece):
            if i + 1 < npiece:
                u_proj(i + 1)
            for half in range(group // PK_NKEYS):
                e = i * (group // PK_NKEYS) + half
                a = act_bufs[i % 2][half * PK_NKEYS:(half + 1) * PK_NKEYS, :].astype(BF16)
                gelu = a * (0.5 + 0.5 * lax.erf(a * (1.0 / math.sqrt(2.0))))
                w = None
                for hd in range(PK_HEADS):
                    n = n1_ref[0, hd, e:e + 1, :].astype(BF16)
                    c = c1_ref[0, hd, e:e + 1, :].astype(BF16)
                    wh = jnp.where(r2_ref[0, hd] < n, e2_ref[0, hd] * c, jnp.zeros(a.shape, BF16))
                    w = wh if w is None else w + wh
                aw_build[e * PK_NKEYS:(e + 1) * PK_NKEYS, :] = gelu * w
            rs = slice(i * drows, (i + 1) * drows)
            part = _dot(vt_ref[rs, :], aw_drain[...])
            acc_s[rs, :] = jnp.where(restart, 0.0, acc_s[rs, :]) + part

    parity = lax.rem(s, jnp.int32(2))

    @pl.when(parity == 0)
    def _():
        step(aw0_s, aw1_s)

    @pl.when(parity == 1)
    def _():
        step(aw1_s, aw0_s)

    @pl.when(jnp.logical_and(jp == nst - 1, s > 0))
    def _():
        x2 = x1_ref[...] + acc_s[...].T
        ms = jnp.mean(x2 * x2, axis=-1, keepdims=True)
        o_ref[...] = x2 * lax.rsqrt(ms + NORM_EPS) * nw_ref[...]


def _peer(h2t, u_bf16, vt_bf16, r2, e2, n1, c1, x1, final_w, te=2048):
    ntile, _, tm = h2t.shape
    T = ntile * tm
    group = max(PK_NKEYS, PEER_PIECE_ELEMS // tm)
    nst = PK_EXPERTS // te
    tile_ab = lambda s: jnp.minimum(s // nst, ntile - 1)
    tile_c = lambda s: jnp.maximum(s - 1, 0) // nst
    rt = lambda: pl.BlockSpec((1, PK_HEADS, PK_NKEYS, tm), lambda s: (tile_ab(s), 0, 0, 0))
    rows = lambda: pl.BlockSpec((1, PK_HEADS, te // PK_NKEYS, tm), lambda s: (tile_ab(s), 0, s % nst, 0))
    return pl.pallas_call(
        functools.partial(_peer_kernel, te=te, nst=nst),
        grid=(ntile * nst + 1,),
        in_specs=[
            pl.BlockSpec((1, D_MODEL, tm), lambda s: (tile_ab(s), 0, 0)),
            pl.BlockSpec((te, D_MODEL), lambda s: (s % nst, 0)),
            pl.BlockSpec((D_MODEL, te), lambda s: (0, jnp.maximum(s - 1, 0) % nst)),
            rt(), rt(), rows(), rows(),
            pl.BlockSpec((tm, D_MODEL), lambda s: (tile_c(s), 0)),
            pl.BlockSpec((1, D_MODEL), lambda s: (0, 0)),
        ],
        out_specs=pl.BlockSpec((tm, D_MODEL), lambda s: (tile_c(s), 0)),
        out_shape=jax.ShapeDtypeStruct((T, D_MODEL), F32),
        scratch_shapes=[pltpu.VMEM((D_MODEL, tm), F32),
                        pltpu.VMEM((te, tm), BF16), pltpu.VMEM((te, tm), BF16),
                        pltpu.VMEM((group, tm), F32), pltpu.VMEM((group, tm), F32)],
        compiler_params=pltpu.CompilerParams(
            dimension_semantics=("arbitrary",), vmem_limit_bytes=VMEM_LIMIT),
        name="peer",
    )(h2t, u_bf16, vt_bf16, r2, e2, n1, c1, x1, final_w)


def _lora_blockdiag(w2, a2):
    z = jnp.zeros_like(w2)
    return jnp.concatenate([jnp.concatenate([w2, z], axis=1), jnp.concatenate([z, a2], axis=1)], axis=0)


def kernel(x, norm_mix_w, w_in, shift_mu, w0, w2, a0, a2, g2, k_k, k_a, r_k, lnx_w, lnx_b, lam_q1, lam_k1, lam_q2, lam_k2, subln_w, proj_a, proj_b, w_out, norm_ffn_w, peer_wq, peer_keys, peer_u, peer_v, final_norm_w):
    B, S, D = x.shape
    T = B * S
    depth = w_in.shape[0]
    assert depth == 1 and D == D_MODEL
    l = 0
    x2 = x.reshape(T, D)
    zs, zqkv, zg = _inproj(x2, norm_mix_w[l][None], w_in[l].astype(BF16))
    ya = _rwkv(zs.reshape(B, S, SHIFT_COLS), shift_mu[l][None], w0[l][None], a0[l][None], k_k[l][None],
               k_a[l][None], r_k[l].reshape(1, RW_WIDTH), lnx_w[l][None], lnx_b[l][None],
               _lora_blockdiag(w2[l], a2[l]).astype(BF16), g2[l].astype(BF16))
    yb = _attn(zqkv.reshape(B, S, QKV_COLS), lam_q1[l][None], lam_k1[l][None], lam_q2[l][None],
               lam_k2[l][None], subln_w[l][None])
    x1, h2t, q = _merge(ya.reshape(T, RW_WIDTH), yb.reshape(T, DA_WIDTH), zg, x2,
                       proj_a[l].astype(BF16), proj_b[l].astype(BF16), w_out[l].astype(BF16),
                       norm_ffn_w[l][None], peer_wq[l].astype(BF16))
    r2, e2, n1, c1 = _route(q, peer_keys[l].astype(BF16))
    out = _peer(h2t, peer_u[l].astype(BF16), peer_v[l].T.astype(BF16), r2, e2, n1, c1, x1,
                final_norm_w[None])
    return out.reshape(B, S, D)
```

```python
import functools
import math

import jax
import jax.numpy as jnp
from jax import lax
from jax.experimental import pallas as pl
from jax.experimental.pallas import tpu as pltpu

F32 = jnp.float32
BF16 = jnp.bfloat16

D_MODEL = 1024
RW_HEADS = 8
RW_HEAD = 64
RW_WIDTH = RW_HEADS * RW_HEAD
DECAY_LORA = 64
ICLR_LORA = 64
GATE_LORA = 128
DA_HEADS = 4
DA_HEAD = 64
DA_VDIM = 2 * DA_HEAD
DA_QK_WIDTH = DA_HEADS * 2 * DA_HEAD
DA_WIDTH = DA_HEADS * DA_VDIM
PK_HEADS = 8
PK_NKEYS = 128
PK_QDIM = 256
PK_TOPK = 16
PK_EXPERTS = PK_NKEYS * PK_NKEYS
NORM_EPS = 1e-6
GN_EPS = 64e-5
SUBLN_EPS = 1e-5
SHIFT_COLS = 3 * RW_WIDTH + DECAY_LORA + ICLR_LORA + GATE_LORA
QKV_COLS = 2 * DA_QK_WIDTH + DA_WIDTH
GATE_COLS = 2 * D_MODEL
LAM_INIT = 0.8 - 0.6 * math.exp(0.0)

LANES = 128
CHUNK = 64
PAIR = 2 * RW_HEAD
PEER_PIECE_ELEMS = 128 * 1024
PEER_TM = 512
NEG_BIG = -1e30
VMEM_LIMIT = 56 * 1024 * 1024

NT_DIMS = (((1,), (1,)), ((), ()))
TN_DIMS = (((0,), (0,)), ((), ()))


def _dot(a, b):
    return jnp.dot(a, b, preferred_element_type=F32)


def _dot_nt(a, b):
    return lax.dot_general(a, b, NT_DIMS, preferred_element_type=F32)


def _dot_tn(a, b):
    return lax.dot_general(a, b, TN_DIMS, preferred_element_type=F32)


def _sigmoid(x):
    return 1.0 / (1.0 + jnp.exp(-x))


def _split_terms(x, terms):
    parts = []
    rest = x
    for i in range(terms):
        p = rest.astype(BF16)
        parts.append(p)
        if i + 1 < terms:
            rest = rest - p.astype(F32)
    return parts


def _dot_exact_rhs(a_bf16, x, terms=3):
    n = x.shape[1]
    y = _dot(a_bf16, jnp.concatenate(_split_terms(x, terms), axis=1))
    return sum(y[:, i * n:(i + 1) * n] for i in range(terms))


def _dot_exact_lhs(x, b_bf16, terms=2):
    m = x.shape[0]
    y = _dot(jnp.concatenate(_split_terms(x, terms), axis=0), b_bf16)
    return sum(y[i * m:(i + 1) * m] for i in range(terms))


def _inproj_kernel(x_ref, nw_ref, w_ref, zs_ref, zqkv_ref, zg_ref, *, col_chunk):
    x = x_ref[...]
    ms = jnp.mean(x * x, axis=-1, keepdims=True)
    h = (x * lax.rsqrt(ms + NORM_EPS) * nw_ref[...]).astype(BF16)
    c0 = 0
    for out_ref in (zs_ref, zqkv_ref, zg_ref):
        width = out_ref.shape[-1]
        for j in range(0, width, col_chunk):
            z = _dot(h, w_ref[:, c0 + j:c0 + j + col_chunk])
            out_ref[:, j:j + col_chunk] = z.astype(out_ref.dtype)
        c0 += width


def _inproj(x2, norm_w, w_in_bf16, tm=256, col_chunk=256):
    T = x2.shape[0]
    in_cols = w_in_bf16.shape[1]
    return pl.pallas_call(
        functools.partial(_inproj_kernel, col_chunk=col_chunk),
        grid=(T // tm,),
        in_specs=[
            pl.BlockSpec((tm, D_MODEL), lambda i: (i, 0)),
            pl.BlockSpec((1, D_MODEL), lambda i: (0, 0)),
            pl.BlockSpec((D_MODEL, in_cols), lambda i: (0, 0)),
        ],
        out_specs=[
            pl.BlockSpec((tm, SHIFT_COLS), lambda i: (i, 0)),
            pl.BlockSpec((tm, QKV_COLS), lambda i: (i, 0)),
            pl.BlockSpec((tm, GATE_COLS), lambda i: (i, 0)),
        ],
        out_shape=[
            jax.ShapeDtypeStruct((T, SHIFT_COLS), F32),
            jax.ShapeDtypeStruct((T, QKV_COLS), BF16),
            jax.ShapeDtypeStruct((T, GATE_COLS), F32),
        ],
        compiler_params=pltpu.CompilerParams(
            dimension_semantics=("parallel",), vmem_limit_bytes=VMEM_LIMIT),
        name="inproj",
    )(x2, norm_w, w_in_bf16)


def _rwkv_kernel(zs_ref, mu_ref, w0_ref, a0_ref, kk_ref, ka_ref, rk_ref, lnw_ref, lnb_ref,
                 wlora_ref, g2_ref, ones_ref, tri_ref, ya_ref,
                 state_ref, prev_ref, r_s, k_s, v_s, a_s, b_s, ld_s, l_s, y_s, *, ts, group):
    t = pl.program_id(1)

    @pl.when(t == 0)
    def _():
        state_ref[...] = jnp.zeros_like(state_ref)
        prev_ref[...] = jnp.zeros_like(prev_ref)

    z = zs_ref[0]
    row = lax.broadcasted_iota(jnp.int32, (ts, 1), 0)
    zprev = jnp.where(row == 0, prev_ref[...], pltpu.roll(z, 1, axis=0))
    prev_ref[...] = z[ts - 1:ts, :]
    zz = z + (zprev - z) * mu_ref[...]

    W = RW_WIDTH
    r = zz[:, 0:W]
    k = zz[:, W:2 * W]
    v = zz[:, 2 * W:3 * W]
    wa = zz[:, 3 * W:3 * W + LANES]
    gl = zz[:, 3 * W + LANES:3 * W + 2 * LANES]
    lane = lax.broadcasted_iota(jnp.int32, (1, LANES), 1)
    wa_act = jnp.where(lane < DECAY_LORA, jnp.tanh(wa), wa).astype(BF16)
    lora = _dot(wa_act, wlora_ref[...])
    u = -(w0_ref[...] + lora[:, 0:W])
    softplus = jnp.maximum(u, 0.0) + jnp.log1p(jnp.exp(-jnp.abs(u)))
    wlog = -softplus - 0.5
    ld = -jnp.exp(wlog)
    a = _sigmoid(a0_ref[...] + lora[:, W:2 * W])
    g = _dot(_sigmoid(gl).astype(BF16), g2_ref[...])

    ones_blk = ones_ref[...]
    kk = k * kk_ref[...]
    ss = _dot_exact_lhs(kk * kk, ones_blk)
    kk = kk / jnp.maximum(jnp.sqrt(ss), 1e-12)
    k2 = k * (1.0 + (a - 1.0) * ka_ref[...])
    bonus = _dot_exact_lhs(r * k2 * rk_ref[...], ones_blk) * v

    r_s[...] = r
    k_s[...] = k2
    v_s[...] = v
    a_s[...] = -kk
    b_s[...] = kk * a
    ld_s[...] = ld
    l_s[...] = _dot_exact_rhs(tri_ref[...], ld)

    rowi = lax.broadcasted_iota(jnp.int32, (PAIR, PAIR), 0)
    coli = lax.broadcasted_iota(jnp.int32, (PAIR, PAIR), 1)
    strict_lower = rowi > coli
    lower = rowi >= coli
    eye = rowi == coli
    head0 = lane < RW_HEAD

    def stack(xp):
        return jnp.concatenate([jnp.where(head0, xp, 0.0), jnp.where(head0, 0.0, xp)], axis=0)

    def unstack(xs):
        return xs[0:CHUNK] + xs[CHUNK:2 * CHUNK]

    mid = CHUNK // 2 - 1

    def chunk_terms(r0):
        rows = pl.ds(r0, CHUNK)
        lc = l_s[rows, :]
        cm = l_s[pl.ds(r0 + mid, 1), :]
        lend = l_s[pl.ds(r0 + CHUNK - 1, 1), :]
        e_pos = jnp.exp(lc - cm)
        e_neg = jnp.exp(cm - lc)
        e_cm = jnp.exp(cm)
        e_end = e_neg * jnp.exp(lend - cm)
        r_cen = r_s[rows, :] * e_pos
        a_cen = a_s[rows, :] * jnp.exp(lc - ld_s[rows, :] - cm)
        bc = b_s[rows, :]
        kc = k_s[rows, :]
        return dict(rows=rows, p_end=jnp.exp(lend), v=v_s[rows, :],
                    r_cen=r_cen, r_tru=r_cen * e_cm, a_cen=a_cen, a_tru=a_cen * e_cm,
                    b_cen=bc * e_neg, k_cen=kc * e_neg, b_end=bc * e_end, k_end=kc * e_end)

    def chunk_body(c, carry):
        terms = [chunk_terms(pl.multiple_of((c * group + ci) * CHUNK, CHUNK)) for ci in range(group)]
        items = [(t, slice(p * PAIR, (p + 1) * PAIR)) for t in terms for p in range(RW_HEADS // 2)]
        idx = range(len(items))
        sc = [_dot_nt(jnp.concatenate([stack(t["a_cen"][:, ls]), stack(t["r_cen"][:, ls])], axis=0).astype(BF16),
                      jnp.concatenate([stack(t["b_cen"][:, ls]), stack(t["k_cen"][:, ls])], axis=0).astype(BF16))
              for t, ls in items]
        a_ab = [jnp.where(strict_lower, s_[0:PAIR, 0:PAIR], 0.0) for s_ in sc]
        a_ak = [jnp.where(strict_lower, s_[0:PAIR, PAIR:2 * PAIR], 0.0).astype(BF16) for s_ in sc]
        m_rb = [jnp.where(lower, s_[PAIR:2 * PAIR, 0:PAIR], 0.0).astype(BF16) for s_ in sc]
        m_rk = [jnp.where(lower, s_[PAIR:2 * PAIR, PAIR:2 * PAIR], 0.0).astype(BF16) for s_ in sc]
        v_st = [stack(t["v"][:, ls]).astype(BF16) for t, ls in items]
        x = [jnp.concatenate([stack(t["a_tru"][:, ls]), _dot(a_ak[i], v_st[i])], axis=1)
             for i, (t, ls) in enumerate(items)]
        n = a_ab
        steps = int(math.log2(CHUNK))
        for k in range(steps):
            nb = [n_.astype(BF16) for n_ in n]
            x = [x[i] + _dot(nb[i], x[i].astype(BF16)) for i in idx]
            if k + 1 < steps:
                n = [_dot(nb_, nb_) for nb_ in nb]
        xb = [x_.astype(BF16) for x_ in x]
        ry = [_dot(m_rb[i], xb[i]) for i in idx]
        r_new = [unstack(stack(t["r_tru"][:, ls]) + ry[i][:, 0:PAIR]).astype(BF16)
                 for i, (t, ls) in enumerate(items)]
        y0 = [unstack(ry[i][:, PAIR:2 * PAIR] + _dot(m_rk[i], v_st[i])) for i in idx]
        b_st = [stack(t["b_end"][:, ls]).astype(BF16) for t, ls in items]
        k_st = [stack(t["k_end"][:, ls]).astype(BF16) for t, ls in items]
        gh = [_dot_tn(xb[i], b_st[i]) for i in idx]
        h_t = [gh[i][PAIR:2 * PAIR] + _dot_tn(v_st[i], k_st[i]) for i in idx]
        g_t = [gh[i][0:PAIR].astype(BF16) for i in idx]
        for i, (t, ls) in enumerate(items):
            p = i % (RW_HEADS // 2)
            s_old = state_ref[p]
            sb = s_old.astype(BF16)
            y_s[t["rows"], ls] = _dot_nt(r_new[i], sb) + y0[i]
            state_ref[p] = s_old * t["p_end"][:, ls] + _dot(sb, g_t[i]) + h_t[i]
        return carry

    lax.fori_loop(0, ts // (CHUNK * group), chunk_body, 0)

    y = y_s[...]
    inv_n = 1.0 / RW_HEAD
    mean = _dot_exact_lhs(y, ones_blk) * inv_n
    yc = y - mean
    var = _dot_exact_lhs(yc * yc, ones_blk) * inv_n
    yn = yc * lax.rsqrt(var + GN_EPS) * lnw_ref[...] + lnb_ref[...]
    ya_ref[0] = ((yn + bonus) * g).astype(ya_ref.dtype)


def _rwkv(zs3, mu, w0, a0, k_k, k_a, r_k, lnx_w, lnx_b, wlora, g2, ts=256):
    B, S, _ = zs3.shape
    W = RW_WIDTH
    ones_blk = (jnp.arange(W)[:, None] // RW_HEAD == jnp.arange(W)[None, :] // RW_HEAD).astype(BF16)
    ti = jnp.arange(ts)
    tri = ((ti[:, None] // CHUNK == ti[None, :] // CHUNK) & (ti[:, None] >= ti[None, :])).astype(BF16)
    vec = lambda n: pl.BlockSpec((1, n), lambda b, t: (0, 0))
    full = lambda a: pl.BlockSpec(a.shape, lambda b, t: (0,) * a.ndim)
    return pl.pallas_call(
        functools.partial(_rwkv_kernel, ts=ts, group=2),
        grid=(B, S // ts),
        in_specs=[
            pl.BlockSpec((1, ts, SHIFT_COLS), lambda b, t: (b, t, 0)),
            vec(SHIFT_COLS), vec(W), vec(W), vec(W), vec(W), vec(W), vec(W), vec(W),
            full(wlora), full(g2), full(ones_blk), full(tri),
        ],
        out_specs=pl.BlockSpec((1, ts, W), lambda b, t: (b, t, 0)),
        out_shape=jax.ShapeDtypeStruct((B, S, W), BF16),
        scratch_shapes=[
            pltpu.VMEM((RW_HEADS // 2, PAIR, PAIR), F32),
            pltpu.VMEM((1, SHIFT_COLS), F32),
        ] + [pltpu.VMEM((ts, W), F32) for _ in range(8)],
        compiler_params=pltpu.CompilerParams(
            dimension_semantics=("parallel", "arbitrary"), vmem_limit_bytes=VMEM_LIMIT),
        name="rwkv",
    )(zs3, mu, w0, a0, k_k, k_a, r_k, lnx_w, lnx_b, wlora, g2, ones_blk, tri)


def _attn_kernel(slope_ref, lq1_ref, lk1_ref, lq2_ref, lk2_ref, sw_ref, q_ref, k_ref, v_ref, o_ref,
                 m_s, l_s, acc_s, bias_s, *, tq, tk):
    h = pl.program_id(1)
    qi = pl.program_id(2)
    slope = slope_ref[h]
    lam = (jnp.exp(jnp.sum(lq1_ref[...] * lk1_ref[...], axis=-1, keepdims=True))
           - jnp.exp(jnp.sum(lq2_ref[...] * lk2_ref[...], axis=-1, keepdims=True)) + LAM_INIT)

    lane = lax.broadcasted_iota(jnp.int32, (1, LANES), 1)
    map0 = lane < DA_HEAD
    q = q_ref[0] * (1.0 / math.sqrt(DA_HEAD))
    zero = jnp.zeros_like(q)
    qst = jnp.concatenate([jnp.where(map0, q, zero), jnp.where(map0, zero, q)], axis=0)

    m_s[...] = jnp.full_like(m_s, NEG_BIG)
    l_s[...] = jnp.zeros_like(l_s)
    acc_s[...] = jnp.zeros_like(acc_s)

    @pl.when(qi == 0)
    def _():
        krow = lax.broadcasted_iota(jnp.int32, (tk, 2 * tq), 0)
        qcol = lax.broadcasted_iota(jnp.int32, (tk, 2 * tq), 1)
        qcol = jnp.where(qcol >= tq, qcol - tq, qcol)
        bias_s[...] = -slope * (qcol - krow).astype(F32)

    def block(j, diagonal):
        k0 = pl.multiple_of(j * tk, tk)
        kb = k_ref[0, pl.ds(k0, tk), :]
        vb = v_ref[0, pl.ds(k0, tk), :]
        shift = slope * ((qi - j) * tq).astype(F32)
        b = bias_s[...]
        x = _dot_nt(kb, qst) + b
        if diagonal:
            x = jnp.where(b <= 0.0, x, NEG_BIG)
        m_old = m_s[...]
        m_new = jnp.maximum(m_old, jnp.max(x, axis=0, keepdims=True) - shift)
        alpha = jnp.exp(m_old - m_new)
        p = jnp.exp(x - (m_new + shift))
        l_s[...] = alpha * l_s[...] + jnp.sum(p, axis=0, keepdims=True)
        acc_s[...] = alpha * acc_s[...] + _dot_tn(vb, p.astype(BF16))
        m_s[...] = m_new

    def body(j, carry):
        block(j, False)
        return carry

    lax.fori_loop(0, qi, body, 0)
    block(qi, True)

    o = acc_s[...] / l_s[...]
    o = o[:, 0:tq] - lam * o[:, tq:2 * tq]
    o = o * lax.rsqrt(jnp.mean(o * o, axis=0, keepdims=True) + SUBLN_EPS) * sw_ref[...]
    o_ref[0] = (o * (1.0 - LAM_INIT)).T.astype(o_ref.dtype)


def _attn(zqkv3, lam_q1, lam_k1, lam_q2, lam_k2, subln_w, tq=512, tk=512):
    B, S, _ = zqkv3.shape
    assert tq == tk, "the kernel masks only the diagonal block of aligned square tiles"
    slopes = jnp.asarray([2.0 ** (-8.0 * (i + 1) / DA_HEADS) for i in range(DA_HEADS)], F32)
    nqk = DA_QK_WIDTH // LANES
    vec = lambda n: pl.BlockSpec((1, n), lambda b, h, i: (0, 0))
    return pl.pallas_call(
        functools.partial(_attn_kernel, tq=tq, tk=tk),
        grid=(B, DA_HEADS, S // tq),
        in_specs=[
            pl.BlockSpec(memory_space=pltpu.SMEM),
            vec(DA_HEAD), vec(DA_HEAD), vec(DA_HEAD), vec(DA_HEAD),
            pl.BlockSpec((DA_VDIM, 1), lambda b, h, i: (0, 0)),
            pl.BlockSpec((1, tq, LANES), lambda b, h, i: (b, i, h)),
            pl.BlockSpec((1, S, LANES), lambda b, h, i: (b, 0, nqk + h)),
            pl.BlockSpec((1, S, LANES), lambda b, h, i: (b, 0, 2 * nqk + h)),
        ],
        out_specs=pl.BlockSpec((1, tq, LANES), lambda b, h, i: (b, i, h)),
        out_shape=jax.ShapeDtypeStruct((B, S, DA_WIDTH), BF16),
        scratch_shapes=[
            pltpu.VMEM((1, 2 * tq), F32),
            pltpu.VMEM((1, 2 * tq), F32),
            pltpu.VMEM((DA_VDIM, 2 * tq), F32),
            pltpu.VMEM((tk, 2 * tq), F32),
        ],
        compiler_params=pltpu.CompilerParams(
            dimension_semantics=("parallel", "parallel", "arbitrary"), vmem_limit_bytes=VMEM_LIMIT),
        name="attn",
    )(slopes, lam_q1, lam_k1, lam_q2, lam_k2, subln_w, zqkv3, zqkv3, zqkv3)


def _merge_kernel(ya_ref, yb_ref, zg_ref, x_ref, pa_ref, pb_ref, wo_ref, nw_ref, wq_ref,
                  x1_ref, h2t_ref, q_ref):
    pa = _dot(ya_ref[...], pa_ref[...])
    pb = _dot(yb_ref[...], pb_ref[...])
    ga = zg_ref[:, 0:D_MODEL]
    gb = zg_ref[:, D_MODEL:2 * D_MODEL]
    merged = _sigmoid(ga) * pa + _sigmoid(gb) * pb
    x1 = x_ref[...] + _dot(merged.astype(BF16), wo_ref[...])
    x1_ref[...] = x1
    ms = jnp.mean(x1 * x1, axis=-1, keepdims=True)
    h2 = x1 * lax.rsqrt(ms + NORM_EPS) * nw_ref[...]
    h2t_ref[0] = h2.T.astype(BF16)
    q = _dot(h2.astype(BF16), wq_ref[...]).astype(q_ref.dtype)
    for hd in range(PK_HEADS):
        q_ref[hd] = q[:, hd * PK_QDIM:(hd + 1) * PK_QDIM]


def _merge(ya2, yb2, zg, x2, proj_a, proj_b, w_out, norm_w, wq, tm=PEER_TM):
    T = x2.shape[0]
    row = lambda n: pl.BlockSpec((tm, n), lambda i: (i, 0))
    full = lambda a: pl.BlockSpec(a.shape, lambda i: (0,) * a.ndim)
    return pl.pallas_call(
        _merge_kernel,
        grid=(T // tm,),
        in_specs=[row(RW_WIDTH), row(DA_WIDTH), row(GATE_COLS), row(D_MODEL),
                  full(proj_a), full(proj_b), full(w_out), full(norm_w), full(wq)],
        out_specs=[row(D_MODEL),
                   pl.BlockSpec((1, D_MODEL, tm), lambda i: (i, 0, 0)),
                   pl.BlockSpec((PK_HEADS, tm, PK_QDIM), lambda i: (0, i, 0))],
        out_shape=[
            jax.ShapeDtypeStruct((T, D_MODEL), F32),
            jax.ShapeDtypeStruct((T // tm, D_MODEL, tm), BF16),
            jax.ShapeDtypeStruct((PK_HEADS, T, PK_QDIM), BF16),
        ],
        compiler_params=pltpu.CompilerParams(
            dimension_semantics=("parallel",), vmem_limit_bytes=VMEM_LIMIT),
        name="merge",
    )(ya2, yb2, zg, x2, proj_a, proj_b, w_out, norm_w, wq)


_STAIR = tuple(PK_TOPK // (i + 1) for i in range(PK_TOPK))


def _route_kernel(q_ref, keys_ref, r2_ref, e2_ref, n1_ref, c1_ref, v1_s, v2_s):
    half = PK_QDIM // 2
    q = q_ref[0]
    s1 = _dot_nt(keys_ref[0, 0], q[:, 0:half])
    s2 = _dot_nt(keys_ref[0, 1], q[:, half:2 * half])

    rank2 = jnp.full(s2.shape, float(PK_TOPK), F32)
    w1, w2 = s1, s2
    for i in range(PK_TOPK):
        m1 = jnp.max(w1, axis=0, keepdims=True)
        m2 = jnp.max(w2, axis=0, keepdims=True)
        v1_s[i:i + 1, :] = m1
        v2_s[i:i + 1, :] = m2
        w1 = jnp.where(w1 == m1, -jnp.inf, w1)
        hit2 = w2 == m2
        rank2 = jnp.where(hit2, float(i), rank2)
        w2 = jnp.where(hit2, -jnp.inf, w2)
    v1 = v1_s[...]
    v2 = v2_s[...]

    def stair(i, rows):
        jrow = lax.broadcasted_iota(jnp.int32, (rows, 1), 0)
        return jnp.where(jrow < _STAIR[i], v1[i:i + 1] + v2[0:rows], -jnp.inf)

    cand = jnp.concatenate([stair(i, PK_TOPK) for i in range(4)]
                           + [stair(i, 8) for i in range(4, 8)]
                           + [v1[8:PK_TOPK] + v2[0:1]], axis=0)
    work = cand
    tau = None
    for i in range(PK_TOPK):
        tau = jnp.max(work, axis=0, keepdims=True)
        work = jnp.where(work == tau, -jnp.inf, work)
    cmax = v1[0:1] + v2[0:1]
    z = jnp.sum(jnp.where(cand >= tau, jnp.exp(cand - cmax), 0.0), axis=0, keepdims=True)
    n1 = jnp.zeros_like(s1)
    for jj in range(PK_TOPK):
        n1 = jnp.where(s1 + v2[jj:jj + 1] >= tau, float(jj + 1), n1)
    r2_ref[0, 0] = rank2.astype(r2_ref.dtype)
    e2_ref[0, 0] = jnp.exp(s2 - v2[0:1]).astype(e2_ref.dtype)
    n1_ref[0, 0] = n1
    c1_ref[0, 0] = jnp.exp(s1 - v1[0:1]) * (1.0 / z)


def _route(q3, keys, tm=PEER_TM):
    T = q3.shape[1]
    blk = lambda: pl.BlockSpec((1, 1, PK_NKEYS, tm), lambda i, h: (i, h, 0, 0))
    shp = lambda dt: jax.ShapeDtypeStruct((T // tm, PK_HEADS, PK_NKEYS, tm), dt)
    return pl.pallas_call(
        _route_kernel,
        grid=(T // tm, PK_HEADS),
        in_specs=[
            pl.BlockSpec((1, tm, PK_QDIM), lambda i, h: (h, i, 0)),
            pl.BlockSpec((1, 2, PK_NKEYS, PK_QDIM // 2), lambda i, h: (h, 0, 0, 0)),
        ],
        out_specs=[blk(), blk(), blk(), blk()],
        out_shape=[shp(BF16), shp(BF16), shp(F32), shp(F32)],
        scratch_shapes=[pltpu.VMEM((PK_TOPK, tm), F32), pltpu.VMEM((PK_TOPK, tm), F32)],
        compiler_params=pltpu.CompilerParams(
            dimension_semantics=("parallel", "arbitrary"), vmem_limit_bytes=VMEM_LIMIT),
        name="route",
    )(q3, keys)


def _peer_kernel(ht_ref, u_ref, vt_ref, r2_ref, e2_ref, n1_ref, c1_ref, x1_ref, nw_ref, o_ref,
                 acc_s, aw0_s, aw1_s, act0_s, act1_s, *, te, nst):
    s = pl.program_id(0)
    jp = lax.rem(s + (nst - 1), jnp.int32(nst))

    @pl.when(s == 0)
    def _():
        acc_s[...] = jnp.zeros_like(acc_s)
        aw1_s[...] = jnp.zeros_like(aw1_s)

    group = act0_s.shape[0]
    npiece = te // group
    drows = D_MODEL // npiece
    restart = jp == 0
    act_bufs = (act0_s, act1_s)

    def u_proj(i):
        act_bufs[i % 2][...] = _dot(u_ref[i * group:(i + 1) * group, :], ht_ref[0])

    def step(aw_build, aw_drain):
        u_proj(0)
        for i in range(npiece):
            if i + 1 < npiece:
                u_proj(i + 1)
            for half in range(group // PK_NKEYS):
                e = i * (group // PK_NKEYS) + half
                a = act_bufs[i % 2][half * PK_NKEYS:(half + 1) * PK_NKEYS, :].astype(BF16)
                gelu = a * (0.5 + 0.5 * lax.erf(a * (1.0 / math.sqrt(2.0))))
                w = None
                for hd in range(PK_HEADS):
                    n = n1_ref[0, hd, e:e + 1, :].astype(BF16)
                    c = c1_ref[0, hd, e:e + 1, :].astype(BF16)
                    wh = jnp.where(r2_ref[0, hd] < n, e2_ref[0, hd] * c, jnp.zeros(a.shape, BF16))
                    w = wh if w is None else w + wh
                aw_build[e * PK_NKEYS:(e + 1) * PK_NKEYS, :] = gelu * w
            rs = slice(i * drows, (i + 1) * drows)
            part = _dot(vt_ref[rs, :], aw_drain[...])
            acc_s[rs, :] = jnp.where(restart, 0.0, acc_s[rs, :]) + part

    parity = lax.rem(s, jnp.int32(2))

    @pl.when(parity == 0)
    def _():
        step(aw0_s, aw1_s)

    @pl.when(parity == 1)
    def _():
        step(aw1_s, aw0_s)

    @pl.when(jnp.logical_and(jp == nst - 1, s > 0))
    def _():
        x2 = x1_ref[...] + acc_s[...].T
        ms = jnp.mean(x2 * x2, axis=-1, keepdims=True)
        o_ref[...] = x2 * lax.rsqrt(ms + NORM_EPS) * nw_ref[...]


def _peer(h2t, u_bf16, vt_bf16, r2, e2, n1, c1, x1, final_w, te=2048):
    ntile, _, tm = h2t.shape
    T = ntile * tm
    group = max(PK_NKEYS, PEER_PIECE_ELEMS // tm)
    nst = PK_EXPERTS // te
    tile_ab = lambda s: jnp.minimum(s // nst, ntile - 1)
    tile_c = lambda s: jnp.maximum(s - 1, 0) // nst
    rt = lambda: pl.BlockSpec((1, PK_HEADS, PK_NKEYS, tm), lambda s: (tile_ab(s), 0, 0, 0))
    rows = lambda: pl.BlockSpec((1, PK_HEADS, te // PK_NKEYS, tm), lambda s: (tile_ab(s), 0, s % nst, 0))
    return pl.pallas_call(
        functools.partial(_peer_kernel, te=te, nst=nst),
        grid=(ntile * nst + 1,),
        in_specs=[
            pl.BlockSpec((1, D_MODEL, tm), lambda s: (tile_ab(s), 0, 0)),
            pl.BlockSpec((te, D_MODEL), lambda s: (s % nst, 0)),
            pl.BlockSpec((D_MODEL, te), lambda s: (0, jnp.maximum(s - 1, 0) % nst)),
            rt(), rt(), rows(), rows(),
            pl.BlockSpec((tm, D_MODEL), lambda s: (tile_c(s), 0)),
            pl.BlockSpec((1, D_MODEL), lambda s: (0, 0)),
        ],
        out_specs=pl.BlockSpec((tm, D_MODEL), lambda s: (tile_c(s), 0)),
        out_shape=jax.ShapeDtypeStruct((T, D_MODEL), F32),
        scratch_shapes=[pltpu.VMEM((D_MODEL, tm), F32),
                        pltpu.VMEM((te, tm), BF16), pltpu.VMEM((te, tm), BF16),
                        pltpu.VMEM((group, tm), F32), pltpu.VMEM((group, tm), F32)],
        compiler_params=pltpu.CompilerParams(
            dimension_semantics=("arbitrary",), vmem_limit_bytes=VMEM_LIMIT),
        name="peer",
    )(h2t, u_bf16, vt_bf16, r2, e2, n1, c1, x1, final_w)


def _lora_blockdiag(w2, a2):
    z = jnp.zeros_like(w2)
    return jnp.concatenate([jnp.concatenate([w2, z], axis=1), jnp.concatenate([z, a2], axis=1)], axis=0)


def kernel(x, norm_mix_w, w_in, shift_mu, w0, w2, a0, a2, g2, k_k, k_a, r_k, lnx_w, lnx_b, lam_q1, lam_k1, lam_q2, lam_k2, subln_w, proj_a, proj_b, w_out, norm_ffn_w, peer_wq, peer_keys, peer_u, peer_v, final_norm_w):
    B, S, D = x.shape
    T = B * S
    depth = w_in.shape[0]
    assert depth == 1 and D == D_MODEL
    l = 0
    x2 = x.reshape(T, D)
    zs, zqkv, zg = _inproj(x2, norm_mix_w[l][None], w_in[l].astype(BF16))
    ya = _rwkv(zs.reshape(B, S, SHIFT_COLS), shift_mu[l][None], w0[l][None], a0[l][None], k_k[l][None],
               k_a[l][None], r_k[l].reshape(1, RW_WIDTH), lnx_w[l][None], lnx_b[l][None],
               _lora_blockdiag(w2[l], a2[l]).astype(BF16), g2[l].astype(BF16))
    yb = _attn(zqkv.reshape(B, S, QKV_COLS), lam_q1[l][None], lam_k1[l][None], lam_q2[l][None],
               lam_k2[l][None], subln_w[l][:, None])
    x1, h2t, q = _merge(ya.reshape(T, RW_WIDTH), yb.reshape(T, DA_WIDTH), zg, x2,
                       proj_a[l].astype(BF16), proj_b[l].astype(BF16), w_out[l].astype(BF16),
                       norm_ffn_w[l][None], peer_wq[l].astype(BF16))
    r2, e2, n1, c1 = _route(q, peer_keys[l].astype(BF16))
    out = _peer(h2t, peer_u[l].astype(BF16), peer_v[l].T.astype(BF16), r2, e2, n1, c1, x1,
                final_norm_w[None])
    return out.reshape(B, S, D)
```

```python
import functools
import math

import jax
import jax.numpy as jnp
from jax import lax
from jax.experimental import pallas as pl
from jax.experimental.pallas import tpu as pltpu

F32 = jnp.float32
BF16 = jnp.bfloat16

D_MODEL = 1024
RW_HEADS = 8
RW_HEAD = 64
RW_WIDTH = RW_HEADS * RW_HEAD
DECAY_LORA = 64
ICLR_LORA = 64
GATE_LORA = 128
DA_HEADS = 4
DA_HEAD = 64
DA_VDIM = 2 * DA_HEAD
DA_QK_WIDTH = DA_HEADS * 2 * DA_HEAD
DA_WIDTH = DA_HEADS * DA_VDIM
PK_HEADS = 8
PK_NKEYS = 128
PK_QDIM = 256
PK_TOPK = 16
PK_EXPERTS = PK_NKEYS * PK_NKEYS
NORM_EPS = 1e-6
GN_EPS = 64e-5
SUBLN_EPS = 1e-5
SHIFT_COLS = 3 * RW_WIDTH + DECAY_LORA + ICLR_LORA + GATE_LORA
QKV_COLS = 2 * DA_QK_WIDTH + DA_WIDTH
GATE_COLS = 2 * D_MODEL
LAM_INIT = 0.8 - 0.6 * math.exp(0.0)

LANES = 128
BF16_SUBLANES = 16
CHUNK = 64
PAIR = 2 * RW_HEAD
PEER_PIECE_ELEMS = 128 * 1024
PEER_TM = 512
NEG_BIG = -1e30
VMEM_LIMIT = 56 * 1024 * 1024

NT_DIMS = (((1,), (1,)), ((), ()))
TN_DIMS = (((0,), (0,)), ((), ()))


def _dot(a, b):
    return jnp.dot(a, b, preferred_element_type=F32)


def _dot_nt(a, b):
    return lax.dot_general(a, b, NT_DIMS, preferred_element_type=F32)


def _dot_tn(a, b):
    return lax.dot_general(a, b, TN_DIMS, preferred_element_type=F32)


def _sigmoid(x):
    return 1.0 / (1.0 + jnp.exp(-x))


def _split_terms(x, terms):
    parts = []
    rest = x
    for i in range(terms):
        p = rest.astype(BF16)
        parts.append(p)
        if i + 1 < terms:
            rest = rest - p.astype(F32)
    return parts


def _dot_exact_rhs(a_bf16, x, terms=3):
    n = x.shape[1]
    y = _dot(a_bf16, jnp.concatenate(_split_terms(x, terms), axis=1))
    return sum(y[:, i * n:(i + 1) * n] for i in range(terms))


def _dot_exact_lhs(x, b_bf16, terms=2):
    m = x.shape[0]
    y = _dot(jnp.concatenate(_split_terms(x, terms), axis=0), b_bf16)
    return sum(y[i * m:(i + 1) * m] for i in range(terms))


def _inproj_kernel(x_ref, nw_ref, w_ref, zs_ref, zqkv_ref, zg_ref, *, col_chunk):
    x = x_ref[...]
    ms = jnp.mean(x * x, axis=-1, keepdims=True)
    h = (x * lax.rsqrt(ms + NORM_EPS) * nw_ref[...]).astype(BF16)
    c0 = 0
    for out_ref in (zs_ref, zqkv_ref, zg_ref):
        width = out_ref.shape[-1]
        for j in range(0, width, col_chunk):
            z = _dot(h, w_ref[:, c0 + j:c0 + j + col_chunk])
            out_ref[:, j:j + col_chunk] = z.astype(out_ref.dtype)
        c0 += width


def _inproj(x2, norm_w, w_in_bf16, tm=256, col_chunk=256):
    T = x2.shape[0]
    in_cols = w_in_bf16.shape[1]
    return pl.pallas_call(
        functools.partial(_inproj_kernel, col_chunk=col_chunk),
        grid=(T // tm,),
        in_specs=[
            pl.BlockSpec((tm, D_MODEL), lambda i: (i, 0)),
            pl.BlockSpec((1, D_MODEL), lambda i: (0, 0)),
            pl.BlockSpec((D_MODEL, in_cols), lambda i: (0, 0)),
        ],
        out_specs=[
            pl.BlockSpec((tm, SHIFT_COLS), lambda i: (i, 0)),
            pl.BlockSpec((tm, QKV_COLS), lambda i: (i, 0)),
            pl.BlockSpec((tm, GATE_COLS), lambda i: (i, 0)),
        ],
        out_shape=[
            jax.ShapeDtypeStruct((T, SHIFT_COLS), F32),
            jax.ShapeDtypeStruct((T, QKV_COLS), BF16),
            jax.ShapeDtypeStruct((T, GATE_COLS), F32),
        ],
        compiler_params=pltpu.CompilerParams(
            dimension_semantics=("parallel",), vmem_limit_bytes=VMEM_LIMIT),
        name="inproj",
    )(x2, norm_w, w_in_bf16)


def _rwkv_kernel(zs_ref, mu_ref, w0_ref, a0_ref, kk_ref, ka_ref, rk_ref, lnw_ref, lnb_ref,
                 wlora_ref, g2_ref, ones_ref, tri_ref, ya_ref,
                 state_ref, prev_ref, r_s, k_s, v_s, a_s, b_s, ld_s, l_s, y_s, *, ts, group):
    t = pl.program_id(1)

    @pl.when(t == 0)
    def _():
        state_ref[...] = jnp.zeros_like(state_ref)
        prev_ref[...] = jnp.zeros_like(prev_ref)

    z = zs_ref[0]
    row = lax.broadcasted_iota(jnp.int32, (ts, 1), 0)
    zprev = jnp.where(row == 0, prev_ref[...], pltpu.roll(z, 1, axis=0))
    prev_ref[...] = z[ts - 1:ts, :]
    zz = z + (zprev - z) * mu_ref[...]

    W = RW_WIDTH
    r = zz[:, 0:W]
    k = zz[:, W:2 * W]
    v = zz[:, 2 * W:3 * W]
    wa = zz[:, 3 * W:3 * W + LANES]
    gl = zz[:, 3 * W + LANES:3 * W + 2 * LANES]
    lane = lax.broadcasted_iota(jnp.int32, (1, LANES), 1)
    wa_act = jnp.where(lane < DECAY_LORA, jnp.tanh(wa), wa).astype(BF16)
    lora = _dot(wa_act, wlora_ref[...])
    u = -(w0_ref[...] + lora[:, 0:W])
    softplus = jnp.maximum(u, 0.0) + jnp.log1p(jnp.exp(-jnp.abs(u)))
    wlog = -softplus - 0.5
    ld = -jnp.exp(wlog)
    a = _sigmoid(a0_ref[...] + lora[:, W:2 * W])
    g = _dot(_sigmoid(gl).astype(BF16), g2_ref[...])

    ones_blk = ones_ref[...]
    kk = k * kk_ref[...]
    ss = _dot_exact_lhs(kk * kk, ones_blk)
    kk = kk / jnp.maximum(jnp.sqrt(ss), 1e-12)
    k2 = k * (1.0 + (a - 1.0) * ka_ref[...])
    bonus = _dot_exact_lhs(r * k2 * rk_ref[...], ones_blk) * v

    r_s[...] = r
    k_s[...] = k2
    v_s[...] = v
    a_s[...] = -kk
    b_s[...] = kk * a
    ld_s[...] = ld
    l_s[...] = _dot_exact_rhs(tri_ref[...], ld)

    rowi = lax.broadcasted_iota(jnp.int32, (PAIR, PAIR), 0)
    coli = lax.broadcasted_iota(jnp.int32, (PAIR, PAIR), 1)
    strict_lower = rowi > coli
    lower = rowi >= coli
    eye = rowi == coli
    head0 = lane < RW_HEAD

    def stack(xp):
        return jnp.concatenate([jnp.where(head0, xp, 0.0), jnp.where(head0, 0.0, xp)], axis=0)

    def unstack(xs):
        return xs[0:CHUNK] + xs[CHUNK:2 * CHUNK]

    mid = CHUNK // 2 - 1

    def chunk_terms(r0):
        rows = pl.ds(r0, CHUNK)
        lc = l_s[rows, :]
        cm = l_s[pl.ds(r0 + mid, 1), :]
        lend = l_s[pl.ds(r0 + CHUNK - 1, 1), :]
        e_pos = jnp.exp(lc - cm)
        e_neg = jnp.exp(cm - lc)
        e_cm = jnp.exp(cm)
        e_end = e_neg * jnp.exp(lend - cm)
        r_cen = r_s[rows, :] * e_pos
        a_cen = a_s[rows, :] * jnp.exp(lc - ld_s[rows, :] - cm)
        bc = b_s[rows, :]
        kc = k_s[rows, :]
        return dict(rows=rows, p_end=jnp.exp(lend), v=v_s[rows, :],
                    r_cen=r_cen, r_tru=r_cen * e_cm, a_cen=a_cen, a_tru=a_cen * e_cm,
                    b_cen=bc * e_neg, k_cen=kc * e_neg, b_end=bc * e_end, k_end=kc * e_end)

    def chunk_body(c, carry):
        terms = [chunk_terms(pl.multiple_of((c * group + ci) * CHUNK, CHUNK)) for ci in range(group)]
        items = [(t, slice(p * PAIR, (p + 1) * PAIR)) for t in terms for p in range(RW_HEADS // 2)]
        idx = range(len(items))
        sc = [_dot_nt(jnp.concatenate([stack(t["a_cen"][:, ls]), stack(t["r_cen"][:, ls])], axis=0).astype(BF16),
                      jnp.concatenate([stack(t["b_cen"][:, ls]), stack(t["k_cen"][:, ls])], axis=0).astype(BF16))
              for t, ls in items]
        a_ab = [jnp.where(strict_lower, s_[0:PAIR, 0:PAIR], 0.0) for s_ in sc]
        a_ak = [jnp.where(strict_lower, s_[0:PAIR, PAIR:2 * PAIR], 0.0).astype(BF16) for s_ in sc]
        m_rb = [jnp.where(lower, s_[PAIR:2 * PAIR, 0:PAIR], 0.0).astype(BF16) for s_ in sc]
        m_rk = [jnp.where(lower, s_[PAIR:2 * PAIR, PAIR:2 * PAIR], 0.0).astype(BF16) for s_ in sc]
        v_st = [stack(t["v"][:, ls]).astype(BF16) for t, ls in items]
        x = [jnp.concatenate([stack(t["a_tru"][:, ls]), _dot(a_ak[i], v_st[i])], axis=1)
             for i, (t, ls) in enumerate(items)]
        n = a_ab
        steps = int(math.log2(CHUNK))
        for k in range(steps):
            nb = [n_.astype(BF16) for n_ in n]
            x = [x[i] + _dot(nb[i], x[i].astype(BF16)) for i in idx]
            if k + 1 < steps:
                n = [_dot(nb_, nb_) for nb_ in nb]
        xb = [x_.astype(BF16) for x_ in x]
        ry = [_dot(m_rb[i], xb[i]) for i in idx]
        r_new = [unstack(stack(t["r_tru"][:, ls]) + ry[i][:, 0:PAIR]).astype(BF16)
                 for i, (t, ls) in enumerate(items)]
        y0 = [unstack(ry[i][:, PAIR:2 * PAIR] + _dot(m_rk[i], v_st[i])) for i in idx]
        b_st = [stack(t["b_end"][:, ls]).astype(BF16) for t, ls in items]
        k_st = [stack(t["k_end"][:, ls]).astype(BF16) for t, ls in items]
        gh = [_dot_tn(xb[i], b_st[i]) for i in idx]
        h_t = [gh[i][PAIR:2 * PAIR] + _dot_tn(v_st[i], k_st[i]) for i in idx]
        g_t = [gh[i][0:PAIR].astype(BF16) for i in idx]
        for i, (t, ls) in enumerate(items):
            p = i % (RW_HEADS // 2)
            s_old = state_ref[p]
            sb = s_old.astype(BF16)
            y_s[t["rows"], ls] = _dot_nt(r_new[i], sb) + y0[i]
            state_ref[p] = s_old * t["p_end"][:, ls] + _dot(sb, g_t[i]) + h_t[i]
        return carry

    lax.fori_loop(0, ts // (CHUNK * group), chunk_body, 0)

    y = y_s[...]
    inv_n = 1.0 / RW_HEAD
    mean = _dot_exact_lhs(y, ones_blk) * inv_n
    yc = y - mean
    var = _dot_exact_lhs(yc * yc, ones_blk) * inv_n
    yn = yc * lax.rsqrt(var + GN_EPS) * lnw_ref[...] + lnb_ref[...]
    ya_ref[0] = ((yn + bonus) * g).astype(ya_ref.dtype)


def _rwkv(zs3, mu, w0, a0, k_k, k_a, r_k, lnx_w, lnx_b, wlora, g2, ts=256):
    B, S, _ = zs3.shape
    W = RW_WIDTH
    ones_blk = (jnp.arange(W)[:, None] // RW_HEAD == jnp.arange(W)[None, :] // RW_HEAD).astype(BF16)
    ti = jnp.arange(ts)
    tri = ((ti[:, None] // CHUNK == ti[None, :] // CHUNK) & (ti[:, None] >= ti[None, :])).astype(BF16)
    vec = lambda n: pl.BlockSpec((1, n), lambda b, t: (0, 0))
    full = lambda a: pl.BlockSpec(a.shape, lambda b, t: (0,) * a.ndim)
    return pl.pallas_call(
        functools.partial(_rwkv_kernel, ts=ts, group=2),
        grid=(B, S // ts),
        in_specs=[
            pl.BlockSpec((1, ts, SHIFT_COLS), lambda b, t: (b, t, 0)),
            vec(SHIFT_COLS), vec(W), vec(W), vec(W), vec(W), vec(W), vec(W), vec(W),
            full(wlora), full(g2), full(ones_blk), full(tri),
        ],
        out_specs=pl.BlockSpec((1, ts, W), lambda b, t: (b, t, 0)),
        out_shape=jax.ShapeDtypeStruct((B, S, W), BF16),
        scratch_shapes=[
            pltpu.VMEM((RW_HEADS // 2, PAIR, PAIR), F32),
            pltpu.VMEM((1, SHIFT_COLS), F32),
        ] + [pltpu.VMEM((ts, W), F32) for _ in range(8)],
        compiler_params=pltpu.CompilerParams(
            dimension_semantics=("parallel", "arbitrary"), vmem_limit_bytes=VMEM_LIMIT),
        name="rwkv",
    )(zs3, mu, w0, a0, k_k, k_a, r_k, lnx_w, lnx_b, wlora, g2, ones_blk, tri)


def _attn_kernel(slope_ref, lq1_ref, lk1_ref, lq2_ref, lk2_ref, sw_ref, q_ref, k_ref, v_ref, o_ref,
                 m_s, l_s, acc_s, bias_s, *, tq, tk):
    h = pl.program_id(1)
    qi = pl.program_id(2)
    slope = slope_ref[h]
    lam = (jnp.exp(jnp.sum(lq1_ref[...] * lk1_ref[...], axis=-1, keepdims=True))
           - jnp.exp(jnp.sum(lq2_ref[...] * lk2_ref[...], axis=-1, keepdims=True)) + LAM_INIT)

    lane = lax.broadcasted_iota(jnp.int32, (1, LANES), 1)
    map0 = lane < DA_HEAD
    q = q_ref[0] * (1.0 / math.sqrt(DA_HEAD))
    zero = jnp.zeros_like(q)
    qst = jnp.concatenate([jnp.where(map0, q, zero), jnp.where(map0, zero, q)], axis=0)

    m_s[...] = jnp.full_like(m_s, NEG_BIG)
    l_s[...] = jnp.zeros_like(l_s)
    acc_s[...] = jnp.zeros_like(acc_s)

    @pl.when(qi == 0)
    def _():
        krow = lax.broadcasted_iota(jnp.int32, (tk, 2 * tq), 0)
        qcol = lax.broadcasted_iota(jnp.int32, (tk, 2 * tq), 1)
        qcol = jnp.where(qcol >= tq, qcol - tq, qcol)
        bias_s[...] = -slope * (qcol - krow).astype(F32)

    def block(j, diagonal):
        k0 = pl.multiple_of(j * tk, tk)
        kb = k_ref[0, pl.ds(k0, tk), :]
        vb = v_ref[0, pl.ds(k0, tk), :]
        shift = slope * ((qi - j) * tq).astype(F32)
        b = bias_s[...]
        x = _dot_nt(kb, qst) + b
        if diagonal:
            x = jnp.where(b <= 0.0, x, NEG_BIG)
        m_old = m_s[...]
        m_new = jnp.maximum(m_old, jnp.max(x, axis=0, keepdims=True) - shift)
        alpha = jnp.exp(m_old - m_new)
        p = jnp.exp(x - (m_new + shift))
        l_s[...] = alpha * l_s[...] + jnp.sum(p, axis=0, keepdims=True)
        acc_s[...] = alpha * acc_s[...] + _dot_tn(vb, p.astype(BF16))
        m_s[...] = m_new

    def body(j, carry):
        block(j, False)
        return carry

    lax.fori_loop(0, qi, body, 0)
    block(qi, True)

    o = acc_s[...] / l_s[...]
    o = o[:, 0:tq] - lam * o[:, tq:2 * tq]
    o = o * lax.rsqrt(jnp.mean(o * o, axis=0, keepdims=True) + SUBLN_EPS) * sw_ref[...]
    o_ref[0] = (o * (1.0 - LAM_INIT)).T.astype(o_ref.dtype)


def _attn(zqkv3, lam_q1, lam_k1, lam_q2, lam_k2, subln_w, tq=512, tk=512):
    B, S, _ = zqkv3.shape
    assert tq == tk, "the kernel masks only the diagonal block of aligned square tiles"
    slopes = jnp.asarray([2.0 ** (-8.0 * (i + 1) / DA_HEADS) for i in range(DA_HEADS)], F32)
    nqk = DA_QK_WIDTH // LANES
    vec = lambda n: pl.BlockSpec((1, n), lambda b, h, i: (0, 0))
    return pl.pallas_call(
        functools.partial(_attn_kernel, tq=tq, tk=tk),
        grid=(B, DA_HEADS, S // tq),
        in_specs=[
            pl.BlockSpec(memory_space=pltpu.SMEM),
            vec(DA_HEAD), vec(DA_HEAD), vec(DA_HEAD), vec(DA_HEAD),
            pl.BlockSpec((DA_VDIM, 1), lambda b, h, i: (0, 0)),
            pl.BlockSpec((1, tq, LANES), lambda b, h, i: (b, i, h)),
            pl.BlockSpec((1, S, LANES), lambda b, h, i: (b, 0, nqk + h)),
            pl.BlockSpec((1, S, LANES), lambda b, h, i: (b, 0, 2 * nqk + h)),
        ],
        out_specs=pl.BlockSpec((1, tq, LANES), lambda b, h, i: (b, i, h)),
        out_shape=jax.ShapeDtypeStruct((B, S, DA_WIDTH), BF16),
        scratch_shapes=[
            pltpu.VMEM((1, 2 * tq), F32),
            pltpu.VMEM((1, 2 * tq), F32),
            pltpu.VMEM((DA_VDIM, 2 * tq), F32),
            pltpu.VMEM((tk, 2 * tq), F32),
        ],
        compiler_params=pltpu.CompilerParams(
            dimension_semantics=("parallel", "parallel", "arbitrary"), vmem_limit_bytes=VMEM_LIMIT),
        name="attn",
    )(slopes, lam_q1, lam_k1, lam_q2, lam_k2, subln_w, zqkv3, zqkv3, zqkv3)


def _merge_kernel(ya_ref, yb_ref, zg_ref, x_ref, pa_ref, pb_ref, wo_ref, nw_ref, wq_ref,
                  x1_ref, h2t_ref, q_ref):
    pa = _dot(ya_ref[...], pa_ref[...])
    pb = _dot(yb_ref[...], pb_ref[...])
    ga = zg_ref[:, 0:D_MODEL]
    gb = zg_ref[:, D_MODEL:2 * D_MODEL]
    merged = _sigmoid(ga) * pa + _sigmoid(gb) * pb
    x1 = x_ref[...] + _dot(merged.astype(BF16), wo_ref[...])
    x1_ref[...] = x1
    ms = jnp.mean(x1 * x1, axis=-1, keepdims=True)
    h2 = x1 * lax.rsqrt(ms + NORM_EPS) * nw_ref[...]
    h2t_ref[0] = h2.T.astype(BF16)
    q = _dot(h2.astype(BF16), wq_ref[...]).astype(q_ref.dtype)
    for hd in range(PK_HEADS):
        q_ref[hd] = q[:, hd * PK_QDIM:(hd + 1) * PK_QDIM]


def _merge(ya2, yb2, zg, x2, proj_a, proj_b, w_out, norm_w, wq, tm=PEER_TM):
    T = x2.shape[0]
    row = lambda n: pl.BlockSpec((tm, n), lambda i: (i, 0))
    full = lambda a: pl.BlockSpec(a.shape, lambda i: (0,) * a.ndim)
    return pl.pallas_call(
        _merge_kernel,
        grid=(T // tm,),
        in_specs=[row(RW_WIDTH), row(DA_WIDTH), row(GATE_COLS), row(D_MODEL),
                  full(proj_a), full(proj_b), full(w_out), full(norm_w), full(wq)],
        out_specs=[row(D_MODEL),
                   pl.BlockSpec((1, D_MODEL, tm), lambda i: (i, 0, 0)),
                   pl.BlockSpec((PK_HEADS, tm, PK_QDIM), lambda i: (0, i, 0))],
        out_shape=[
            jax.ShapeDtypeStruct((T, D_MODEL), F32),
            jax.ShapeDtypeStruct((T // tm, D_MODEL, tm), BF16),
            jax.ShapeDtypeStruct((PK_HEADS, T, PK_QDIM), BF16),
        ],
        compiler_params=pltpu.CompilerParams(
            dimension_semantics=("parallel",), vmem_limit_bytes=VMEM_LIMIT),
        name="merge",
    )(ya2, yb2, zg, x2, proj_a, proj_b, w_out, norm_w, wq)


_STAIR = tuple(PK_TOPK // (i + 1) for i in range(PK_TOPK))


def _route_kernel(q_ref, keys_ref, r2_ref, e2_ref, n1_ref, c1_ref, v1_s, v2_s):
    half = PK_QDIM // 2
    q = q_ref[0]
    s1 = _dot_nt(keys_ref[0, 0], q[:, 0:half])
    s2 = _dot_nt(keys_ref[0, 1], q[:, half:2 * half])

    rank2 = jnp.full(s2.shape, float(PK_TOPK), F32)
    w1, w2 = s1, s2
    for i in range(PK_TOPK):
        m1 = jnp.max(w1, axis=0, keepdims=True)
        m2 = jnp.max(w2, axis=0, keepdims=True)
        v1_s[i:i + 1, :] = m1
        v2_s[i:i + 1, :] = m2
        w1 = jnp.where(w1 == m1, -jnp.inf, w1)
        hit2 = w2 == m2
        rank2 = jnp.where(hit2, float(i), rank2)
        w2 = jnp.where(hit2, -jnp.inf, w2)
    v1 = v1_s[...]
    v2 = v2_s[...]

    def stair(i, rows):
        jrow = lax.broadcasted_iota(jnp.int32, (rows, 1), 0)
        return jnp.where(jrow < _STAIR[i], v1[i:i + 1] + v2[0:rows], -jnp.inf)

    cand = jnp.concatenate([stair(i, PK_TOPK) for i in range(4)]
                           + [stair(i, 8) for i in range(4, 8)]
                           + [v1[8:PK_TOPK] + v2[0:1]], axis=0)
    work = cand
    tau = None
    for i in range(PK_TOPK):
        tau = jnp.max(work, axis=0, keepdims=True)
        work = jnp.where(work == tau, -jnp.inf, work)
    cmax = v1[0:1] + v2[0:1]
    z = jnp.sum(jnp.where(cand >= tau, jnp.exp(cand - cmax), 0.0), axis=0, keepdims=True)
    n1 = jnp.zeros_like(s1)
    for jj in range(PK_TOPK):
        n1 = jnp.where(s1 + v2[jj:jj + 1] >= tau, float(jj + 1), n1)
    r2_ref[0, 0] = rank2.astype(r2_ref.dtype)
    e2_ref[0, 0] = jnp.exp(s2 - v2[0:1]).astype(e2_ref.dtype)
    n1_ref[0, 0] = n1
    c1_ref[0, 0] = jnp.exp(s1 - v1[0:1]) * (0.5 / z)


def _route(q3, keys, tm=PEER_TM):
    T = q3.shape[1]
    blk = lambda: pl.BlockSpec((1, 1, PK_NKEYS, tm), lambda i, h: (i, h, 0, 0))
    shp = lambda dt: jax.ShapeDtypeStruct((T // tm, PK_HEADS, PK_NKEYS, tm), dt)
    return pl.pallas_call(
        _route_kernel,
        grid=(T // tm, PK_HEADS),
        in_specs=[
            pl.BlockSpec((1, tm, PK_QDIM), lambda i, h: (h, i, 0)),
            pl.BlockSpec((1, 2, PK_NKEYS, PK_QDIM // 2), lambda i, h: (h, 0, 0, 0)),
        ],
        out_specs=[blk(), blk(), blk(), blk()],
        out_shape=[shp(BF16), shp(BF16), shp(F32), shp(F32)],
        scratch_shapes=[pltpu.VMEM((PK_TOPK, tm), F32), pltpu.VMEM((PK_TOPK, tm), F32)],
        compiler_params=pltpu.CompilerParams(
            dimension_semantics=("parallel", "arbitrary"), vmem_limit_bytes=VMEM_LIMIT),
        name="route",
    )(q3, keys)


def _peer_kernel(ht_ref, u_ref, vt_ref, r2_ref, e2_ref, n1_ref, c1_ref, x1_ref, nw_ref, o_ref,
                 acc_s, aw0_s, aw1_s, act0_s, act1_s, *, te, nst):
    s = pl.program_id(0)
    jp = lax.rem(s + (nst - 1), jnp.int32(nst))

    @pl.when(s == 0)
    def _():
        acc_s[...] = jnp.zeros_like(acc_s)
        aw1_s[...] = jnp.zeros_like(aw1_s)

    group = act0_s.shape[0]
    npiece = te // group
    drows = D_MODEL // npiece
    restart = jp == 0
    act_bufs = (act0_s, act1_s)
    tile = (BF16_SUBLANES, ht_ref.shape[2])
    ntile = PK_NKEYS // BF16_SUBLANES

    def u_proj(i):
        act = _dot(u_ref[i * group:(i + 1) * group, :], ht_ref[0])
        act_bufs[i % 2][...] = act.astype(BF16)

    def step(aw_build, aw_drain):
        u_proj(0)
        for i in range(npiece):
            if i + 1 < npiece:
                u_proj(i + 1)
            for half in range(group // PK_NKEYS):
                e = i * (group // PK_NKEYS) + half
                a = act_bufs[i % 2][half * PK_NKEYS:(half + 1) * PK_NKEYS, :]
                gelu2 = a * (1.0 + lax.erf(a * (1.0 / math.sqrt(2.0))))
                w = None
                for hd in range(PK_HEADS):
                    n = jnp.broadcast_to(n1_ref[0, hd, e:e + 1, :], tile).astype(BF16)
                    c = jnp.broadcast_to(c1_ref[0, hd, e:e + 1, :], tile).astype(BF16)
                    r2 = r2_ref[0, hd].reshape(ntile, *tile)
                    e2 = e2_ref[0, hd].reshape(ntile, *tile)
                    wh = jnp.where(r2 < n[None], e2 * c[None], jnp.zeros(r2.shape, BF16))
                    w = wh if w is None else w + wh
                aw_build[e * PK_NKEYS:(e + 1) * PK_NKEYS, :] = gelu2 * w.reshape(a.shape)
            rs = slice(i * drows, (i + 1) * drows)
            part = _dot(vt_ref[rs, :], aw_drain[...])
            acc_s[rs, :] = jnp.where(restart, 0.0, acc_s[rs, :]) + part

    parity = lax.rem(s, jnp.int32(2))

    @pl.when(parity == 0)
    def _():
        step(aw0_s, aw1_s)

    @pl.when(parity == 1)
    def _():
        step(aw1_s, aw0_s)

    @pl.when(jnp.logical_and(jp == nst - 1, s > 0))
    def _():
        x2 = x1_ref[...] + acc_s[...].T
        ms = jnp.mean(x2 * x2, axis=-1, keepdims=True)
        o_ref[...] = x2 * lax.rsqrt(ms + NORM_EPS) * nw_ref[...]


def _peer(h2t, u_bf16, vt_bf16, r2, e2, n1, c1, x1, final_w, te=2048):
    ntile, _, tm = h2t.shape
    T = ntile * tm
    group = max(PK_NKEYS, PEER_PIECE_ELEMS // tm)
    nst = PK_EXPERTS // te
    tile_ab = lambda s: jnp.minimum(s // nst, ntile - 1)
    tile_c = lambda s: jnp.maximum(s - 1, 0) // nst
    rt = lambda: pl.BlockSpec((1, PK_HEADS, PK_NKEYS, tm), lambda s: (tile_ab(s), 0, 0, 0))
    rows = lambda: pl.BlockSpec((1, PK_HEADS, te // PK_NKEYS, tm), lambda s: (tile_ab(s), 0, s % nst, 0))
    return pl.pallas_call(
        functools.partial(_peer_kernel, te=te, nst=nst),
        grid=(ntile * nst + 1,),
        in_specs=[
            pl.BlockSpec((1, D_MODEL, tm), lambda s: (tile_ab(s), 0, 0)),
            pl.BlockSpec((te, D_MODEL), lambda s: (s % nst, 0)),
            pl.BlockSpec((D_MODEL, te), lambda s: (0, jnp.maximum(s - 1, 0) % nst)),
            rt(), rt(), rows(), rows(),
            pl.BlockSpec((tm, D_MODEL), lambda s: (tile_c(s), 0)),
            pl.BlockSpec((1, D_MODEL), lambda s: (0, 0)),
        ],
        out_specs=pl.BlockSpec((tm, D_MODEL), lambda s: (tile_c(s), 0)),
        out_shape=jax.ShapeDtypeStruct((T, D_MODEL), F32),
        scratch_shapes=[pltpu.VMEM((D_MODEL, tm), F32),
                        pltpu.VMEM((te, tm), BF16), pltpu.VMEM((te, tm), BF16),
                        pltpu.VMEM((group, tm), BF16), pltpu.VMEM((group, tm), BF16)],
        compiler_params=pltpu.CompilerParams(
            dimension_semantics=("arbitrary",), vmem_limit_bytes=VMEM_LIMIT),
        name="peer",
    )(h2t, u_bf16, vt_bf16, r2, e2, n1, c1, x1, final_w)


def _lora_blockdiag(w2, a2):
    z = jnp.zeros_like(w2)
    return jnp.concatenate([jnp.concatenate([w2, z], axis=1), jnp.concatenate([z, a2], axis=1)], axis=0)


def kernel(x, norm_mix_w, w_in, shift_mu, w0, w2, a0, a2, g2, k_k, k_a, r_k, lnx_w, lnx_b, lam_q1, lam_k1, lam_q2, lam_k2, subln_w, proj_a, proj_b, w_out, norm_ffn_w, peer_wq, peer_keys, peer_u, peer_v, final_norm_w):
    B, S, D = x.shape
    T = B * S
    depth = w_in.shape[0]
    assert depth == 1 and D == D_MODEL
    l = 0
    x2 = x.reshape(T, D)
    zs, zqkv, zg = _inproj(x2, norm_mix_w[l][None], w_in[l].astype(BF16))
    ya = _rwkv(zs.reshape(B, S, SHIFT_COLS), shift_mu[l][None], w0[l][None], a0[l][None], k_k[l][None],
               k_a[l][None], r_k[l].reshape(1, RW_WIDTH), lnx_w[l][None], lnx_b[l][None],
               _lora_blockdiag(w2[l], a2[l]).astype(BF16), g2[l].astype(BF16))
    yb = _attn(zqkv.reshape(B, S, QKV_COLS), lam_q1[l][None], lam_k1[l][None], lam_q2[l][None],
               lam_k2[l][None], subln_w[l][:, None])
    x1, h2t, q = _merge(ya.reshape(T, RW_WIDTH), yb.reshape(T, DA_WIDTH), zg, x2,
                       proj_a[l].astype(BF16), proj_b[l].astype(BF16), w_out[l].astype(BF16),
                       norm_ffn_w[l][None], peer_wq[l].astype(BF16))
    r2, e2, n1, c1 = _route(q, peer_keys[l].astype(BF16))
    out = _peer(h2t, peer_u[l].astype(BF16), peer_v[l].T.astype(BF16), r2, e2, n1, c1, x1,
                final_norm_w[None])
    return out.reshape(B, S, D)
```

```python
import functools
import math

import jax
import jax.numpy as jnp
from jax import lax
from jax.experimental import pallas as pl
from jax.experimental.pallas import tpu as pltpu

F32 = jnp.float32
BF16 = jnp.bfloat16

D_MODEL = 1024
RW_HEADS = 8
RW_HEAD = 64
RW_WIDTH = RW_HEADS * RW_HEAD
DECAY_LORA = 64
ICLR_LORA = 64
GATE_LORA = 128
DA_HEADS = 4
DA_HEAD = 64
DA_VDIM = 2 * DA_HEAD
DA_QK_WIDTH = DA_HEADS * 2 * DA_HEAD
DA_WIDTH = DA_HEADS * DA_VDIM
PK_HEADS = 8
PK_NKEYS = 128
PK_QDIM = 256
PK_TOPK = 16
PK_EXPERTS = PK_NKEYS * PK_NKEYS
NORM_EPS = 1e-6
GN_EPS = 64e-5
SUBLN_EPS = 1e-5
SHIFT_COLS = 3 * RW_WIDTH + DECAY_LORA + ICLR_LORA + GATE_LORA
QKV_COLS = 2 * DA_QK_WIDTH + DA_WIDTH
GATE_COLS = 2 * D_MODEL
LAM_INIT = 0.8 - 0.6 * math.exp(0.0)

LANES = 128
BF16_SUBLANES = 16
CHUNK = 64
PAIR = 2 * RW_HEAD
PEER_PIECE_ELEMS = 128 * 1024
PEER_TM = 512
NEG_BIG = -1e30
VMEM_LIMIT = 56 * 1024 * 1024

NT_DIMS = (((1,), (1,)), ((), ()))
TN_DIMS = (((0,), (0,)), ((), ()))


def _dot(a, b):
    return jnp.dot(a, b, preferred_element_type=F32)


def _dot_nt(a, b):
    return lax.dot_general(a, b, NT_DIMS, preferred_element_type=F32)


def _dot_tn(a, b):
    return lax.dot_general(a, b, TN_DIMS, preferred_element_type=F32)


def _sigmoid(x):
    return 1.0 / (1.0 + jnp.exp(-x))


def _split_terms(x, terms):
    parts = []
    rest = x
    for i in range(terms):
        p = rest.astype(BF16)
        parts.append(p)
        if i + 1 < terms:
            rest = rest - p.astype(F32)
    return parts


def _dot_exact_rhs(a_bf16, x, terms=3):
    n = x.shape[1]
    y = _dot(a_bf16, jnp.concatenate(_split_terms(x, terms), axis=1))
    return sum(y[:, i * n:(i + 1) * n] for i in range(terms))


def _dot_exact_lhs(x, b_bf16, terms=2):
    m = x.shape[0]
    y = _dot(jnp.concatenate(_split_terms(x, terms), axis=0), b_bf16)
    return sum(y[i * m:(i + 1) * m] for i in range(terms))


def _inproj_kernel(x_ref, nw_ref, w_ref, zs_ref, zqkv_ref, zg_ref, *, col_chunk):
    x = x_ref[...]
    ms = jnp.mean(x * x, axis=-1, keepdims=True)
    h = (x * lax.rsqrt(ms + NORM_EPS) * nw_ref[...]).astype(BF16)
    c0 = 0
    for out_ref in (zs_ref, zqkv_ref, zg_ref):
        width = out_ref.shape[-1]
        for j in range(0, width, col_chunk):
            z = _dot(h, w_ref[:, c0 + j:c0 + j + col_chunk])
            out_ref[:, j:j + col_chunk] = z.astype(out_ref.dtype)
        c0 += width


def _inproj(x2, norm_w, w_in_bf16, tm=256, col_chunk=256):
    T = x2.shape[0]
    in_cols = w_in_bf16.shape[1]
    return pl.pallas_call(
        functools.partial(_inproj_kernel, col_chunk=col_chunk),
        grid=(T // tm,),
        in_specs=[
            pl.BlockSpec((tm, D_MODEL), lambda i: (i, 0)),
            pl.BlockSpec((1, D_MODEL), lambda i: (0, 0)),
            pl.BlockSpec((D_MODEL, in_cols), lambda i: (0, 0)),
        ],
        out_specs=[
            pl.BlockSpec((tm, SHIFT_COLS), lambda i: (i, 0)),
            pl.BlockSpec((tm, QKV_COLS), lambda i: (i, 0)),
            pl.BlockSpec((tm, GATE_COLS), lambda i: (i, 0)),
        ],
        out_shape=[
            jax.ShapeDtypeStruct((T, SHIFT_COLS), F32),
            jax.ShapeDtypeStruct((T, QKV_COLS), BF16),
            jax.ShapeDtypeStruct((T, GATE_COLS), F32),
        ],
        compiler_params=pltpu.CompilerParams(
            dimension_semantics=("parallel",), vmem_limit_bytes=VMEM_LIMIT),
        name="inproj",
    )(x2, norm_w, w_in_bf16)


def _rwkv_kernel(zs_ref, mu_ref, w0_ref, a0_ref, kk_ref, ka_ref, rk_ref, lnw_ref, lnb_ref,
                 wlora_ref, g2_ref, ones_ref, tri_ref, ya_ref,
                 state_ref, prev_ref, r_s, k_s, v_s, a_s, b_s, ld_s, l_s, y_s, *, ts, group):
    t = pl.program_id(1)

    @pl.when(t == 0)
    def _():
        state_ref[...] = jnp.zeros_like(state_ref)
        prev_ref[...] = jnp.zeros_like(prev_ref)

    z = zs_ref[0]
    row = lax.broadcasted_iota(jnp.int32, (ts, 1), 0)
    zprev = jnp.where(row == 0, prev_ref[...], pltpu.roll(z, 1, axis=0))
    prev_ref[...] = z[ts - 1:ts, :]
    zz = z + (zprev - z) * mu_ref[...]

    W = RW_WIDTH
    r = zz[:, 0:W]
    k = zz[:, W:2 * W]
    v = zz[:, 2 * W:3 * W]
    wa = zz[:, 3 * W:3 * W + LANES]
    gl = zz[:, 3 * W + LANES:3 * W + 2 * LANES]
    lane = lax.broadcasted_iota(jnp.int32, (1, LANES), 1)
    wa_act = jnp.where(lane < DECAY_LORA, jnp.tanh(wa), wa).astype(BF16)
    lora = _dot(wa_act, wlora_ref[...])
    u = -(w0_ref[...] + lora[:, 0:W])
    softplus = jnp.maximum(u, 0.0) + jnp.log1p(jnp.exp(-jnp.abs(u)))
    wlog = -softplus - 0.5
    ld = -jnp.exp(wlog)
    a = _sigmoid(a0_ref[...] + lora[:, W:2 * W])
    g = _dot(_sigmoid(gl).astype(BF16), g2_ref[...])

    ones_blk = ones_ref[...]
    kk = k * kk_ref[...]
    ss = _dot_exact_lhs(kk * kk, ones_blk)
    kk = kk / jnp.maximum(jnp.sqrt(ss), 1e-12)
    k2 = k * (1.0 + (a - 1.0) * ka_ref[...])
    bonus = _dot_exact_lhs(r * k2 * rk_ref[...], ones_blk) * v

    r_s[...] = r
    k_s[...] = k2
    v_s[...] = v
    a_s[...] = -kk
    b_s[...] = kk * a
    ld_s[...] = ld
    l_s[...] = _dot_exact_rhs(tri_ref[...], ld)

    rowi = lax.broadcasted_iota(jnp.int32, (PAIR, PAIR), 0)
    coli = lax.broadcasted_iota(jnp.int32, (PAIR, PAIR), 1)
    strict_lower = rowi > coli
    lower = rowi >= coli
    eye = rowi == coli
    head0 = lane < RW_HEAD

    def stack(xp):
        return jnp.concatenate([jnp.where(head0, xp, 0.0), jnp.where(head0, 0.0, xp)], axis=0)

    def unstack(xs):
        return xs[0:CHUNK] + xs[CHUNK:2 * CHUNK]

    mid = CHUNK // 2 - 1

    def chunk_terms(r0):
        rows = pl.ds(r0, CHUNK)
        lc = l_s[rows, :]
        cm = l_s[pl.ds(r0 + mid, 1), :]
        lend = l_s[pl.ds(r0 + CHUNK - 1, 1), :]
        e_pos = jnp.exp(lc - cm)
        e_neg = jnp.exp(cm - lc)
        e_cm = jnp.exp(cm)
        e_end = e_neg * jnp.exp(lend - cm)
        r_cen = r_s[rows, :] * e_pos
        a_cen = a_s[rows, :] * jnp.exp(lc - ld_s[rows, :] - cm)
        bc = b_s[rows, :]
        kc = k_s[rows, :]
        return dict(rows=rows, p_end=jnp.exp(lend), v=v_s[rows, :],
                    r_cen=r_cen, r_tru=r_cen * e_cm, a_cen=a_cen, a_tru=a_cen * e_cm,
                    b_cen=bc * e_neg, k_cen=kc * e_neg, b_end=bc * e_end, k_end=kc * e_end)

    def chunk_body(c, carry):
        terms = [chunk_terms(pl.multiple_of((c * group + ci) * CHUNK, CHUNK)) for ci in range(group)]
        items = [(t, slice(p * PAIR, (p + 1) * PAIR)) for t in terms for p in range(RW_HEADS // 2)]
        idx = range(len(items))
        sc = [_dot_nt(jnp.concatenate([stack(t["a_cen"][:, ls]), stack(t["r_cen"][:, ls])], axis=0).astype(BF16),
                      jnp.concatenate([stack(t["b_cen"][:, ls]), stack(t["k_cen"][:, ls])], axis=0).astype(BF16))
              for t, ls in items]
        a_ab = [jnp.where(strict_lower, s_[0:PAIR, 0:PAIR], 0.0) for s_ in sc]
        a_ak = [jnp.where(strict_lower, s_[0:PAIR, PAIR:2 * PAIR], 0.0).astype(BF16) for s_ in sc]
        m_rb = [jnp.where(lower, s_[PAIR:2 * PAIR, 0:PAIR], 0.0).astype(BF16) for s_ in sc]
        m_rk = [jnp.where(lower, s_[PAIR:2 * PAIR, PAIR:2 * PAIR], 0.0).astype(BF16) for s_ in sc]
        v_st = [stack(t["v"][:, ls]).astype(BF16) for t, ls in items]
        x = [jnp.concatenate([stack(t["a_tru"][:, ls]), _dot(a_ak[i], v_st[i])], axis=1)
             for i, (t, ls) in enumerate(items)]
        n = a_ab
        steps = int(math.log2(CHUNK))
        for k in range(steps):
            nb = [n_.astype(BF16) for n_ in n]
            x = [x[i] + _dot(nb[i], x[i].astype(BF16)) for i in idx]
            if k + 1 < steps:
                n = [_dot(nb_, nb_) for nb_ in nb]
        xb = [x_.astype(BF16) for x_ in x]
        ry = [_dot(m_rb[i], xb[i]) for i in idx]
        r_new = [unstack(stack(t["r_tru"][:, ls]) + ry[i][:, 0:PAIR]).astype(BF16)
                 for i, (t, ls) in enumerate(items)]
        y0 = [unstack(ry[i][:, PAIR:2 * PAIR] + _dot(m_rk[i], v_st[i])) for i in idx]
        b_st = [stack(t["b_end"][:, ls]).astype(BF16) for t, ls in items]
        k_st = [stack(t["k_end"][:, ls]).astype(BF16) for t, ls in items]
        gh = [_dot_tn(xb[i], b_st[i]) for i in idx]
        h_t = [gh[i][PAIR:2 * PAIR] + _dot_tn(v_st[i], k_st[i]) for i in idx]
        g_t = [gh[i][0:PAIR].astype(BF16) for i in idx]
        for i, (t, ls) in enumerate(items):
            p = i % (RW_HEADS // 2)
            s_old = state_ref[p]
            sb = s_old.astype(BF16)
            y_s[t["rows"], ls] = _dot_nt(r_new[i], sb) + y0[i]
            state_ref[p] = s_old * t["p_end"][:, ls] + _dot(sb, g_t[i]) + h_t[i]
        return carry

    lax.fori_loop(0, ts // (CHUNK * group), chunk_body, 0)

    y = y_s[...]
    inv_n = 1.0 / RW_HEAD
    mean = _dot_exact_lhs(y, ones_blk) * inv_n
    yc = y - mean
    var = _dot_exact_lhs(yc * yc, ones_blk) * inv_n
    yn = yc * lax.rsqrt(var + GN_EPS) * lnw_ref[...] + lnb_ref[...]
    ya_ref[0] = ((yn + bonus) * g).astype(ya_ref.dtype)


def _rwkv(zs3, mu, w0, a0, k_k, k_a, r_k, lnx_w, lnx_b, wlora, g2, ts=256):
    B, S, _ = zs3.shape
    W = RW_WIDTH
    ones_blk = (jnp.arange(W)[:, None] // RW_HEAD == jnp.arange(W)[None, :] // RW_HEAD).astype(BF16)
    ti = jnp.arange(ts)
    tri = ((ti[:, None] // CHUNK == ti[None, :] // CHUNK) & (ti[:, None] >= ti[None, :])).astype(BF16)
    vec = lambda n: pl.BlockSpec((1, n), lambda b, t: (0, 0))
    full = lambda a: pl.BlockSpec(a.shape, lambda b, t: (0,) * a.ndim)
    return pl.pallas_call(
        functools.partial(_rwkv_kernel, ts=ts, group=2),
        grid=(B, S // ts),
        in_specs=[
            pl.BlockSpec((1, ts, SHIFT_COLS), lambda b, t: (b, t, 0)),
            vec(SHIFT_COLS), vec(W), vec(W), vec(W), vec(W), vec(W), vec(W), vec(W),
            full(wlora), full(g2), full(ones_blk), full(tri),
        ],
        out_specs=pl.BlockSpec((1, ts, W), lambda b, t: (b, t, 0)),
        out_shape=jax.ShapeDtypeStruct((B, S, W), BF16),
        scratch_shapes=[
            pltpu.VMEM((RW_HEADS // 2, PAIR, PAIR), F32),
            pltpu.VMEM((1, SHIFT_COLS), F32),
        ] + [pltpu.VMEM((ts, W), F32) for _ in range(8)],
        compiler_params=pltpu.CompilerParams(
            dimension_semantics=("parallel", "arbitrary"), vmem_limit_bytes=VMEM_LIMIT),
        name="rwkv",
    )(zs3, mu, w0, a0, k_k, k_a, r_k, lnx_w, lnx_b, wlora, g2, ones_blk, tri)


def _attn_kernel(slope_ref, lq1_ref, lk1_ref, lq2_ref, lk2_ref, sw_ref, q_ref, k_ref, v_ref, o_ref,
                 m_s, l_s, acc_s, bias_s, *, tq, tk):
    h = pl.program_id(1)
    qi = pl.program_id(2)
    slope = slope_ref[h]
    lam = (jnp.exp(jnp.sum(lq1_ref[...] * lk1_ref[...], axis=-1, keepdims=True))
           - jnp.exp(jnp.sum(lq2_ref[...] * lk2_ref[...], axis=-1, keepdims=True)) + LAM_INIT)

    lane = lax.broadcasted_iota(jnp.int32, (1, LANES), 1)
    map0 = lane < DA_HEAD
    q = q_ref[0] * (1.0 / math.sqrt(DA_HEAD))
    zero = jnp.zeros_like(q)
    qst = jnp.concatenate([jnp.where(map0, q, zero), jnp.where(map0, zero, q)], axis=0)

    m_s[...] = jnp.full_like(m_s, NEG_BIG)
    l_s[...] = jnp.zeros_like(l_s)
    acc_s[...] = jnp.zeros_like(acc_s)

    @pl.when(qi == 0)
    def _():
        krow = lax.broadcasted_iota(jnp.int32, (tk, 2 * tq), 0)
        qcol = lax.broadcasted_iota(jnp.int32, (tk, 2 * tq), 1)
        qcol = jnp.where(qcol >= tq, qcol - tq, qcol)
        bias_s[...] = -slope * (qcol - krow).astype(F32)

    def block(j, diagonal):
        k0 = pl.multiple_of(j * tk, tk)
        kb = k_ref[0, pl.ds(k0, tk), :]
        vb = v_ref[0, pl.ds(k0, tk), :]
        shift = slope * ((qi - j) * tq).astype(F32)
        b = bias_s[...]
        x = _dot_nt(kb, qst) + b
        if diagonal:
            x = jnp.where(b <= 0.0, x, NEG_BIG)
        m_old = m_s[...]
        m_new = jnp.maximum(m_old, jnp.max(x, axis=0, keepdims=True) - shift)
        alpha = jnp.exp(m_old - m_new)
        p = jnp.exp(x - (m_new + shift))
        l_s[...] = alpha * l_s[...] + jnp.sum(p, axis=0, keepdims=True)
        acc_s[...] = alpha * acc_s[...] + _dot_tn(vb, p.astype(BF16))
        m_s[...] = m_new

    def body(j, carry):
        block(j, False)
        return carry

    lax.fori_loop(0, qi, body, 0)
    block(qi, True)

    o = acc_s[...] / l_s[...]
    o = o[:, 0:tq] - lam * o[:, tq:2 * tq]
    o = o * lax.rsqrt(jnp.mean(o * o, axis=0, keepdims=True) + SUBLN_EPS) * sw_ref[...]
    o_ref[0] = (o * (1.0 - LAM_INIT)).T.astype(o_ref.dtype)


def _attn(zqkv3, lam_q1, lam_k1, lam_q2, lam_k2, subln_w, tq=512, tk=512):
    B, S, _ = zqkv3.shape
    assert tq == tk, "the kernel masks only the diagonal block of aligned square tiles"
    slopes = jnp.asarray([2.0 ** (-8.0 * (i + 1) / DA_HEADS) for i in range(DA_HEADS)], F32)
    nqk = DA_QK_WIDTH // LANES
    vec = lambda n: pl.BlockSpec((1, n), lambda b, h, i: (0, 0))
    return pl.pallas_call(
        functools.partial(_attn_kernel, tq=tq, tk=tk),
        grid=(B, DA_HEADS, S // tq),
        in_specs=[
            pl.BlockSpec(memory_space=pltpu.SMEM),
            vec(DA_HEAD), vec(DA_HEAD), vec(DA_HEAD), vec(DA_HEAD),
            pl.BlockSpec((DA_VDIM, 1), lambda b, h, i: (0, 0)),
            pl.BlockSpec((1, tq, LANES), lambda b, h, i: (b, i, h)),
            pl.BlockSpec((1, S, LANES), lambda b, h, i: (b, 0, nqk + h)),
            pl.BlockSpec((1, S, LANES), lambda b, h, i: (b, 0, 2 * nqk + h)),
        ],
        out_specs=pl.BlockSpec((1, tq, LANES), lambda b, h, i: (b, i, h)),
        out_shape=jax.ShapeDtypeStruct((B, S, DA_WIDTH), BF16),
        scratch_shapes=[
            pltpu.VMEM((1, 2 * tq), F32),
            pltpu.VMEM((1, 2 * tq), F32),
            pltpu.VMEM((DA_VDIM, 2 * tq), F32),
            pltpu.VMEM((tk, 2 * tq), F32),
        ],
        compiler_params=pltpu.CompilerParams(
            dimension_semantics=("parallel", "parallel", "arbitrary"), vmem_limit_bytes=VMEM_LIMIT),
        name="attn",
    )(slopes, lam_q1, lam_k1, lam_q2, lam_k2, subln_w, zqkv3, zqkv3, zqkv3)


def _merge_kernel(ya_ref, yb_ref, zg_ref, x_ref, pa_ref, pb_ref, wo_ref, nw_ref, wq_ref,
                  x1_ref, h2t_ref, q_ref):
    pa = _dot(ya_ref[...], pa_ref[...])
    pb = _dot(yb_ref[...], pb_ref[...])
    ga = zg_ref[:, 0:D_MODEL]
    gb = zg_ref[:, D_MODEL:2 * D_MODEL]
    merged = _sigmoid(ga) * pa + _sigmoid(gb) * pb
    x1 = x_ref[...] + _dot(merged.astype(BF16), wo_ref[...])
    x1_ref[...] = x1
    ms = jnp.mean(x1 * x1, axis=-1, keepdims=True)
    h2 = x1 * lax.rsqrt(ms + NORM_EPS) * nw_ref[...]
    h2t_ref[0] = h2.T.astype(BF16)
    q = _dot(h2.astype(BF16), wq_ref[...]).astype(q_ref.dtype)
    for hd in range(PK_HEADS):
        q_ref[hd] = q[:, hd * PK_QDIM:(hd + 1) * PK_QDIM]


def _merge(ya2, yb2, zg, x2, proj_a, proj_b, w_out, norm_w, wq, tm=PEER_TM):
    T = x2.shape[0]
    row = lambda n: pl.BlockSpec((tm, n), lambda i: (i, 0))
    full = lambda a: pl.BlockSpec(a.shape, lambda i: (0,) * a.ndim)
    return pl.pallas_call(
        _merge_kernel,
        grid=(T // tm,),
        in_specs=[row(RW_WIDTH), row(DA_WIDTH), row(GATE_COLS), row(D_MODEL),
                  full(proj_a), full(proj_b), full(w_out), full(norm_w), full(wq)],
        out_specs=[row(D_MODEL),
                   pl.BlockSpec((1, D_MODEL, tm), lambda i: (i, 0, 0)),
                   pl.BlockSpec((PK_HEADS, tm, PK_QDIM), lambda i: (0, i, 0))],
        out_shape=[
            jax.ShapeDtypeStruct((T, D_MODEL), F32),
            jax.ShapeDtypeStruct((T // tm, D_MODEL, tm), BF16),
            jax.ShapeDtypeStruct((PK_HEADS, T, PK_QDIM), BF16),
        ],
        compiler_params=pltpu.CompilerParams(
            dimension_semantics=("parallel",), vmem_limit_bytes=VMEM_LIMIT),
        name="merge",
    )(ya2, yb2, zg, x2, proj_a, proj_b, w_out, norm_w, wq)


_STAIR = tuple(PK_TOPK // (i + 1) for i in range(PK_TOPK))


def _route_kernel(q_ref, keys_ref, r2_ref, e2_ref, n1_ref, c1_ref, v1_s, v2_s):
    half = PK_QDIM // 2
    q = q_ref[0]
    s1 = _dot_nt(keys_ref[0, 0], q[:, 0:half])
    s2 = _dot_nt(keys_ref[0, 1], q[:, half:2 * half])

    rank2 = jnp.full(s2.shape, float(PK_TOPK), F32)
    w1, w2 = s1, s2
    for i in range(PK_TOPK):
        m1 = jnp.max(w1, axis=0, keepdims=True)
        m2 = jnp.max(w2, axis=0, keepdims=True)
        v1_s[i:i + 1, :] = m1
        v2_s[i:i + 1, :] = m2
        w1 = jnp.where(w1 == m1, -jnp.inf, w1)
        hit2 = w2 == m2
        rank2 = jnp.where(hit2, float(i), rank2)
        w2 = jnp.where(hit2, -jnp.inf, w2)
    v1 = v1_s[...]
    v2 = v2_s[...]

    def stair(i, rows):
        jrow = lax.broadcasted_iota(jnp.int32, (rows, 1), 0)
        return jnp.where(jrow < _STAIR[i], v1[i:i + 1] + v2[0:rows], -jnp.inf)

    cand = jnp.concatenate([stair(i, PK_TOPK) for i in range(4)]
                           + [stair(i, 8) for i in range(4, 8)]
                           + [v1[8:PK_TOPK] + v2[0:1]], axis=0)
    work = cand
    tau = None
    for i in range(PK_TOPK):
        tau = jnp.max(work, axis=0, keepdims=True)
        work = jnp.where(work == tau, -jnp.inf, work)
    cmax = v1[0:1] + v2[0:1]
    z = jnp.sum(jnp.where(cand >= tau, jnp.exp(cand - cmax), 0.0), axis=0, keepdims=True)
    n1 = jnp.zeros_like(s1)
    for jj in range(PK_TOPK):
        n1 = jnp.where(s1 + v2[jj:jj + 1] >= tau, float(jj + 1), n1)
    r2_ref[0, 0] = rank2.astype(r2_ref.dtype)
    e2_ref[0, 0] = jnp.exp(s2 - v2[0:1]).astype(e2_ref.dtype)
    n1_ref[0, 0] = n1
    c1_ref[0, 0] = jnp.exp(s1 - v1[0:1]) * (0.5 / z)


def _route(q3, keys, tm=PEER_TM):
    T = q3.shape[1]
    blk = lambda: pl.BlockSpec((1, 1, PK_NKEYS, tm), lambda i, h: (i, h, 0, 0))
    shp = lambda dt: jax.ShapeDtypeStruct((T // tm, PK_HEADS, PK_NKEYS, tm), dt)
    return pl.pallas_call(
        _route_kernel,
        grid=(T // tm, PK_HEADS),
        in_specs=[
            pl.BlockSpec((1, tm, PK_QDIM), lambda i, h: (h, i, 0)),
            pl.BlockSpec((1, 2, PK_NKEYS, PK_QDIM // 2), lambda i, h: (h, 0, 0, 0)),
        ],
        out_specs=[blk(), blk(), blk(), blk()],
        out_shape=[shp(BF16), shp(BF16), shp(F32), shp(F32)],
        scratch_shapes=[pltpu.VMEM((PK_TOPK, tm), F32), pltpu.VMEM((PK_TOPK, tm), F32)],
        compiler_params=pltpu.CompilerParams(
            dimension_semantics=("parallel", "arbitrary"), vmem_limit_bytes=VMEM_LIMIT),
        name="route",
    )(q3, keys)


def _peer_kernel(ht_ref, u_ref, vt_ref, r2_ref, e2_ref, n1_ref, c1_ref, x1_ref, nw_ref, o_ref,
                 acc_s, aw0_s, aw1_s, act0_s, act1_s, *, te, nst):
    s = pl.program_id(0)
    jp = lax.rem(s + (nst - 1), jnp.int32(nst))

    @pl.when(s == 0)
    def _():
        acc_s[...] = jnp.zeros_like(acc_s)
        aw1_s[...] = jnp.zeros_like(aw1_s)

    group = act0_s.shape[0]
    npiece = te // group
    e_per_piece = group // PK_NKEYS
    tm = ht_ref.shape[2]
    drain_every = 2
    drows = D_MODEL * drain_every // npiece
    restart = jp == 0
    act_bufs = (act0_s, act1_s)
    tile = (BF16_SUBLANES, tm)
    ntile = PK_NKEYS // BF16_SUBLANES

    def u_proj(i):
        act = _dot(u_ref[i * group:(i + 1) * group, :], ht_ref[0])
        act_bufs[i % 2][...] = act.astype(BF16)

    def step(aw_build, aw_drain):
        u_proj(0)
        for i in range(npiece):
            if i + 1 < npiece:
                u_proj(i + 1)
            for half in range(e_per_piece):
                e = i * e_per_piece + half
                a = act_bufs[i % 2][half * PK_NKEYS:(half + 1) * PK_NKEYS, :]
                gelu2 = a * (1.0 + lax.erf(a * (1.0 / math.sqrt(2.0))))
                w = None
                for hd in range(PK_HEADS):
                    n = jnp.broadcast_to(n1_ref[0, hd, e:e + 1, :], tile).astype(BF16)
                    c = jnp.broadcast_to(c1_ref[0, hd, e:e + 1, :], tile).astype(BF16)
                    r2 = r2_ref[0, hd].reshape(ntile, *tile)
                    e2 = e2_ref[0, hd].reshape(ntile, *tile)
                    wh = jnp.where(r2 < n[None], e2 * c[None], jnp.zeros(r2.shape, BF16))
                    w = wh if w is None else w + wh
                aw_build[e * PK_NKEYS:(e + 1) * PK_NKEYS, :] = gelu2 * w.reshape(a.shape)
            if (i + 1) % drain_every == 0:
                d = (i + 1) // drain_every - 1
                rs = slice(d * drows, (d + 1) * drows)
                part = _dot(vt_ref[rs, :], aw_drain[...])
                acc_s[rs, :] = jnp.where(restart, 0.0, acc_s[rs, :]) + part

    parity = lax.rem(s, jnp.int32(2))

    @pl.when(parity == 0)
    def _():
        step(aw0_s, aw1_s)

    @pl.when(parity == 1)
    def _():
        step(aw1_s, aw0_s)

    @pl.when(jnp.logical_and(jp == nst - 1, s > 0))
    def _():
        x2 = x1_ref[...] + acc_s[...].T
        ms = jnp.mean(x2 * x2, axis=-1, keepdims=True)
        o_ref[...] = x2 * lax.rsqrt(ms + NORM_EPS) * nw_ref[...]


def _peer(h2t, u_bf16, vt_bf16, r2, e2, n1, c1, x1, final_w, te=2048):
    ntile, _, tm = h2t.shape
    T = ntile * tm
    group = max(PK_NKEYS, PEER_PIECE_ELEMS // tm)
    nst = PK_EXPERTS // te
    tile_ab = lambda s: jnp.minimum(s // nst, ntile - 1)
    tile_c = lambda s: jnp.maximum(s - 1, 0) // nst
    rt = lambda: pl.BlockSpec((1, PK_HEADS, PK_NKEYS, tm), lambda s: (tile_ab(s), 0, 0, 0))
    rows = lambda: pl.BlockSpec((1, PK_HEADS, te // PK_NKEYS, tm), lambda s: (tile_ab(s), 0, s % nst, 0))
    return pl.pallas_call(
        functools.partial(_peer_kernel, te=te, nst=nst),
        grid=(ntile * nst + 1,),
        in_specs=[
            pl.BlockSpec((1, D_MODEL, tm), lambda s: (tile_ab(s), 0, 0)),
            pl.BlockSpec((te, D_MODEL), lambda s: (s % nst, 0)),
            pl.BlockSpec((D_MODEL, te), lambda s: (0, jnp.maximum(s - 1, 0) % nst)),
            rt(), rt(), rows(), rows(),
            pl.BlockSpec((tm, D_MODEL), lambda s: (tile_c(s), 0)),
            pl.BlockSpec((1, D_MODEL), lambda s: (0, 0)),
        ],
        out_specs=pl.BlockSpec((tm, D_MODEL), lambda s: (tile_c(s), 0)),
        out_shape=jax.ShapeDtypeStruct((T, D_MODEL), F32),
        scratch_shapes=[pltpu.VMEM((D_MODEL, tm), F32),
                        pltpu.VMEM((te, tm), BF16), pltpu.VMEM((te, tm), BF16),
                        pltpu.VMEM((group, tm), BF16), pltpu.VMEM((group, tm), BF16)],
        compiler_params=pltpu.CompilerParams(
            dimension_semantics=("arbitrary",), vmem_limit_bytes=VMEM_LIMIT),
        name="peer",
    )(h2t, u_bf16, vt_bf16, r2, e2, n1, c1, x1, final_w)


def _lora_blockdiag(w2, a2):
    z = jnp.zeros_like(w2)
    return jnp.concatenate([jnp.concatenate([w2, z], axis=1), jnp.concatenate([z, a2], axis=1)], axis=0)


def kernel(x, norm_mix_w, w_in, shift_mu, w0, w2, a0, a2, g2, k_k, k_a, r_k, lnx_w, lnx_b, lam_q1, lam_k1, lam_q2, lam_k2, subln_w, proj_a, proj_b, w_out, norm_ffn_w, peer_wq, peer_keys, peer_u, peer_v, final_norm_w):
    B, S, D = x.shape
    T = B * S
    depth = w_in.shape[0]
    assert depth == 1 and D == D_MODEL
    l = 0
    x2 = x.reshape(T, D)
    zs, zqkv, zg = _inproj(x2, norm_mix_w[l][None], w_in[l].astype(BF16))
    ya = _rwkv(zs.reshape(B, S, SHIFT_COLS), shift_mu[l][None], w0[l][None], a0[l][None], k_k[l][None],
               k_a[l][None], r_k[l].reshape(1, RW_WIDTH), lnx_w[l][None], lnx_b[l][None],
               _lora_blockdiag(w2[l], a2[l]).astype(BF16), g2[l].astype(BF16))
    yb = _attn(zqkv.reshape(B, S, QKV_COLS), lam_q1[l][None], lam_k1[l][None], lam_q2[l][None],
               lam_k2[l][None], subln_w[l][:, None])
    x1, h2t, q = _merge(ya.reshape(T, RW_WIDTH), yb.reshape(T, DA_WIDTH), zg, x2,
                       proj_a[l].astype(BF16), proj_b[l].astype(BF16), w_out[l].astype(BF16),
                       norm_ffn_w[l][None], peer_wq[l].astype(BF16))
    r2, e2, n1, c1 = _route(q, peer_keys[l].astype(BF16))
    out = _peer(h2t, peer_u[l].astype(BF16), peer_v[l].T.astype(BF16), r2, e2, n1, c1, x1,
                final_norm_w[None])
    return out.reshape(B, S, D)
```

```python
import functools
import math

import jax
import jax.numpy as jnp
from jax import lax
from jax.experimental import pallas as pl
from jax.experimental.pallas import tpu as pltpu

F32 = jnp.float32
BF16 = jnp.bfloat16

D_MODEL = 1024
RW_HEADS = 8
RW_HEAD = 64
RW_WIDTH = RW_HEADS * RW_HEAD
DECAY_LORA = 64
ICLR_LORA = 64
GATE_LORA = 128
DA_HEADS = 4
DA_HEAD = 64
DA_VDIM = 2 * DA_HEAD
DA_QK_WIDTH = DA_HEADS * 2 * DA_HEAD
DA_WIDTH = DA_HEADS * DA_VDIM
PK_HEADS = 8
PK_NKEYS = 128
PK_QDIM = 256
PK_TOPK = 16
PK_EXPERTS = PK_NKEYS * PK_NKEYS
NORM_EPS = 1e-6
GN_EPS = 64e-5
SUBLN_EPS = 1e-5
SHIFT_COLS = 3 * RW_WIDTH + DECAY_LORA + ICLR_LORA + GATE_LORA
QKV_COLS = 2 * DA_QK_WIDTH + DA_WIDTH
GATE_COLS = 2 * D_MODEL
LAM_INIT = 0.8 - 0.6 * math.exp(0.0)

LANES = 128
BF16_SUBLANES = 16
CHUNK = 64
PAIR = 2 * RW_HEAD
PEER_PIECE_ELEMS = 256 * 1024
PEER_TM = 512
NEG_BIG = -1e30
VMEM_LIMIT = 56 * 1024 * 1024

NT_DIMS = (((1,), (1,)), ((), ()))
TN_DIMS = (((0,), (0,)), ((), ()))


def _dot(a, b):
    return jnp.dot(a, b, preferred_element_type=F32)


def _dot_nt(a, b):
    return lax.dot_general(a, b, NT_DIMS, preferred_element_type=F32)


def _dot_tn(a, b):
    return lax.dot_general(a, b, TN_DIMS, preferred_element_type=F32)


def _sigmoid(x):
    return 1.0 / (1.0 + jnp.exp(-x))


def _split_terms(x, terms):
    parts = []
    rest = x
    for i in range(terms):
        p = rest.astype(BF16)
        parts.append(p)
        if i + 1 < terms:
            rest = rest - p.astype(F32)
    return parts


def _dot_exact_rhs(a_bf16, x, terms=3):
    n = x.shape[1]
    y = _dot(a_bf16, jnp.concatenate(_split_terms(x, terms), axis=1))
    return sum(y[:, i * n:(i + 1) * n] for i in range(terms))


def _dot_exact_lhs(x, b_bf16, terms=2):
    m = x.shape[0]
    y = _dot(jnp.concatenate(_split_terms(x, terms), axis=0), b_bf16)
    return sum(y[i * m:(i + 1) * m] for i in range(terms))


def _inproj_kernel(x_ref, nw_ref, w_ref, zs_ref, zqkv_ref, zg_ref, *, col_chunk):
    x = x_ref[...]
    ms = jnp.mean(x * x, axis=-1, keepdims=True)
    h = (x * lax.rsqrt(ms + NORM_EPS) * nw_ref[...]).astype(BF16)
    c0 = 0
    for out_ref in (zs_ref, zqkv_ref, zg_ref):
        width = out_ref.shape[-1]
        for j in range(0, width, col_chunk):
            z = _dot(h, w_ref[:, c0 + j:c0 + j + col_chunk])
            out_ref[:, j:j + col_chunk] = z.astype(out_ref.dtype)
        c0 += width


def _inproj(x2, norm_w, w_in_bf16, tm=256, col_chunk=256):
    T = x2.shape[0]
    in_cols = w_in_bf16.shape[1]
    return pl.pallas_call(
        functools.partial(_inproj_kernel, col_chunk=col_chunk),
        grid=(T // tm,),
        in_specs=[
            pl.BlockSpec((tm, D_MODEL), lambda i: (i, 0)),
            pl.BlockSpec((1, D_MODEL), lambda i: (0, 0)),
            pl.BlockSpec((D_MODEL, in_cols), lambda i: (0, 0)),
        ],
        out_specs=[
            pl.BlockSpec((tm, SHIFT_COLS), lambda i: (i, 0)),
            pl.BlockSpec((tm, QKV_COLS), lambda i: (i, 0)),
            pl.BlockSpec((tm, GATE_COLS), lambda i: (i, 0)),
        ],
        out_shape=[
            jax.ShapeDtypeStruct((T, SHIFT_COLS), F32),
            jax.ShapeDtypeStruct((T, QKV_COLS), BF16),
            jax.ShapeDtypeStruct((T, GATE_COLS), F32),
        ],
        compiler_params=pltpu.CompilerParams(
            dimension_semantics=("parallel",), vmem_limit_bytes=VMEM_LIMIT),
        name="inproj",
    )(x2, norm_w, w_in_bf16)


def _rwkv_kernel(zs_ref, mu_ref, w0_ref, a0_ref, kk_ref, ka_ref, rk_ref, lnw_ref, lnb_ref,
                 wlora_ref, g2_ref, ones_ref, tri_ref, ya_ref,
                 state_ref, prev_ref, r_s, k_s, v_s, a_s, b_s, ld_s, l_s, y_s, *, ts, group):
    t = pl.program_id(1)

    @pl.when(t == 0)
    def _():
        state_ref[...] = jnp.zeros_like(state_ref)
        prev_ref[...] = jnp.zeros_like(prev_ref)

    z = zs_ref[0]
    row = lax.broadcasted_iota(jnp.int32, (ts, 1), 0)
    zprev = jnp.where(row == 0, prev_ref[...], pltpu.roll(z, 1, axis=0))
    prev_ref[...] = z[ts - 1:ts, :]
    zz = z + (zprev - z) * mu_ref[...]

    W = RW_WIDTH
    r = zz[:, 0:W]
    k = zz[:, W:2 * W]
    v = zz[:, 2 * W:3 * W]
    wa = zz[:, 3 * W:3 * W + LANES]
    gl = zz[:, 3 * W + LANES:3 * W + 2 * LANES]
    lane = lax.broadcasted_iota(jnp.int32, (1, LANES), 1)
    wa_act = jnp.where(lane < DECAY_LORA, jnp.tanh(wa), wa).astype(BF16)
    lora = _dot(wa_act, wlora_ref[...])
    u = -(w0_ref[...] + lora[:, 0:W])
    softplus = jnp.maximum(u, 0.0) + jnp.log1p(jnp.exp(-jnp.abs(u)))
    wlog = -softplus - 0.5
    ld = -jnp.exp(wlog)
    a = _sigmoid(a0_ref[...] + lora[:, W:2 * W])
    g = _dot(_sigmoid(gl).astype(BF16), g2_ref[...])

    ones_blk = ones_ref[...]
    kk = k * kk_ref[...]
    ss = _dot_exact_lhs(kk * kk, ones_blk)
    kk = kk / jnp.maximum(jnp.sqrt(ss), 1e-12)
    k2 = k * (1.0 + (a - 1.0) * ka_ref[...])
    bonus = _dot_exact_lhs(r * k2 * rk_ref[...], ones_blk) * v

    r_s[...] = r
    k_s[...] = k2
    v_s[...] = v
    a_s[...] = -kk
    b_s[...] = kk * a
    ld_s[...] = ld
    l_s[...] = _dot_exact_rhs(tri_ref[...], ld)

    rowi = lax.broadcasted_iota(jnp.int32, (PAIR, PAIR), 0)
    coli = lax.broadcasted_iota(jnp.int32, (PAIR, PAIR), 1)
    strict_lower = rowi > coli
    lower = rowi >= coli
    eye = rowi == coli
    head0 = lane < RW_HEAD

    def stack(xp):
        return jnp.concatenate([jnp.where(head0, xp, 0.0), jnp.where(head0, 0.0, xp)], axis=0)

    def unstack(xs):
        return xs[0:CHUNK] + xs[CHUNK:2 * CHUNK]

    mid = CHUNK // 2 - 1

    def chunk_terms(r0):
        rows = pl.ds(r0, CHUNK)
        lc = l_s[rows, :]
        cm = l_s[pl.ds(r0 + mid, 1), :]
        lend = l_s[pl.ds(r0 + CHUNK - 1, 1), :]
        e_pos = jnp.exp(lc - cm)
        e_neg = jnp.exp(cm - lc)
        e_cm = jnp.exp(cm)
        e_end = e_neg * jnp.exp(lend - cm)
        r_cen = r_s[rows, :] * e_pos
        a_cen = a_s[rows, :] * jnp.exp(lc - ld_s[rows, :] - cm)
        bc = b_s[rows, :]
        kc = k_s[rows, :]
        return dict(rows=rows, p_end=jnp.exp(lend), v=v_s[rows, :],
                    r_cen=r_cen, r_tru=r_cen * e_cm, a_cen=a_cen, a_tru=a_cen * e_cm,
                    b_cen=bc * e_neg, k_cen=kc * e_neg, b_end=bc * e_end, k_end=kc * e_end)

    def chunk_body(c, carry):
        terms = [chunk_terms(pl.multiple_of((c * group + ci) * CHUNK, CHUNK)) for ci in range(group)]
        items = [(t, slice(p * PAIR, (p + 1) * PAIR)) for t in terms for p in range(RW_HEADS // 2)]
        idx = range(len(items))
        sc = [_dot_nt(jnp.concatenate([stack(t["a_cen"][:, ls]), stack(t["r_cen"][:, ls])], axis=0).astype(BF16),
                      jnp.concatenate([stack(t["b_cen"][:, ls]), stack(t["k_cen"][:, ls])], axis=0).astype(BF16))
              for t, ls in items]
        a_ab = [jnp.where(strict_lower, s_[0:PAIR, 0:PAIR], 0.0) for s_ in sc]
        a_ak = [jnp.where(strict_lower, s_[0:PAIR, PAIR:2 * PAIR], 0.0).astype(BF16) for s_ in sc]
        m_rb = [jnp.where(lower, s_[PAIR:2 * PAIR, 0:PAIR], 0.0).astype(BF16) for s_ in sc]
        m_rk = [jnp.where(lower, s_[PAIR:2 * PAIR, PAIR:2 * PAIR], 0.0).astype(BF16) for s_ in sc]
        v_st = [stack(t["v"][:, ls]).astype(BF16) for t, ls in items]
        x = [jnp.concatenate([stack(t["a_tru"][:, ls]), _dot(a_ak[i], v_st[i])], axis=1)
             for i, (t, ls) in enumerate(items)]
        n = a_ab
        steps = int(math.log2(CHUNK))
        for k in range(steps):
            nb = [n_.astype(BF16) for n_ in n]
            x = [x[i] + _dot(nb[i], x[i].astype(BF16)) for i in idx]
            if k + 1 < steps:
                n = [_dot(nb_, nb_) for nb_ in nb]
        xb = [x_.astype(BF16) for x_ in x]
        ry = [_dot(m_rb[i], xb[i]) for i in idx]
        r_new = [unstack(stack(t["r_tru"][:, ls]) + ry[i][:, 0:PAIR]).astype(BF16)
                 for i, (t, ls) in enumerate(items)]
        y0 = [unstack(ry[i][:, PAIR:2 * PAIR] + _dot(m_rk[i], v_st[i])) for i in idx]
        b_st = [stack(t["b_end"][:, ls]).astype(BF16) for t, ls in items]
        k_st = [stack(t["k_end"][:, ls]).astype(BF16) for t, ls in items]
        gh = [_dot_tn(xb[i], b_st[i]) for i in idx]
        h_t = [gh[i][PAIR:2 * PAIR] + _dot_tn(v_st[i], k_st[i]) for i in idx]
        g_t = [gh[i][0:PAIR].astype(BF16) for i in idx]
        for i, (t, ls) in enumerate(items):
            p = i % (RW_HEADS // 2)
            s_old = state_ref[p]
            sb = s_old.astype(BF16)
            y_s[t["rows"], ls] = _dot_nt(r_new[i], sb) + y0[i]
            state_ref[p] = s_old * t["p_end"][:, ls] + _dot(sb, g_t[i]) + h_t[i]
        return carry

    lax.fori_loop(0, ts // (CHUNK * group), chunk_body, 0)

    y = y_s[...]
    inv_n = 1.0 / RW_HEAD
    mean = _dot_exact_lhs(y, ones_blk) * inv_n
    yc = y - mean
    var = _dot_exact_lhs(yc * yc, ones_blk) * inv_n
    yn = yc * lax.rsqrt(var + GN_EPS) * lnw_ref[...] + lnb_ref[...]
    ya_ref[0] = ((yn + bonus) * g).astype(ya_ref.dtype)


def _rwkv(zs3, mu, w0, a0, k_k, k_a, r_k, lnx_w, lnx_b, wlora, g2, ts=256):
    B, S, _ = zs3.shape
    W = RW_WIDTH
    ones_blk = (jnp.arange(W)[:, None] // RW_HEAD == jnp.arange(W)[None, :] // RW_HEAD).astype(BF16)
    ti = jnp.arange(ts)
    tri = ((ti[:, None] // CHUNK == ti[None, :] // CHUNK) & (ti[:, None] >= ti[None, :])).astype(BF16)
    vec = lambda n: pl.BlockSpec((1, n), lambda b, t: (0, 0))
    full = lambda a: pl.BlockSpec(a.shape, lambda b, t: (0,) * a.ndim)
    return pl.pallas_call(
        functools.partial(_rwkv_kernel, ts=ts, group=2),
        grid=(B, S // ts),
        in_specs=[
            pl.BlockSpec((1, ts, SHIFT_COLS), lambda b, t: (b, t, 0)),
            vec(SHIFT_COLS), vec(W), vec(W), vec(W), vec(W), vec(W), vec(W), vec(W),
            full(wlora), full(g2), full(ones_blk), full(tri),
        ],
        out_specs=pl.BlockSpec((1, ts, W), lambda b, t: (b, t, 0)),
        out_shape=jax.ShapeDtypeStruct((B, S, W), BF16),
        scratch_shapes=[
            pltpu.VMEM((RW_HEADS // 2, PAIR, PAIR), F32),
            pltpu.VMEM((1, SHIFT_COLS), F32),
        ] + [pltpu.VMEM((ts, W), F32) for _ in range(8)],
        compiler_params=pltpu.CompilerParams(
            dimension_semantics=("parallel", "arbitrary"), vmem_limit_bytes=VMEM_LIMIT),
        name="rwkv",
    )(zs3, mu, w0, a0, k_k, k_a, r_k, lnx_w, lnx_b, wlora, g2, ones_blk, tri)


def _attn_kernel(slope_ref, lq1_ref, lk1_ref, lq2_ref, lk2_ref, sw_ref, q_ref, k_ref, v_ref, o_ref,
                 m_s, l_s, acc_s, bias_s, *, tq, tk):
    h = pl.program_id(1)
    qi = pl.program_id(2)
    slope = slope_ref[h]
    lam = (jnp.exp(jnp.sum(lq1_ref[...] * lk1_ref[...], axis=-1, keepdims=True))
           - jnp.exp(jnp.sum(lq2_ref[...] * lk2_ref[...], axis=-1, keepdims=True)) + LAM_INIT)

    lane = lax.broadcasted_iota(jnp.int32, (1, LANES), 1)
    map0 = lane < DA_HEAD
    q = q_ref[0] * (1.0 / math.sqrt(DA_HEAD))
    zero = jnp.zeros_like(q)
    qst = jnp.concatenate([jnp.where(map0, q, zero), jnp.where(map0, zero, q)], axis=0)

    m_s[...] = jnp.full_like(m_s, NEG_BIG)
    l_s[...] = jnp.zeros_like(l_s)
    acc_s[...] = jnp.zeros_like(acc_s)

    @pl.when(qi == 0)
    def _():
        krow = lax.broadcasted_iota(jnp.int32, (tk, 2 * tq), 0)
        qcol = lax.broadcasted_iota(jnp.int32, (tk, 2 * tq), 1)
        qcol = jnp.where(qcol >= tq, qcol - tq, qcol)
        bias_s[...] = -slope * (qcol - krow).astype(F32)

    def block(j, diagonal):
        k0 = pl.multiple_of(j * tk, tk)
        kb = k_ref[0, pl.ds(k0, tk), :]
        vb = v_ref[0, pl.ds(k0, tk), :]
        shift = slope * ((qi - j) * tq).astype(F32)
        b = bias_s[...]
        x = _dot_nt(kb, qst) + b
        if diagonal:
            x = jnp.where(b <= 0.0, x, NEG_BIG)
        m_old = m_s[...]
        m_new = jnp.maximum(m_old, jnp.max(x, axis=0, keepdims=True) - shift)
        alpha = jnp.exp(m_old - m_new)
        p = jnp.exp(x - (m_new + shift))
        l_s[...] = alpha * l_s[...] + jnp.sum(p, axis=0, keepdims=True)
        acc_s[...] = alpha * acc_s[...] + _dot_tn(vb, p.astype(BF16))
        m_s[...] = m_new

    def body(j, carry):
        block(j, False)
        return carry

    lax.fori_loop(0, qi, body, 0)
    block(qi, True)

    o = acc_s[...] / l_s[...]
    o = o[:, 0:tq] - lam * o[:, tq:2 * tq]
    o = o * lax.rsqrt(jnp.mean(o * o, axis=0, keepdims=True) + SUBLN_EPS) * sw_ref[...]
    o_ref[0] = (o * (1.0 - LAM_INIT)).T.astype(o_ref.dtype)


def _attn(zqkv3, lam_q1, lam_k1, lam_q2, lam_k2, subln_w, tq=512, tk=512):
    B, S, _ = zqkv3.shape
    assert tq == tk, "the kernel masks only the diagonal block of aligned square tiles"
    slopes = jnp.asarray([2.0 ** (-8.0 * (i + 1) / DA_HEADS) for i in range(DA_HEADS)], F32)
    nqk = DA_QK_WIDTH // LANES
    vec = lambda n: pl.BlockSpec((1, n), lambda b, h, i: (0, 0))
    return pl.pallas_call(
        functools.partial(_attn_kernel, tq=tq, tk=tk),
        grid=(B, DA_HEADS, S // tq),
        in_specs=[
            pl.BlockSpec(memory_space=pltpu.SMEM),
            vec(DA_HEAD), vec(DA_HEAD), vec(DA_HEAD), vec(DA_HEAD),
            pl.BlockSpec((DA_VDIM, 1), lambda b, h, i: (0, 0)),
            pl.BlockSpec((1, tq, LANES), lambda b, h, i: (b, i, h)),
            pl.BlockSpec((1, S, LANES), lambda b, h, i: (b, 0, nqk + h)),
            pl.BlockSpec((1, S, LANES), lambda b, h, i: (b, 0, 2 * nqk + h)),
        ],
        out_specs=pl.BlockSpec((1, tq, LANES), lambda b, h, i: (b, i, h)),
        out_shape=jax.ShapeDtypeStruct((B, S, DA_WIDTH), BF16),
        scratch_shapes=[
            pltpu.VMEM((1, 2 * tq), F32),
            pltpu.VMEM((1, 2 * tq), F32),
            pltpu.VMEM((DA_VDIM, 2 * tq), F32),
            pltpu.VMEM((tk, 2 * tq), F32),
        ],
        compiler_params=pltpu.CompilerParams(
            dimension_semantics=("parallel", "parallel", "arbitrary"), vmem_limit_bytes=VMEM_LIMIT),
        name="attn",
    )(slopes, lam_q1, lam_k1, lam_q2, lam_k2, subln_w, zqkv3, zqkv3, zqkv3)


def _merge_kernel(ya_ref, yb_ref, zg_ref, x_ref, pa_ref, pb_ref, wo_ref, nw_ref, wq_ref,
                  x1_ref, h2t_ref, q_ref):
    pa = _dot(ya_ref[...], pa_ref[...])
    pb = _dot(yb_ref[...], pb_ref[...])
    ga = zg_ref[:, 0:D_MODEL]
    gb = zg_ref[:, D_MODEL:2 * D_MODEL]
    merged = _sigmoid(ga) * pa + _sigmoid(gb) * pb
    x1 = x_ref[...] + _dot(merged.astype(BF16), wo_ref[...])
    x1_ref[...] = x1
    ms = jnp.mean(x1 * x1, axis=-1, keepdims=True)
    h2 = x1 * lax.rsqrt(ms + NORM_EPS) * nw_ref[...]
    h2t_ref[0] = h2.T.astype(BF16)
    q = _dot(h2.astype(BF16), wq_ref[...]).astype(q_ref.dtype)
    for hd in range(PK_HEADS):
        q_ref[hd] = q[:, hd * PK_QDIM:(hd + 1) * PK_QDIM]


def _merge(ya2, yb2, zg, x2, proj_a, proj_b, w_out, norm_w, wq, tm=PEER_TM):
    T = x2.shape[0]
    row = lambda n: pl.BlockSpec((tm, n), lambda i: (i, 0))
    full = lambda a: pl.BlockSpec(a.shape, lambda i: (0,) * a.ndim)
    return pl.pallas_call(
        _merge_kernel,
        grid=(T // tm,),
        in_specs=[row(RW_WIDTH), row(DA_WIDTH), row(GATE_COLS), row(D_MODEL),
                  full(proj_a), full(proj_b), full(w_out), full(norm_w), full(wq)],
        out_specs=[row(D_MODEL),
                   pl.BlockSpec((1, D_MODEL, tm), lambda i: (i, 0, 0)),
                   pl.BlockSpec((PK_HEADS, tm, PK_QDIM), lambda i: (0, i, 0))],
        out_shape=[
            jax.ShapeDtypeStruct((T, D_MODEL), F32),
            jax.ShapeDtypeStruct((T // tm, D_MODEL, tm), BF16),
            jax.ShapeDtypeStruct((PK_HEADS, T, PK_QDIM), BF16),
        ],
        compiler_params=pltpu.CompilerParams(
            dimension_semantics=("parallel",), vmem_limit_bytes=VMEM_LIMIT),
        name="merge",
    )(ya2, yb2, zg, x2, proj_a, proj_b, w_out, norm_w, wq)


_STAIR = tuple(PK_TOPK // (i + 1) for i in range(PK_TOPK))


def _route_kernel(q_ref, keys_ref, r2_ref, e2_ref, n1_ref, c1_ref, v1_s, v2_s):
    half = PK_QDIM // 2
    q = q_ref[0]
    s1 = _dot_nt(keys_ref[0, 0], q[:, 0:half])
    s2 = _dot_nt(keys_ref[0, 1], q[:, half:2 * half])

    rank2 = jnp.full(s2.shape, float(PK_TOPK), F32)
    w1, w2 = s1, s2
    for i in range(PK_TOPK):
        m1 = jnp.max(w1, axis=0, keepdims=True)
        m2 = jnp.max(w2, axis=0, keepdims=True)
        v1_s[i:i + 1, :] = m1
        v2_s[i:i + 1, :] = m2
        w1 = jnp.where(w1 == m1, -jnp.inf, w1)
        hit2 = w2 == m2
        rank2 = jnp.where(hit2, float(i), rank2)
        w2 = jnp.where(hit2, -jnp.inf, w2)
    v1 = v1_s[...]
    v2 = v2_s[...]

    def stair(i, rows):
        jrow = lax.broadcasted_iota(jnp.int32, (rows, 1), 0)
        return jnp.where(jrow < _STAIR[i], v1[i:i + 1] + v2[0:rows], -jnp.inf)

    cand = jnp.concatenate([stair(i, PK_TOPK) for i in range(4)]
                           + [stair(i, 8) for i in range(4, 8)]
                           + [v1[8:PK_TOPK] + v2[0:1]], axis=0)
    work = cand
    tau = None
    for i in range(PK_TOPK):
        tau = jnp.max(work, axis=0, keepdims=True)
        work = jnp.where(work == tau, -jnp.inf, work)
    cmax = v1[0:1] + v2[0:1]
    z = jnp.sum(jnp.where(cand >= tau, jnp.exp(cand - cmax), 0.0), axis=0, keepdims=True)
    n1 = jnp.zeros_like(s1)
    for jj in range(PK_TOPK):
        n1 = jnp.where(s1 + v2[jj:jj + 1] >= tau, float(jj + 1), n1)
    r2_ref[0, 0] = rank2.astype(r2_ref.dtype)
    e2_ref[0, 0] = jnp.exp(s2 - v2[0:1]).astype(e2_ref.dtype)
    n1_ref[0, 0] = n1
    c1_ref[0, 0] = jnp.exp(s1 - v1[0:1]) * (0.5 / z)


def _route(q3, keys, tm=PEER_TM):
    T = q3.shape[1]
    blk = lambda: pl.BlockSpec((1, 1, PK_NKEYS, tm), lambda i, h: (i, h, 0, 0))
    shp = lambda dt: jax.ShapeDtypeStruct((T // tm, PK_HEADS, PK_NKEYS, tm), dt)
    return pl.pallas_call(
        _route_kernel,
        grid=(T // tm, PK_HEADS),
        in_specs=[
            pl.BlockSpec((1, tm, PK_QDIM), lambda i, h: (h, i, 0)),
            pl.BlockSpec((1, 2, PK_NKEYS, PK_QDIM // 2), lambda i, h: (h, 0, 0, 0)),
        ],
        out_specs=[blk(), blk(), blk(), blk()],
        out_shape=[shp(BF16), shp(BF16), shp(F32), shp(F32)],
        scratch_shapes=[pltpu.VMEM((PK_TOPK, tm), F32), pltpu.VMEM((PK_TOPK, tm), F32)],
        compiler_params=pltpu.CompilerParams(
            dimension_semantics=("parallel", "arbitrary"), vmem_limit_bytes=VMEM_LIMIT),
        name="route",
    )(q3, keys)


def _peer_kernel(ht_ref, u_ref, vt_ref, r2_ref, e2_ref, n1_ref, c1_ref, x1_ref, nw_ref, o_ref,
                 acc_s, aw0_s, aw1_s, act0_s, act1_s, *, te, nst):
    s = pl.program_id(0)
    jp = lax.rem(s + (nst - 1), jnp.int32(nst))

    @pl.when(s == 0)
    def _():
        acc_s[...] = jnp.zeros_like(acc_s)
        aw1_s[...] = jnp.zeros_like(aw1_s)

    group = act0_s.shape[0]
    npiece = te // group
    e_per_piece = group // PK_NKEYS
    tm = ht_ref.shape[2]
    drain_every = 2
    drows = D_MODEL * drain_every // npiece
    restart = jp == 0
    act_bufs = (act0_s, act1_s)
    tile = (BF16_SUBLANES, tm)
    ntile = PK_NKEYS // BF16_SUBLANES

    def u_proj(i):
        act = _dot(u_ref[i * group:(i + 1) * group, :], ht_ref[0])
        act_bufs[i % 2][...] = act.astype(BF16)

    def step(aw_build, aw_drain):
        u_proj(0)
        for i in range(npiece):
            if i + 1 < npiece:
                u_proj(i + 1)
            for half in range(e_per_piece):
                e = i * e_per_piece + half
                a = act_bufs[i % 2][half * PK_NKEYS:(half + 1) * PK_NKEYS, :]
                gelu2 = a * (1.0 + lax.erf(a * (1.0 / math.sqrt(2.0))))
                w = None
                for hd in range(PK_HEADS):
                    n = jnp.broadcast_to(n1_ref[0, hd, e:e + 1, :], tile).astype(BF16)
                    c = jnp.broadcast_to(c1_ref[0, hd, e:e + 1, :], tile).astype(BF16)
                    r2 = r2_ref[0, hd].reshape(ntile, *tile)
                    e2 = e2_ref[0, hd].reshape(ntile, *tile)
                    wh = jnp.where(r2 < n[None], e2 * c[None], jnp.zeros(r2.shape, BF16))
                    w = wh if w is None else w + wh
                aw_build[e * PK_NKEYS:(e + 1) * PK_NKEYS, :] = gelu2 * w.reshape(a.shape)
            if (i + 1) % drain_every == 0:
                d = (i + 1) // drain_every - 1
                rs = slice(d * drows, (d + 1) * drows)
                part = _dot(vt_ref[rs, :], aw_drain[...])
                acc_s[rs, :] = jnp.where(restart, 0.0, acc_s[rs, :]) + part

    parity = lax.rem(s, jnp.int32(2))

    @pl.when(parity == 0)
    def _():
        step(aw0_s, aw1_s)

    @pl.when(parity == 1)
    def _():
        step(aw1_s, aw0_s)

    @pl.when(jnp.logical_and(jp == nst - 1, s > 0))
    def _():
        x2 = x1_ref[...] + acc_s[...].T
        ms = jnp.mean(x2 * x2, axis=-1, keepdims=True)
        o_ref[...] = x2 * lax.rsqrt(ms + NORM_EPS) * nw_ref[...]


def _peer(h2t, u_bf16, vt_bf16, r2, e2, n1, c1, x1, final_w, te=2048):
    ntile, _, tm = h2t.shape
    T = ntile * tm
    group = max(PK_NKEYS, PEER_PIECE_ELEMS // tm)
    nst = PK_EXPERTS // te
    tile_ab = lambda s: jnp.minimum(s // nst, ntile - 1)
    tile_c = lambda s: jnp.maximum(s - 1, 0) // nst
    rt = lambda: pl.BlockSpec((1, PK_HEADS, PK_NKEYS, tm), lambda s: (tile_ab(s), 0, 0, 0))
    rows = lambda: pl.BlockSpec((1, PK_HEADS, te // PK_NKEYS, tm), lambda s: (tile_ab(s), 0, s % nst, 0))
    return pl.pallas_call(
        functools.partial(_peer_kernel, te=te, nst=nst),
        grid=(ntile * nst + 1,),
        in_specs=[
            pl.BlockSpec((1, D_MODEL, tm), lambda s: (tile_ab(s), 0, 0)),
            pl.BlockSpec((te, D_MODEL), lambda s: (s % nst, 0)),
            pl.BlockSpec((D_MODEL, te), lambda s: (0, jnp.maximum(s - 1, 0) % nst)),
            rt(), rt(), rows(), rows(),
            pl.BlockSpec((tm, D_MODEL), lambda s: (tile_c(s), 0)),
            pl.BlockSpec((1, D_MODEL), lambda s: (0, 0)),
        ],
        out_specs=pl.BlockSpec((tm, D_MODEL), lambda s: (tile_c(s), 0)),
        out_shape=jax.ShapeDtypeStruct((T, D_MODEL), F32),
        scratch_shapes=[pltpu.VMEM((D_MODEL, tm), F32),
                        pltpu.VMEM((te, tm), BF16), pltpu.VMEM((te, tm), BF16),
                        pltpu.VMEM((group, tm), BF16), pltpu.VMEM((group, tm), BF16)],
        compiler_params=pltpu.CompilerParams(
            dimension_semantics=("arbitrary",), vmem_limit_bytes=VMEM_LIMIT),
        name="peer",
    )(h2t, u_bf16, vt_bf16, r2, e2, n1, c1, x1, final_w)


def _lora_blockdiag(w2, a2):
    z = jnp.zeros_like(w2)
    return jnp.concatenate([jnp.concatenate([w2, z], axis=1), jnp.concatenate([z, a2], axis=1)], axis=0)


def kernel(x, norm_mix_w, w_in, shift_mu, w0, w2, a0, a2, g2, k_k, k_a, r_k, lnx_w, lnx_b, lam_q1, lam_k1, lam_q2, lam_k2, subln_w, proj_a, proj_b, w_out, norm_ffn_w, peer_wq, peer_keys, peer_u, peer_v, final_norm_w):
    B, S, D = x.shape
    T = B * S
    depth = w_in.shape[0]
    assert depth == 1 and D == D_MODEL
    l = 0
    x2 = x.reshape(T, D)
    zs, zqkv, zg = _inproj(x2, norm_mix_w[l][None], w_in[l].astype(BF16))
    ya = _rwkv(zs.reshape(B, S, SHIFT_COLS), shift_mu[l][None], w0[l][None], a0[l][None], k_k[l][None],
               k_a[l][None], r_k[l].reshape(1, RW_WIDTH), lnx_w[l][None], lnx_b[l][None],
               _lora_blockdiag(w2[l], a2[l]).astype(BF16), g2[l].astype(BF16))
    yb = _attn(zqkv.reshape(B, S, QKV_COLS), lam_q1[l][None], lam_k1[l][None], lam_q2[l][None],
               lam_k2[l][None], subln_w[l][:, None])
    x1, h2t, q = _merge(ya.reshape(T, RW_WIDTH), yb.reshape(T, DA_WIDTH), zg, x2,
                       proj_a[l].astype(BF16), proj_b[l].astype(BF16), w_out[l].astype(BF16),
                       norm_ffn_w[l][None], peer_wq[l].astype(BF16))
    r2, e2, n1, c1 = _route(q, peer_keys[l].astype(BF16))
    out = _peer(h2t, peer_u[l].astype(BF16), peer_v[l].T.astype(BF16), r2, e2, n1, c1, x1,
                final_norm_w[None])
    return out.reshape(B, S, D)
```

```python
import functools
import math

import jax
import jax.numpy as jnp
from jax import lax
from jax.experimental import pallas as pl
from jax.experimental.pallas import tpu as pltpu

F32 = jnp.float32
BF16 = jnp.bfloat16

D_MODEL = 1024
RW_HEADS = 8
RW_HEAD = 64
RW_WIDTH = RW_HEADS * RW_HEAD
DECAY_LORA = 64
ICLR_LORA = 64
GATE_LORA = 128
DA_HEADS = 4
DA_HEAD = 64
DA_VDIM = 2 * DA_HEAD
DA_QK_WIDTH = DA_HEADS * 2 * DA_HEAD
DA_WIDTH = DA_HEADS * DA_VDIM
PK_HEADS = 8
PK_NKEYS = 128
PK_QDIM = 256
PK_TOPK = 16
PK_EXPERTS = PK_NKEYS * PK_NKEYS
NORM_EPS = 1e-6
GN_EPS = 64e-5
SUBLN_EPS = 1e-5
SHIFT_COLS = 3 * RW_WIDTH + DECAY_LORA + ICLR_LORA + GATE_LORA
QKV_COLS = 2 * DA_QK_WIDTH + DA_WIDTH
GATE_COLS = 2 * D_MODEL
LAM_INIT = 0.8 - 0.6 * math.exp(0.0)

LANES = 128
BF16_SUBLANES = 16
CHUNK = 64
PAIR = 2 * RW_HEAD
PEER_PIECE_ELEMS = 128 * 1024
PEER_TM = 512
NEG_BIG = -1e30
VMEM_LIMIT = 56 * 1024 * 1024

NT_DIMS = (((1,), (1,)), ((), ()))
TN_DIMS = (((0,), (0,)), ((), ()))


def _dot(a, b):
    return jnp.dot(a, b, preferred_element_type=F32)


def _dot_nt(a, b):
    return lax.dot_general(a, b, NT_DIMS, preferred_element_type=F32)


def _dot_tn(a, b):
    return lax.dot_general(a, b, TN_DIMS, preferred_element_type=F32)


def _sigmoid(x):
    return 1.0 / (1.0 + jnp.exp(-x))


def _split_terms(x, terms):
    parts = []
    rest = x
    for i in range(terms):
        p = rest.astype(BF16)
        parts.append(p)
        if i + 1 < terms:
            rest = rest - p.astype(F32)
    return parts


def _dot_exact_rhs(a_bf16, x, terms=3):
    n = x.shape[1]
    y = _dot(a_bf16, jnp.concatenate(_split_terms(x, terms), axis=1))
    return sum(y[:, i * n:(i + 1) * n] for i in range(terms))


def _dot_exact_lhs(x, b_bf16, terms=2):
    m = x.shape[0]
    y = _dot(jnp.concatenate(_split_terms(x, terms), axis=0), b_bf16)
    return sum(y[i * m:(i + 1) * m] for i in range(terms))


def _inproj_kernel(x_ref, nw_ref, w_ref, zs_ref, zqkv_ref, zg_ref, *, col_chunk):
    x = x_ref[...]
    ms = jnp.mean(x * x, axis=-1, keepdims=True)
    h = (x * lax.rsqrt(ms + NORM_EPS) * nw_ref[...]).astype(BF16)
    c0 = 0
    for out_ref in (zs_ref, zqkv_ref, zg_ref):
        width = out_ref.shape[-1]
        for j in range(0, width, col_chunk):
            z = _dot(h, w_ref[:, c0 + j:c0 + j + col_chunk])
            out_ref[:, j:j + col_chunk] = z.astype(out_ref.dtype)
        c0 += width


def _inproj(x2, norm_w, w_in_bf16, tm=256, col_chunk=256):
    T = x2.shape[0]
    in_cols = w_in_bf16.shape[1]
    return pl.pallas_call(
        functools.partial(_inproj_kernel, col_chunk=col_chunk),
        grid=(T // tm,),
        in_specs=[
            pl.BlockSpec((tm, D_MODEL), lambda i: (i, 0)),
            pl.BlockSpec((1, D_MODEL), lambda i: (0, 0)),
            pl.BlockSpec((D_MODEL, in_cols), lambda i: (0, 0)),
        ],
        out_specs=[
            pl.BlockSpec((tm, SHIFT_COLS), lambda i: (i, 0)),
            pl.BlockSpec((tm, QKV_COLS), lambda i: (i, 0)),
            pl.BlockSpec((tm, GATE_COLS), lambda i: (i, 0)),
        ],
        out_shape=[
            jax.ShapeDtypeStruct((T, SHIFT_COLS), F32),
            jax.ShapeDtypeStruct((T, QKV_COLS), BF16),
            jax.ShapeDtypeStruct((T, GATE_COLS), F32),
        ],
        compiler_params=pltpu.CompilerParams(
            dimension_semantics=("parallel",), vmem_limit_bytes=VMEM_LIMIT),
        name="inproj",
    )(x2, norm_w, w_in_bf16)


def _rwkv_kernel(zs_ref, mu_ref, w0_ref, a0_ref, kk_ref, ka_ref, rk_ref, lnw_ref, lnb_ref,
                 wlora_ref, g2_ref, ones_ref, tri_ref, ya_ref,
                 state_ref, prev_ref, r_s, k_s, v_s, a_s, b_s, ld_s, l_s, y_s, *, ts, group):
    t = pl.program_id(1)

    @pl.when(t == 0)
    def _():
        state_ref[...] = jnp.zeros_like(state_ref)
        prev_ref[...] = jnp.zeros_like(prev_ref)

    z = zs_ref[0]
    row = lax.broadcasted_iota(jnp.int32, (ts, 1), 0)
    zprev = jnp.where(row == 0, prev_ref[...], pltpu.roll(z, 1, axis=0))
    prev_ref[...] = z[ts - 1:ts, :]
    zz = z + (zprev - z) * mu_ref[...]

    W = RW_WIDTH
    r = zz[:, 0:W]
    k = zz[:, W:2 * W]
    v = zz[:, 2 * W:3 * W]
    wa = zz[:, 3 * W:3 * W + LANES]
    gl = zz[:, 3 * W + LANES:3 * W + 2 * LANES]
    lane = lax.broadcasted_iota(jnp.int32, (1, LANES), 1)
    wa_act = jnp.where(lane < DECAY_LORA, jnp.tanh(wa), wa).astype(BF16)
    lora = _dot(wa_act, wlora_ref[...])
    ld = -math.exp(-0.5) * _sigmoid(w0_ref[...] + lora[:, 0:W])
    a = _sigmoid(a0_ref[...] + lora[:, W:2 * W])
    g = _dot(_sigmoid(gl).astype(BF16), g2_ref[...])

    ones_blk = ones_ref[...]
    kk = k * kk_ref[...]
    ss = _dot_exact_lhs(kk * kk, ones_blk)
    kk = kk * lax.rsqrt(jnp.maximum(ss, 1e-24))
    k2 = k * (1.0 + (a - 1.0) * ka_ref[...])
    bonus = _dot_exact_lhs(r * k2 * rk_ref[...], ones_blk) * v

    r_s[...] = r
    k_s[...] = k2
    v_s[...] = v
    a_s[...] = -kk
    b_s[...] = kk * a
    ld_s[...] = ld
    l_s[...] = _dot_exact_rhs(tri_ref[...], ld)

    rowi = lax.broadcasted_iota(jnp.int32, (PAIR, PAIR), 0)
    coli = lax.broadcasted_iota(jnp.int32, (PAIR, PAIR), 1)
    strict_lower = rowi > coli
    lower = rowi >= coli
    eye = rowi == coli
    head0 = lane < RW_HEAD

    def stack(xp):
        return jnp.concatenate([jnp.where(head0, xp, 0.0), jnp.where(head0, 0.0, xp)], axis=0)

    def unstack(xs):
        return xs[0:CHUNK] + xs[CHUNK:2 * CHUNK]

    mid = CHUNK // 2 - 1

    def chunk_terms(r0):
        rows = pl.ds(r0, CHUNK)
        lc = l_s[rows, :]
        cm = l_s[pl.ds(r0 + mid, 1), :]
        lend = l_s[pl.ds(r0 + CHUNK - 1, 1), :]
        e_pos = jnp.exp(lc - cm)
        e_neg = jnp.exp(cm - lc)
        e_cm = jnp.exp(cm)
        e_end = e_neg * jnp.exp(lend - cm)
        r_cen = r_s[rows, :] * e_pos
        a_cen = a_s[rows, :] * jnp.exp(lc - ld_s[rows, :] - cm)
        bc = b_s[rows, :]
        kc = k_s[rows, :]
        return dict(rows=rows, p_end=jnp.exp(lend), v=v_s[rows, :],
                    r_cen=r_cen, r_tru=r_cen * e_cm, a_cen=a_cen, a_tru=a_cen * e_cm,
                    b_cen=bc * e_neg, k_cen=kc * e_neg, b_end=bc * e_end, k_end=kc * e_end)

    def chunk_body(c, carry):
        terms = [chunk_terms(pl.multiple_of((c * group + ci) * CHUNK, CHUNK)) for ci in range(group)]
        items = [(t, slice(p * PAIR, (p + 1) * PAIR)) for t in terms for p in range(RW_HEADS // 2)]
        idx = range(len(items))
        sc = [_dot_nt(jnp.concatenate([stack(t["a_cen"][:, ls]), stack(t["r_cen"][:, ls])], axis=0).astype(BF16),
                      jnp.concatenate([stack(t["b_cen"][:, ls]), stack(t["k_cen"][:, ls])], axis=0).astype(BF16))
              for t, ls in items]
        a_ab = [jnp.where(strict_lower, s_[0:PAIR, 0:PAIR], 0.0) for s_ in sc]
        a_ak = [jnp.where(strict_lower, s_[0:PAIR, PAIR:2 * PAIR], 0.0).astype(BF16) for s_ in sc]
        m_rb = [jnp.where(lower, s_[PAIR:2 * PAIR, 0:PAIR], 0.0).astype(BF16) for s_ in sc]
        m_rk = [jnp.where(lower, s_[PAIR:2 * PAIR, PAIR:2 * PAIR], 0.0).astype(BF16) for s_ in sc]
        v_st = [stack(t["v"][:, ls]).astype(BF16) for t, ls in items]
        x = [jnp.concatenate([stack(t["a_tru"][:, ls]), _dot(a_ak[i], v_st[i])], axis=1)
             for i, (t, ls) in enumerate(items)]
        n = a_ab
        steps = int(math.log2(CHUNK))
        for k in range(steps):
            nb = [n_.astype(BF16) for n_ in n]
            x = [x[i] + _dot(nb[i], x[i].astype(BF16)) for i in idx]
            if k + 1 < steps:
                n = [_dot(nb_, nb_) for nb_ in nb]
        xb = [x_.astype(BF16) for x_ in x]
        ry = [_dot(m_rb[i], xb[i]) for i in idx]
        r_new = [unstack(stack(t["r_tru"][:, ls]) + ry[i][:, 0:PAIR]).astype(BF16)
                 for i, (t, ls) in enumerate(items)]
        y0 = [unstack(ry[i][:, PAIR:2 * PAIR] + _dot(m_rk[i], v_st[i])) for i in idx]
        b_st = [stack(t["b_end"][:, ls]).astype(BF16) for t, ls in items]
        k_st = [stack(t["k_end"][:, ls]).astype(BF16) for t, ls in items]
        gh = [_dot_tn(xb[i], b_st[i]) for i in idx]
        h_t = [gh[i][PAIR:2 * PAIR] + _dot_tn(v_st[i], k_st[i]) for i in idx]
        g_t = [gh[i][0:PAIR].astype(BF16) for i in idx]
        for i, (t, ls) in enumerate(items):
            p = i % (RW_HEADS // 2)
            s_old = state_ref[p]
            sb = s_old.astype(BF16)
            y_s[t["rows"], ls] = _dot_nt(r_new[i], sb) + y0[i]
            state_ref[p] = s_old * t["p_end"][:, ls] + _dot(sb, g_t[i]) + h_t[i]
        return carry

    lax.fori_loop(0, ts // (CHUNK * group), chunk_body, 0)

    y = y_s[...]
    inv_n = 1.0 / RW_HEAD
    mean = _dot_exact_lhs(y, ones_blk) * inv_n
    yc = y - mean
    var = _dot_exact_lhs(yc * yc, ones_blk) * inv_n
    yn = yc * lax.rsqrt(var + GN_EPS) * lnw_ref[...] + lnb_ref[...]
    ya_ref[0] = ((yn + bonus) * g).astype(ya_ref.dtype)


def _rwkv(zs3, mu, w0, a0, k_k, k_a, r_k, lnx_w, lnx_b, wlora, g2, ts=256):
    B, S, _ = zs3.shape
    W = RW_WIDTH
    ones_blk = (jnp.arange(W)[:, None] // RW_HEAD == jnp.arange(W)[None, :] // RW_HEAD).astype(BF16)
    ti = jnp.arange(ts)
    tri = ((ti[:, None] // CHUNK == ti[None, :] // CHUNK) & (ti[:, None] >= ti[None, :])).astype(BF16)
    vec = lambda n: pl.BlockSpec((1, n), lambda b, t: (0, 0))
    full = lambda a: pl.BlockSpec(a.shape, lambda b, t: (0,) * a.ndim)
    return pl.pallas_call(
        functools.partial(_rwkv_kernel, ts=ts, group=2),
        grid=(B, S // ts),
        in_specs=[
            pl.BlockSpec((1, ts, SHIFT_COLS), lambda b, t: (b, t, 0)),
            vec(SHIFT_COLS), vec(W), vec(W), vec(W), vec(W), vec(W), vec(W), vec(W),
            full(wlora), full(g2), full(ones_blk), full(tri),
        ],
        out_specs=pl.BlockSpec((1, ts, W), lambda b, t: (b, t, 0)),
        out_shape=jax.ShapeDtypeStruct((B, S, W), BF16),
        scratch_shapes=[
            pltpu.VMEM((RW_HEADS // 2, PAIR, PAIR), F32),
            pltpu.VMEM((1, SHIFT_COLS), F32),
        ] + [pltpu.VMEM((ts, W), F32) for _ in range(8)],
        compiler_params=pltpu.CompilerParams(
            dimension_semantics=("parallel", "arbitrary"), vmem_limit_bytes=VMEM_LIMIT),
        name="rwkv",
    )(zs3, mu, w0, a0, k_k, k_a, r_k, lnx_w, lnx_b, wlora, g2, ones_blk, tri)


def _attn_kernel(slope_ref, lq1_ref, lk1_ref, lq2_ref, lk2_ref, sw_ref, q_ref, k_ref, v_ref, o_ref,
                 m_s, l_s, acc_s, bias_s, *, tq, tk):
    h = pl.program_id(1)
    qi = pl.program_id(2)
    slope = slope_ref[h]
    lam = (jnp.exp(jnp.sum(lq1_ref[...] * lk1_ref[...], axis=-1, keepdims=True))
           - jnp.exp(jnp.sum(lq2_ref[...] * lk2_ref[...], axis=-1, keepdims=True)) + LAM_INIT)

    lane = lax.broadcasted_iota(jnp.int32, (1, LANES), 1)
    map0 = lane < DA_HEAD
    q = q_ref[0] * (1.0 / math.sqrt(DA_HEAD))
    zero = jnp.zeros_like(q)
    qst = jnp.concatenate([jnp.where(map0, q, zero), jnp.where(map0, zero, q)], axis=0)

    m_s[...] = jnp.full_like(m_s, NEG_BIG)
    l_s[...] = jnp.zeros_like(l_s)
    acc_s[...] = jnp.zeros_like(acc_s)

    @pl.when(qi == 0)
    def _():
        krow = lax.broadcasted_iota(jnp.int32, (tk, 2 * tq), 0)
        qcol = lax.broadcasted_iota(jnp.int32, (tk, 2 * tq), 1)
        qcol = jnp.where(qcol >= tq, qcol - tq, qcol)
        bias_s[...] = -slope * (qcol - krow).astype(F32)

    def block(j, diagonal):
        k0 = pl.multiple_of(j * tk, tk)
        kb = k_ref[0, pl.ds(k0, tk), :]
        vb = v_ref[0, pl.ds(k0, tk), :]
        shift = slope * ((qi - j) * tq).astype(F32)
        b = bias_s[...]
        x = _dot_nt(kb, qst) + b
        if diagonal:
            x = jnp.where(b <= 0.0, x, NEG_BIG)
        m_old = m_s[...]
        m_new = jnp.maximum(m_old, jnp.max(x, axis=0, keepdims=True) - shift)
        alpha = jnp.exp(m_old - m_new)
        p = jnp.exp(x - (m_new + shift))
        l_s[...] = alpha * l_s[...] + jnp.sum(p, axis=0, keepdims=True)
        acc_s[...] = alpha * acc_s[...] + _dot_tn(vb, p.astype(BF16))
        m_s[...] = m_new

    def body(j, carry):
        block(j, False)
        return carry

    lax.fori_loop(0, qi, body, 0)
    block(qi, True)

    o = acc_s[...] / l_s[...]
    o = o[:, 0:tq] - lam * o[:, tq:2 * tq]
    o = o * lax.rsqrt(jnp.mean(o * o, axis=0, keepdims=True) + SUBLN_EPS) * sw_ref[...]
    o_ref[0] = (o * (1.0 - LAM_INIT)).T.astype(o_ref.dtype)


def _attn(zqkv3, lam_q1, lam_k1, lam_q2, lam_k2, subln_w, tq=512, tk=512):
    B, S, _ = zqkv3.shape
    assert tq == tk, "the kernel masks only the diagonal block of aligned square tiles"
    slopes = jnp.asarray([2.0 ** (-8.0 * (i + 1) / DA_HEADS) for i in range(DA_HEADS)], F32)
    nqk = DA_QK_WIDTH // LANES
    vec = lambda n: pl.BlockSpec((1, n), lambda b, h, i: (0, 0))
    return pl.pallas_call(
        functools.partial(_attn_kernel, tq=tq, tk=tk),
        grid=(B, DA_HEADS, S // tq),
        in_specs=[
            pl.BlockSpec(memory_space=pltpu.SMEM),
            vec(DA_HEAD), vec(DA_HEAD), vec(DA_HEAD), vec(DA_HEAD),
            pl.BlockSpec((DA_VDIM, 1), lambda b, h, i: (0, 0)),
            pl.BlockSpec((1, tq, LANES), lambda b, h, i: (b, i, h)),
            pl.BlockSpec((1, S, LANES), lambda b, h, i: (b, 0, nqk + h)),
            pl.BlockSpec((1, S, LANES), lambda b, h, i: (b, 0, 2 * nqk + h)),
        ],
        out_specs=pl.BlockSpec((1, tq, LANES), lambda b, h, i: (b, i, h)),
        out_shape=jax.ShapeDtypeStruct((B, S, DA_WIDTH), BF16),
        scratch_shapes=[
            pltpu.VMEM((1, 2 * tq), F32),
            pltpu.VMEM((1, 2 * tq), F32),
            pltpu.VMEM((DA_VDIM, 2 * tq), F32),
            pltpu.VMEM((tk, 2 * tq), F32),
        ],
        compiler_params=pltpu.CompilerParams(
            dimension_semantics=("parallel", "parallel", "arbitrary"), vmem_limit_bytes=VMEM_LIMIT),
        name="attn",
    )(slopes, lam_q1, lam_k1, lam_q2, lam_k2, subln_w, zqkv3, zqkv3, zqkv3)


def _merge_kernel(ya_ref, yb_ref, zg_ref, x_ref, pa_ref, pb_ref, wo_ref, nw_ref, wq_ref,
                  x1_ref, h2t_ref, q_ref):
    pa = _dot(ya_ref[...], pa_ref[...])
    pb = _dot(yb_ref[...], pb_ref[...])
    ga = zg_ref[:, 0:D_MODEL]
    gb = zg_ref[:, D_MODEL:2 * D_MODEL]
    merged = _sigmoid(ga) * pa + _sigmoid(gb) * pb
    x1 = x_ref[...] + _dot(merged.astype(BF16), wo_ref[...])
    x1_ref[...] = x1
    ms = jnp.mean(x1 * x1, axis=-1, keepdims=True)
    h2 = x1 * lax.rsqrt(ms + NORM_EPS) * nw_ref[...]
    h2t_ref[0] = h2.T.astype(BF16)
    q = _dot(h2.astype(BF16), wq_ref[...]).astype(q_ref.dtype)
    for hd in range(PK_HEADS):
        q_ref[hd] = q[:, hd * PK_QDIM:(hd + 1) * PK_QDIM]


def _merge(ya2, yb2, zg, x2, proj_a, proj_b, w_out, norm_w, wq, tm=PEER_TM):
    T = x2.shape[0]
    row = lambda n: pl.BlockSpec((tm, n), lambda i: (i, 0))
    full = lambda a: pl.BlockSpec(a.shape, lambda i: (0,) * a.ndim)
    return pl.pallas_call(
        _merge_kernel,
        grid=(T // tm,),
        in_specs=[row(RW_WIDTH), row(DA_WIDTH), row(GATE_COLS), row(D_MODEL),
                  full(proj_a), full(proj_b), full(w_out), full(norm_w), full(wq)],
        out_specs=[row(D_MODEL),
                   pl.BlockSpec((1, D_MODEL, tm), lambda i: (i, 0, 0)),
                   pl.BlockSpec((PK_HEADS, tm, PK_QDIM), lambda i: (0, i, 0))],
        out_shape=[
            jax.ShapeDtypeStruct((T, D_MODEL), F32),
            jax.ShapeDtypeStruct((T // tm, D_MODEL, tm), BF16),
            jax.ShapeDtypeStruct((PK_HEADS, T, PK_QDIM), BF16),
        ],
        compiler_params=pltpu.CompilerParams(
            dimension_semantics=("parallel",), vmem_limit_bytes=VMEM_LIMIT),
        name="merge",
    )(ya2, yb2, zg, x2, proj_a, proj_b, w_out, norm_w, wq)


_STAIR = tuple(PK_TOPK // (i + 1) for i in range(PK_TOPK))


def _route_kernel(q_ref, keys_ref, r2_ref, e2_ref, n1_ref, c1_ref, v1_s, v2_s):
    half = PK_QDIM // 2
    q = q_ref[0]
    s1 = _dot_nt(keys_ref[0, 0], q[:, 0:half])
    s2 = _dot_nt(keys_ref[0, 1], q[:, half:2 * half])

    rank2 = jnp.full(s2.shape, float(PK_TOPK), F32)
    w1, w2 = s1, s2
    for i in range(PK_TOPK):
        m1 = jnp.max(w1, axis=0, keepdims=True)
        m2 = jnp.max(w2, axis=0, keepdims=True)
        v1_s[i:i + 1, :] = m1
        v2_s[i:i + 1, :] = m2
        w1 = jnp.where(w1 == m1, -jnp.inf, w1)
        hit2 = w2 == m2
        rank2 = jnp.where(hit2, float(i), rank2)
        w2 = jnp.where(hit2, -jnp.inf, w2)
    v1 = v1_s[...]
    v2 = v2_s[...]

    def stair(i, rows):
        jrow = lax.broadcasted_iota(jnp.int32, (rows, 1), 0)
        return jnp.where(jrow < _STAIR[i], v1[i:i + 1] + v2[0:rows], -jnp.inf)

    cand = jnp.concatenate([stair(i, PK_TOPK) for i in range(4)]
                           + [stair(i, 8) for i in range(4, 8)]
                           + [v1[8:PK_TOPK] + v2[0:1]], axis=0)
    work = cand
    tau = None
    for i in range(PK_TOPK):
        tau = jnp.max(work, axis=0, keepdims=True)
        work = jnp.where(work == tau, -jnp.inf, work)
    cmax = v1[0:1] + v2[0:1]
    z = jnp.sum(jnp.where(cand >= tau, jnp.exp(cand - cmax), 0.0), axis=0, keepdims=True)
    n1 = jnp.zeros_like(s1)
    for jj in range(PK_TOPK):
        n1 = jnp.where(s1 + v2[jj:jj + 1] >= tau, float(jj + 1), n1)
    r2_ref[0, 0] = rank2.astype(r2_ref.dtype)
    e2_ref[0, 0] = jnp.exp(s2 - v2[0:1]).astype(e2_ref.dtype)
    n1_ref[0, 0] = n1
    c1_ref[0, 0] = jnp.exp(s1 - v1[0:1]) * (0.5 / z)


def _route(q3, keys, tm=PEER_TM):
    T = q3.shape[1]
    blk = lambda: pl.BlockSpec((1, 1, PK_NKEYS, tm), lambda i, h: (i, h, 0, 0))
    shp = lambda dt: jax.ShapeDtypeStruct((T // tm, PK_HEADS, PK_NKEYS, tm), dt)
    return pl.pallas_call(
        _route_kernel,
        grid=(T // tm, PK_HEADS),
        in_specs=[
            pl.BlockSpec((1, tm, PK_QDIM), lambda i, h: (h, i, 0)),
            pl.BlockSpec((1, 2, PK_NKEYS, PK_QDIM // 2), lambda i, h: (h, 0, 0, 0)),
        ],
        out_specs=[blk(), blk(), blk(), blk()],
        out_shape=[shp(BF16), shp(BF16), shp(F32), shp(F32)],
        scratch_shapes=[pltpu.VMEM((PK_TOPK, tm), F32), pltpu.VMEM((PK_TOPK, tm), F32)],
        compiler_params=pltpu.CompilerParams(
            dimension_semantics=("parallel", "arbitrary"), vmem_limit_bytes=VMEM_LIMIT),
        name="route",
    )(q3, keys)


def _peer_kernel(ht_ref, u_ref, vt_ref, r2_ref, e2_ref, n1_ref, c1_ref, x1_ref, nw_ref, o_ref,
                 acc_s, aw0_s, aw1_s, act0_s, act1_s, *, te, nst):
    s = pl.program_id(0)
    jp = lax.rem(s + (nst - 1), jnp.int32(nst))

    @pl.when(s == 0)
    def _():
        acc_s[...] = jnp.zeros_like(acc_s)
        aw1_s[...] = jnp.zeros_like(aw1_s)

    group = act0_s.shape[0]
    npiece = te // group
    e_per_piece = group // PK_NKEYS
    tm = ht_ref.shape[2]
    drain_every = 2
    drows = D_MODEL * drain_every // npiece
    restart = jp == 0
    act_bufs = (act0_s, act1_s)
    tile = (BF16_SUBLANES, tm)
    ntile = PK_NKEYS // BF16_SUBLANES

    def u_proj(i):
        act = _dot(u_ref[i * group:(i + 1) * group, :], ht_ref[0])
        act_bufs[i % 2][...] = act.astype(BF16)

    def step(aw_build, aw_drain):
        u_proj(0)
        for i in range(npiece):
            if i + 1 < npiece:
                u_proj(i + 1)
            for half in range(e_per_piece):
                e = i * e_per_piece + half
                a = act_bufs[i % 2][half * PK_NKEYS:(half + 1) * PK_NKEYS, :]
                gelu2 = a * (1.0 + lax.erf(a * (1.0 / math.sqrt(2.0))))
                w = None
                for hd in range(PK_HEADS):
                    n = jnp.broadcast_to(n1_ref[0, hd, e:e + 1, :], tile).astype(BF16)
                    c = jnp.broadcast_to(c1_ref[0, hd, e:e + 1, :], tile).astype(BF16)
                    r2 = r2_ref[0, hd].reshape(ntile, *tile)
                    e2 = e2_ref[0, hd].reshape(ntile, *tile)
                    wh = jnp.where(r2 < n[None], e2 * c[None], jnp.zeros(r2.shape, BF16))
                    w = wh if w is None else w + wh
                aw_build[e * PK_NKEYS:(e + 1) * PK_NKEYS, :] = gelu2 * w.reshape(a.shape)
            if (i + 1) % drain_every == 0:
                d = (i + 1) // drain_every - 1
                rs = slice(d * drows, (d + 1) * drows)
                part = _dot(vt_ref[rs, :], aw_drain[...])
                acc_s[rs, :] = jnp.where(restart, 0.0, acc_s[rs, :]) + part

    parity = lax.rem(s, jnp.int32(2))

    @pl.when(parity == 0)
    def _():
        step(aw0_s, aw1_s)

    @pl.when(parity == 1)
    def _():
        step(aw1_s, aw0_s)

    @pl.when(jnp.logical_and(jp == nst - 1, s > 0))
    def _():
        x2 = x1_ref[...] + acc_s[...].T
        ms = jnp.mean(x2 * x2, axis=-1, keepdims=True)
        o_ref[...] = x2 * lax.rsqrt(ms + NORM_EPS) * nw_ref[...]


def _peer(h2t, u_bf16, vt_bf16, r2, e2, n1, c1, x1, final_w, te=2048):
    ntile, _, tm = h2t.shape
    T = ntile * tm
    group = max(PK_NKEYS, PEER_PIECE_ELEMS // tm)
    nst = PK_EXPERTS // te
    tile_ab = lambda s: jnp.minimum(s // nst, ntile - 1)
    tile_c = lambda s: jnp.maximum(s - 1, 0) // nst
    rt = lambda: pl.BlockSpec((1, PK_HEADS, PK_NKEYS, tm), lambda s: (tile_ab(s), 0, 0, 0))
    rows = lambda: pl.BlockSpec((1, PK_HEADS, te // PK_NKEYS, tm), lambda s: (tile_ab(s), 0, s % nst, 0))
    return pl.pallas_call(
        functools.partial(_peer_kernel, te=te, nst=nst),
        grid=(ntile * nst + 1,),
        in_specs=[
            pl.BlockSpec((1, D_MODEL, tm), lambda s: (tile_ab(s), 0, 0)),
            pl.BlockSpec((te, D_MODEL), lambda s: (s % nst, 0)),
            pl.BlockSpec((D_MODEL, te), lambda s: (0, jnp.maximum(s - 1, 0) % nst)),
            rt(), rt(), rows(), rows(),
            pl.BlockSpec((tm, D_MODEL), lambda s: (tile_c(s), 0)),
            pl.BlockSpec((1, D_MODEL), lambda s: (0, 0)),
        ],
        out_specs=pl.BlockSpec((tm, D_MODEL), lambda s: (tile_c(s), 0)),
        out_shape=jax.ShapeDtypeStruct((T, D_MODEL), F32),
        scratch_shapes=[pltpu.VMEM((D_MODEL, tm), F32),
                        pltpu.VMEM((te, tm), BF16), pltpu.VMEM((te, tm), BF16),
                        pltpu.VMEM((group, tm), BF16), pltpu.VMEM((group, tm), BF16)],
        compiler_params=pltpu.CompilerParams(
            dimension_semantics=("arbitrary",), vmem_limit_bytes=VMEM_LIMIT),
        name="peer",
    )(h2t, u_bf16, vt_bf16, r2, e2, n1, c1, x1, final_w)


def _lora_blockdiag(w2, a2):
    z = jnp.zeros_like(w2)
    return jnp.concatenate([jnp.concatenate([w2, z], axis=1), jnp.concatenate([z, a2], axis=1)], axis=0)


def kernel(x, norm_mix_w, w_in, shift_mu, w0, w2, a0, a2, g2, k_k, k_a, r_k, lnx_w, lnx_b, lam_q1, lam_k1, lam_q2, lam_k2, subln_w, proj_a, proj_b, w_out, norm_ffn_w, peer_wq, peer_keys, peer_u, peer_v, final_norm_w):
    B, S, D = x.shape
    T = B * S
    depth = w_in.shape[0]
    assert depth == 1 and D == D_MODEL
    l = 0
    x2 = x.reshape(T, D)
    zs, zqkv, zg = _inproj(x2, norm_mix_w[l][None], w_in[l].astype(BF16))
    ya = _rwkv(zs.reshape(B, S, SHIFT_COLS), shift_mu[l][None], w0[l][None], a0[l][None], k_k[l][None],
               k_a[l][None], r_k[l].reshape(1, RW_WIDTH), lnx_w[l][None], lnx_b[l][None],
               _lora_blockdiag(w2[l], a2[l]).astype(BF16), g2[l].astype(BF16))
    yb = _attn(zqkv.reshape(B, S, QKV_COLS), lam_q1[l][None], lam_k1[l][None], lam_q2[l][None],
               lam_k2[l][None], subln_w[l][:, None])
    x1, h2t, q = _merge(ya.reshape(T, RW_WIDTH), yb.reshape(T, DA_WIDTH), zg, x2,
                       proj_a[l].astype(BF16), proj_b[l].astype(BF16), w_out[l].astype(BF16),
                       norm_ffn_w[l][None], peer_wq[l].astype(BF16))
    r2, e2, n1, c1 = _route(q, peer_keys[l].astype(BF16))
    out = _peer(h2t, peer_u[l].astype(BF16), peer_v[l].T.astype(BF16), r2, e2, n1, c1, x1,
                final_norm_w[None])
    return out.reshape(B, S, D)
```

```python
import functools
import math

import jax
import jax.numpy as jnp
from jax import lax
from jax.experimental import pallas as pl
from jax.experimental.pallas import tpu as pltpu

F32 = jnp.float32
BF16 = jnp.bfloat16

D_MODEL = 1024
RW_HEADS = 8
RW_HEAD = 64
RW_WIDTH = RW_HEADS * RW_HEAD
DECAY_LORA = 64
ICLR_LORA = 64
GATE_LORA = 128
DA_HEADS = 4
DA_HEAD = 64
DA_VDIM = 2 * DA_HEAD
DA_QK_WIDTH = DA_HEADS * 2 * DA_HEAD
DA_WIDTH = DA_HEADS * DA_VDIM
PK_HEADS = 8
PK_NKEYS = 128
PK_QDIM = 256
PK_TOPK = 16
PK_EXPERTS = PK_NKEYS * PK_NKEYS
NORM_EPS = 1e-6
GN_EPS = 64e-5
SUBLN_EPS = 1e-5
SHIFT_COLS = 3 * RW_WIDTH + DECAY_LORA + ICLR_LORA + GATE_LORA
QKV_COLS = 2 * DA_QK_WIDTH + DA_WIDTH
GATE_COLS = 2 * D_MODEL
LAM_INIT = 0.8 - 0.6 * math.exp(0.0)

LANES = 128
BF16_SUBLANES = 16
CHUNK = 64
PAIR = 2 * RW_HEAD
PEER_PIECE_ELEMS = 128 * 1024
PEER_TM = 512
NEG_BIG = -1e30
VMEM_LIMIT = 56 * 1024 * 1024

NT_DIMS = (((1,), (1,)), ((), ()))
TN_DIMS = (((0,), (0,)), ((), ()))


def _dot(a, b):
    return jnp.dot(a, b, preferred_element_type=F32)


def _dot_nt(a, b):
    return lax.dot_general(a, b, NT_DIMS, preferred_element_type=F32)


def _dot_tn(a, b):
    return lax.dot_general(a, b, TN_DIMS, preferred_element_type=F32)


def _sigmoid(x):
    return 1.0 / (1.0 + jnp.exp(-x))


def _split_terms(x, terms):
    parts = []
    rest = x
    for i in range(terms):
        p = rest.astype(BF16)
        parts.append(p)
        if i + 1 < terms:
            rest = rest - p.astype(F32)
    return parts


def _dot_exact_rhs(a_bf16, x, terms=3):
    n = x.shape[1]
    y = _dot(a_bf16, jnp.concatenate(_split_terms(x, terms), axis=1))
    return sum(y[:, i * n:(i + 1) * n] for i in range(terms))


def _dot_exact_lhs(x, b_bf16, terms=2):
    m = x.shape[0]
    y = _dot(jnp.concatenate(_split_terms(x, terms), axis=0), b_bf16)
    return sum(y[i * m:(i + 1) * m] for i in range(terms))


def _inproj_kernel(x_ref, nw_ref, w_ref, zs_ref, zqkv_ref, zg_ref, *, col_chunk):
    x = x_ref[...]
    ms = jnp.mean(x * x, axis=-1, keepdims=True)
    h = (x * lax.rsqrt(ms + NORM_EPS) * nw_ref[...]).astype(BF16)
    c0 = 0
    for out_ref in (zs_ref, zqkv_ref, zg_ref):
        width = out_ref.shape[-1]
        for j in range(0, width, col_chunk):
            z = _dot(h, w_ref[:, c0 + j:c0 + j + col_chunk])
            out_ref[:, j:j + col_chunk] = z.astype(out_ref.dtype)
        c0 += width


def _inproj(x2, norm_w, w_in_bf16, tm=256, col_chunk=256):
    T = x2.shape[0]
    in_cols = w_in_bf16.shape[1]
    return pl.pallas_call(
        functools.partial(_inproj_kernel, col_chunk=col_chunk),
        grid=(T // tm,),
        in_specs=[
            pl.BlockSpec((tm, D_MODEL), lambda i: (i, 0)),
            pl.BlockSpec((1, D_MODEL), lambda i: (0, 0)),
            pl.BlockSpec((D_MODEL, in_cols), lambda i: (0, 0)),
        ],
        out_specs=[
            pl.BlockSpec((tm, SHIFT_COLS), lambda i: (i, 0)),
            pl.BlockSpec((tm, QKV_COLS), lambda i: (i, 0)),
            pl.BlockSpec((tm, GATE_COLS), lambda i: (i, 0)),
        ],
        out_shape=[
            jax.ShapeDtypeStruct((T, SHIFT_COLS), F32),
            jax.ShapeDtypeStruct((T, QKV_COLS), BF16),
            jax.ShapeDtypeStruct((T, GATE_COLS), F32),
        ],
        compiler_params=pltpu.CompilerParams(
            dimension_semantics=("parallel",), vmem_limit_bytes=VMEM_LIMIT),
        name="inproj",
    )(x2, norm_w, w_in_bf16)


def _rwkv_kernel(zs_ref, mu_ref, w0_ref, a0_ref, kk_ref, ka_ref, rk_ref, lnw_ref, lnb_ref,
                 wlora_ref, g2_ref, ones_ref, tri_ref, ya_ref,
                 state_ref, prev_ref, r_s, k_s, v_s, a_s, b_s, ld_s, l_s, y_s, *, ts, group):
    t = pl.program_id(1)

    @pl.when(t == 0)
    def _():
        state_ref[...] = jnp.zeros_like(state_ref)
        prev_ref[...] = jnp.zeros_like(prev_ref)

    z = zs_ref[0]
    row = lax.broadcasted_iota(jnp.int32, (ts, 1), 0)
    zprev = jnp.where(row == 0, prev_ref[...], pltpu.roll(z, 1, axis=0))
    prev_ref[...] = z[ts - 1:ts, :]
    zz = z + (zprev - z) * mu_ref[...]

    W = RW_WIDTH
    r = zz[:, 0:W]
    k = zz[:, W:2 * W]
    v = zz[:, 2 * W:3 * W]
    wa = zz[:, 3 * W:3 * W + LANES]
    gl = zz[:, 3 * W + LANES:3 * W + 2 * LANES]
    lane = lax.broadcasted_iota(jnp.int32, (1, LANES), 1)
    wa_act = jnp.where(lane < DECAY_LORA, jnp.tanh(wa), wa).astype(BF16)
    lora = _dot(wa_act, wlora_ref[...])
    ld = -math.exp(-0.5) * _sigmoid(w0_ref[...] + lora[:, 0:W])
    a = _sigmoid(a0_ref[...] + lora[:, W:2 * W])
    g = _dot(_sigmoid(gl).astype(BF16), g2_ref[...])

    ones_blk = ones_ref[...]
    kk = k * kk_ref[...]
    ss = _dot_exact_lhs(kk * kk, ones_blk)
    kk = kk * lax.rsqrt(jnp.maximum(ss, 1e-24))
    k2 = k * (1.0 + (a - 1.0) * ka_ref[...])
    bonus = _dot_exact_lhs(r * k2 * rk_ref[...], ones_blk) * v

    r_s[...] = r
    k_s[...] = k2
    v_s[...] = v
    a_s[...] = -kk
    b_s[...] = kk * a
    ld_s[...] = ld
    l_s[...] = _dot_exact_rhs(tri_ref[...], ld)

    rowi = lax.broadcasted_iota(jnp.int32, (PAIR, PAIR), 0)
    coli = lax.broadcasted_iota(jnp.int32, (PAIR, PAIR), 1)
    strict_lower = rowi > coli
    lower = rowi >= coli
    eye = rowi == coli
    head0 = lane < RW_HEAD

    def stack(xp):
        return jnp.concatenate([jnp.where(head0, xp, 0.0), jnp.where(head0, 0.0, xp)], axis=0)

    def unstack(xs):
        return xs[0:CHUNK] + xs[CHUNK:2 * CHUNK]

    mid = CHUNK // 2 - 1

    def chunk_terms(r0):
        rows = pl.ds(r0, CHUNK)
        lc = l_s[rows, :]
        cm = l_s[pl.ds(r0 + mid, 1), :]
        lend = l_s[pl.ds(r0 + CHUNK - 1, 1), :]
        e_pos = jnp.exp(lc - cm)
        e_neg = jnp.exp(cm - lc)
        e_cm = jnp.exp(cm)
        e_end = e_neg * jnp.exp(lend - cm)
        r_cen = r_s[rows, :] * e_pos
        a_cen = a_s[rows, :] * jnp.exp(lc - ld_s[rows, :] - cm)
        bc = b_s[rows, :]
        kc = k_s[rows, :]
        return dict(rows=rows, p_end=jnp.exp(lend), v=v_s[rows, :],
                    r_cen=r_cen, r_tru=r_cen * e_cm, a_cen=a_cen, a_tru=a_cen * e_cm,
                    b_cen=bc * e_neg, k_cen=kc * e_neg, b_end=bc * e_end, k_end=kc * e_end)

    def chunk_body(c, carry):
        terms = [chunk_terms(pl.multiple_of((c * group + ci) * CHUNK, CHUNK)) for ci in range(group)]
        items = [(t, slice(p * PAIR, (p + 1) * PAIR)) for t in terms for p in range(RW_HEADS // 2)]
        idx = range(len(items))
        sc = [_dot_nt(jnp.concatenate([stack(t["a_cen"][:, ls]), stack(t["r_cen"][:, ls])], axis=0).astype(BF16),
                      jnp.concatenate([stack(t["b_cen"][:, ls]), stack(t["k_cen"][:, ls])], axis=0).astype(BF16))
              for t, ls in items]
        a_ab = [jnp.where(strict_lower, s_[0:PAIR, 0:PAIR], 0.0) for s_ in sc]
        a_ak = [jnp.where(strict_lower, s_[0:PAIR, PAIR:2 * PAIR], 0.0).astype(BF16) for s_ in sc]
        m_rb = [jnp.where(lower, s_[PAIR:2 * PAIR, 0:PAIR], 0.0).astype(BF16) for s_ in sc]
        m_rk = [jnp.where(lower, s_[PAIR:2 * PAIR, PAIR:2 * PAIR], 0.0).astype(BF16) for s_ in sc]
        v_st = [stack(t["v"][:, ls]).astype(BF16) for t, ls in items]
        x = [jnp.concatenate([stack(t["a_tru"][:, ls]), _dot(a_ak[i], v_st[i])], axis=1)
             for i, (t, ls) in enumerate(items)]
        n = a_ab
        steps = int(math.log2(CHUNK))
        for k in range(steps):
            nb = [n_.astype(BF16) for n_ in n]
            x = [x[i] + _dot(nb[i], x[i].astype(BF16)) for i in idx]
            if k + 1 < steps:
                n = [_dot(nb_, nb_) for nb_ in nb]
        xb = [x_.astype(BF16) for x_ in x]
        ry = [_dot(m_rb[i], xb[i]) for i in idx]
        r_new = [unstack(stack(t["r_tru"][:, ls]) + ry[i][:, 0:PAIR]).astype(BF16)
                 for i, (t, ls) in enumerate(items)]
        y0 = [unstack(ry[i][:, PAIR:2 * PAIR] + _dot(m_rk[i], v_st[i])) for i in idx]
        b_st = [stack(t["b_end"][:, ls]).astype(BF16) for t, ls in items]
        k_st = [stack(t["k_end"][:, ls]).astype(BF16) for t, ls in items]
        gh = [_dot_tn(xb[i], b_st[i]) for i in idx]
        h_t = [gh[i][PAIR:2 * PAIR] + _dot_tn(v_st[i], k_st[i]) for i in idx]
        g_t = [gh[i][0:PAIR].astype(BF16) for i in idx]
        for i, (t, ls) in enumerate(items):
            p = i % (RW_HEADS // 2)
            s_old = state_ref[p]
            sb = s_old.astype(BF16)
            y_s[t["rows"], ls] = _dot_nt(r_new[i], sb) + y0[i]
            state_ref[p] = s_old * t["p_end"][:, ls] + _dot(sb, g_t[i]) + h_t[i]
        return carry

    lax.fori_loop(0, ts // (CHUNK * group), chunk_body, 0)

    y = y_s[...]
    inv_n = 1.0 / RW_HEAD
    mean = _dot_exact_lhs(y, ones_blk) * inv_n
    yc = y - mean
    var = _dot_exact_lhs(yc * yc, ones_blk) * inv_n
    yn = yc * lax.rsqrt(var + GN_EPS) * lnw_ref[...] + lnb_ref[...]
    ya_ref[0] = ((yn + bonus) * g).astype(ya_ref.dtype)


def _rwkv(zs3, mu, w0, a0, k_k, k_a, r_k, lnx_w, lnx_b, wlora, g2, ts=256):
    B, S, _ = zs3.shape
    W = RW_WIDTH
    ones_blk = (jnp.arange(W)[:, None] // RW_HEAD == jnp.arange(W)[None, :] // RW_HEAD).astype(BF16)
    ti = jnp.arange(ts)
    tri = ((ti[:, None] // CHUNK == ti[None, :] // CHUNK) & (ti[:, None] >= ti[None, :])).astype(BF16)
    vec = lambda n: pl.BlockSpec((1, n), lambda b, t: (0, 0))
    full = lambda a: pl.BlockSpec(a.shape, lambda b, t: (0,) * a.ndim)
    return pl.pallas_call(
        functools.partial(_rwkv_kernel, ts=ts, group=2),
        grid=(B, S // ts),
        in_specs=[
            pl.BlockSpec((1, ts, SHIFT_COLS), lambda b, t: (b, t, 0)),
            vec(SHIFT_COLS), vec(W), vec(W), vec(W), vec(W), vec(W), vec(W), vec(W),
            full(wlora), full(g2), full(ones_blk), full(tri),
        ],
        out_specs=pl.BlockSpec((1, ts, W), lambda b, t: (b, t, 0)),
        out_shape=jax.ShapeDtypeStruct((B, S, W), BF16),
        scratch_shapes=[
            pltpu.VMEM((RW_HEADS // 2, PAIR, PAIR), F32),
            pltpu.VMEM((1, SHIFT_COLS), F32),
        ] + [pltpu.VMEM((ts, W), F32) for _ in range(8)],
        compiler_params=pltpu.CompilerParams(
            dimension_semantics=("parallel", "arbitrary"), vmem_limit_bytes=VMEM_LIMIT),
        name="rwkv",
    )(zs3, mu, w0, a0, k_k, k_a, r_k, lnx_w, lnx_b, wlora, g2, ones_blk, tri)


def _attn_kernel(slope_ref, lq1_ref, lk1_ref, lq2_ref, lk2_ref, sw_ref, q_ref, k_ref, v_ref, o_ref,
                 m_s, l_s, acc_s, bias_s, *, tq, tk):
    h = pl.program_id(1)
    qi = pl.program_id(2)
    slope = slope_ref[h]
    lam = (jnp.exp(jnp.sum(lq1_ref[...] * lk1_ref[...], axis=-1, keepdims=True))
           - jnp.exp(jnp.sum(lq2_ref[...] * lk2_ref[...], axis=-1, keepdims=True)) + LAM_INIT)

    lane = lax.broadcasted_iota(jnp.int32, (1, LANES), 1)
    map0 = lane < DA_HEAD
    q = q_ref[0] * (1.0 / math.sqrt(DA_HEAD))
    zero = jnp.zeros_like(q)
    qst = jnp.concatenate([jnp.where(map0, q, zero), jnp.where(map0, zero, q)], axis=0)

    m_s[...] = jnp.full_like(m_s, NEG_BIG)
    l_s[...] = jnp.zeros_like(l_s)
    acc_s[...] = jnp.zeros_like(acc_s)

    @pl.when(qi == 0)
    def _():
        krow = lax.broadcasted_iota(jnp.int32, (tk, 2 * tq), 0)
        qcol = lax.broadcasted_iota(jnp.int32, (tk, 2 * tq), 1)
        qcol = jnp.where(qcol >= tq, qcol - tq, qcol)
        bias_s[...] = -slope * (qcol - krow).astype(F32)

    def block(j, diagonal):
        k0 = pl.multiple_of(j * tk, tk)
        kb = k_ref[0, pl.ds(k0, tk), :]
        vb = v_ref[0, pl.ds(k0, tk), :]
        shift = slope * ((qi - j) * tq).astype(F32)
        b = bias_s[...]
        x = _dot_nt(kb, qst) + b
        if diagonal:
            x = jnp.where(b <= 0.0, x, NEG_BIG)
        m_old = m_s[...]
        m_new = jnp.maximum(m_old, jnp.max(x, axis=0, keepdims=True) - shift)
        alpha = jnp.exp(m_old - m_new)
        p = jnp.exp(x - (m_new + shift))
        l_s[...] = alpha * l_s[...] + jnp.sum(p, axis=0, keepdims=True)
        acc_s[...] = alpha * acc_s[...] + _dot_tn(vb, p.astype(BF16))
        m_s[...] = m_new

    def body(j, carry):
        block(j, False)
        return carry

    lax.fori_loop(0, qi, body, 0)
    block(qi, True)

    o = acc_s[...] / l_s[...]
    o = o[:, 0:tq] - lam * o[:, tq:2 * tq]
    o = o * lax.rsqrt(jnp.mean(o * o, axis=0, keepdims=True) + SUBLN_EPS) * sw_ref[...]
    o_ref[0] = (o * (1.0 - LAM_INIT)).T.astype(o_ref.dtype)


def _attn(zqkv3, lam_q1, lam_k1, lam_q2, lam_k2, subln_w, tq=512, tk=512):
    B, S, _ = zqkv3.shape
    assert tq == tk, "the kernel masks only the diagonal block of aligned square tiles"
    slopes = jnp.asarray([2.0 ** (-8.0 * (i + 1) / DA_HEADS) for i in range(DA_HEADS)], F32)
    nqk = DA_QK_WIDTH // LANES
    vec = lambda n: pl.BlockSpec((1, n), lambda b, h, i: (0, 0))
    return pl.pallas_call(
        functools.partial(_attn_kernel, tq=tq, tk=tk),
        grid=(B, DA_HEADS, S // tq),
        in_specs=[
            pl.BlockSpec(memory_space=pltpu.SMEM),
            vec(DA_HEAD), vec(DA_HEAD), vec(DA_HEAD), vec(DA_HEAD),
            pl.BlockSpec((DA_VDIM, 1), lambda b, h, i: (0, 0)),
            pl.BlockSpec((1, tq, LANES), lambda b, h, i: (b, i, h)),
            pl.BlockSpec((1, S, LANES), lambda b, h, i: (b, 0, nqk + h)),
            pl.BlockSpec((1, S, LANES), lambda b, h, i: (b, 0, 2 * nqk + h)),
        ],
        out_specs=pl.BlockSpec((1, tq, LANES), lambda b, h, i: (b, i, h)),
        out_shape=jax.ShapeDtypeStruct((B, S, DA_WIDTH), BF16),
        scratch_shapes=[
            pltpu.VMEM((1, 2 * tq), F32),
            pltpu.VMEM((1, 2 * tq), F32),
            pltpu.VMEM((DA_VDIM, 2 * tq), F32),
            pltpu.VMEM((tk, 2 * tq), F32),
        ],
        compiler_params=pltpu.CompilerParams(
            dimension_semantics=("parallel", "parallel", "arbitrary"), vmem_limit_bytes=VMEM_LIMIT),
        name="attn",
    )(slopes, lam_q1, lam_k1, lam_q2, lam_k2, subln_w, zqkv3, zqkv3, zqkv3)


def _merge_kernel(ya_ref, yb_ref, zg_ref, x_ref, pa_ref, pb_ref, wo_ref, nw_ref, wq_ref,
                  x1_ref, h2t_ref, q_ref):
    pa = _dot(ya_ref[...], pa_ref[...])
    pb = _dot(yb_ref[...], pb_ref[...])
    ga = zg_ref[:, 0:D_MODEL]
    gb = zg_ref[:, D_MODEL:2 * D_MODEL]
    merged = _sigmoid(ga) * pa + _sigmoid(gb) * pb
    x1 = x_ref[...] + _dot(merged.astype(BF16), wo_ref[...])
    x1_ref[...] = x1
    ms = jnp.mean(x1 * x1, axis=-1, keepdims=True)
    h2 = x1 * lax.rsqrt(ms + NORM_EPS) * nw_ref[...]
    h2t_ref[0] = h2.T.astype(BF16)
    q = _dot(h2.astype(BF16), wq_ref[...]).astype(q_ref.dtype)
    for hd in range(PK_HEADS):
        q_ref[hd] = q[:, hd * PK_QDIM:(hd + 1) * PK_QDIM]


def _merge(ya2, yb2, zg, x2, proj_a, proj_b, w_out, norm_w, wq, tm=PEER_TM):
    T = x2.shape[0]
    row = lambda n: pl.BlockSpec((tm, n), lambda i: (i, 0))
    full = lambda a: pl.BlockSpec(a.shape, lambda i: (0,) * a.ndim)
    return pl.pallas_call(
        _merge_kernel,
        grid=(T // tm,),
        in_specs=[row(RW_WIDTH), row(DA_WIDTH), row(GATE_COLS), row(D_MODEL),
                  full(proj_a), full(proj_b), full(w_out), full(norm_w), full(wq)],
        out_specs=[row(D_MODEL),
                   pl.BlockSpec((1, D_MODEL, tm), lambda i: (i, 0, 0)),
                   pl.BlockSpec((PK_HEADS, tm, PK_QDIM), lambda i: (0, i, 0))],
        out_shape=[
            jax.ShapeDtypeStruct((T, D_MODEL), F32),
            jax.ShapeDtypeStruct((T // tm, D_MODEL, tm), BF16),
            jax.ShapeDtypeStruct((PK_HEADS, T, PK_QDIM), BF16),
        ],
        compiler_params=pltpu.CompilerParams(
            dimension_semantics=("parallel",), vmem_limit_bytes=VMEM_LIMIT),
        name="merge",
    )(ya2, yb2, zg, x2, proj_a, proj_b, w_out, norm_w, wq)


_STAIR = tuple(PK_TOPK // (i + 1) for i in range(PK_TOPK))


def _route_kernel(q_ref, keys_ref, r2_ref, e2_ref, n1_ref, c1_ref, v1_s, v2_s):
    half = PK_QDIM // 2
    q = q_ref[0]
    s1 = _dot_nt(keys_ref[0, 0], q[:, 0:half])
    s2 = _dot_nt(keys_ref[0, 1], q[:, half:2 * half])

    rank2 = jnp.full(s2.shape, float(PK_TOPK), F32)
    w1, w2 = s1, s2
    for i in range(PK_TOPK):
        m1 = jnp.max(w1, axis=0, keepdims=True)
        m2 = jnp.max(w2, axis=0, keepdims=True)
        v1_s[i:i + 1, :] = m1
        v2_s[i:i + 1, :] = m2
        w1 = jnp.where(w1 == m1, -jnp.inf, w1)
        hit2 = w2 == m2
        rank2 = jnp.where(hit2, float(i), rank2)
        w2 = jnp.where(hit2, -jnp.inf, w2)
    v1 = v1_s[...]
    v2 = v2_s[...]

    row8 = lax.broadcasted_iota(jnp.int32, (8, 1), 0)

    def stair(i):
        return jnp.where(row8 < _STAIR[i], v1[i:i + 1] + v2[0:8], -jnp.inf)

    v2_lo = jnp.where(row8 < 4, v2[0:8], pltpu.roll(v2[0:8], 4, axis=0))

    def stair_pair(ia, ib):
        v1_sel = jnp.where(row8 < 4, v1[ia:ia + 1], v1[ib:ib + 1])
        ok = (row8 < _STAIR[ia]) | ((row8 >= 4) & (row8 < 4 + _STAIR[ib]))
        return jnp.where(ok, v1_sel + v2_lo, -jnp.inf)

    cand = jnp.concatenate([v1[0:1] + v2, stair(1), stair(2), stair(3), stair_pair(4, 5), stair_pair(6, 7),
                            v1[8:PK_TOPK] + v2[0:1]], axis=0)
    work = cand
    tau = None
    for i in range(PK_TOPK):
        tau = jnp.max(work, axis=0, keepdims=True)
        work = jnp.where(work == tau, -jnp.inf, work)
    cmax = v1[0:1] + v2[0:1]
    z = jnp.sum(jnp.where(cand >= tau, jnp.exp(cand - cmax), 0.0), axis=0, keepdims=True)
    n1 = jnp.zeros_like(s1)
    for jj in range(PK_TOPK):
        reach = v1 + v2[jj:jj + 1] >= tau
        cut = jnp.min(jnp.where(reach, v1, jnp.inf), axis=0, keepdims=True)
        n1 = jnp.where(s1 >= cut, float(jj + 1), n1)
    r2_ref[0, 0] = rank2.astype(r2_ref.dtype)
    e2_ref[0, 0] = jnp.exp(s2 - v2[0:1]).astype(e2_ref.dtype)
    n1_ref[0, 0] = n1
    c1_ref[0, 0] = jnp.exp(s1 - v1[0:1]) * (0.5 / z)


def _route(q3, keys, tm=PEER_TM):
    T = q3.shape[1]
    blk = lambda: pl.BlockSpec((1, 1, PK_NKEYS, tm), lambda i, h: (i, h, 0, 0))
    shp = lambda dt: jax.ShapeDtypeStruct((T // tm, PK_HEADS, PK_NKEYS, tm), dt)
    return pl.pallas_call(
        _route_kernel,
        grid=(T // tm, PK_HEADS),
        in_specs=[
            pl.BlockSpec((1, tm, PK_QDIM), lambda i, h: (h, i, 0)),
            pl.BlockSpec((1, 2, PK_NKEYS, PK_QDIM // 2), lambda i, h: (h, 0, 0, 0)),
        ],
        out_specs=[blk(), blk(), blk(), blk()],
        out_shape=[shp(BF16), shp(BF16), shp(F32), shp(F32)],
        scratch_shapes=[pltpu.VMEM((PK_TOPK, tm), F32), pltpu.VMEM((PK_TOPK, tm), F32)],
        compiler_params=pltpu.CompilerParams(
            dimension_semantics=("parallel", "arbitrary"), vmem_limit_bytes=VMEM_LIMIT),
        name="route",
    )(q3, keys)


def _peer_kernel(ht_ref, u_ref, vt_ref, r2_ref, e2_ref, n1_ref, c1_ref, x1_ref, nw_ref, o_ref,
                 acc_s, aw0_s, aw1_s, act0_s, act1_s, *, te, nst):
    s = pl.program_id(0)
    jp = lax.rem(s + (nst - 1), jnp.int32(nst))

    @pl.when(s == 0)
    def _():
        acc_s[...] = jnp.zeros_like(acc_s)
        aw1_s[...] = jnp.zeros_like(aw1_s)

    group = act0_s.shape[0]
    npiece = te // group
    e_per_piece = group // PK_NKEYS
    tm = ht_ref.shape[2]
    drain_every = 2
    drows = D_MODEL * drain_every // npiece
    restart = jp == 0
    act_bufs = (act0_s, act1_s)
    tile = (BF16_SUBLANES, tm)
    ntile = PK_NKEYS // BF16_SUBLANES

    def u_proj(i):
        act = _dot(u_ref[i * group:(i + 1) * group, :], ht_ref[0])
        act_bufs[i % 2][...] = act.astype(BF16)

    def step(aw_build, aw_drain):
        u_proj(0)
        for i in range(npiece):
            if i + 1 < npiece:
                u_proj(i + 1)
            for half in range(e_per_piece):
                e = i * e_per_piece + half
                a = act_bufs[i % 2][half * PK_NKEYS:(half + 1) * PK_NKEYS, :]
                gelu2 = a * (1.0 + lax.erf(a * (1.0 / math.sqrt(2.0))))
                w = None
                for hd in range(PK_HEADS):
                    n = jnp.broadcast_to(n1_ref[0, hd, e:e + 1, :], tile).astype(BF16)
                    c = jnp.broadcast_to(c1_ref[0, hd, e:e + 1, :], tile).astype(BF16)
                    r2 = r2_ref[0, hd].reshape(ntile, *tile)
                    e2 = e2_ref[0, hd].reshape(ntile, *tile)
                    wh = jnp.where(r2 < n[None], e2 * c[None], jnp.zeros(r2.shape, BF16))
                    w = wh if w is None else w + wh
                aw_build[e * PK_NKEYS:(e + 1) * PK_NKEYS, :] = gelu2 * w.reshape(a.shape)
            if (i + 1) % drain_every == 0:
                d = (i + 1) // drain_every - 1
                rs = slice(d * drows, (d + 1) * drows)
                part = _dot(vt_ref[rs, :], aw_drain[...])
                acc_s[rs, :] = jnp.where(restart, 0.0, acc_s[rs, :]) + part

    parity = lax.rem(s, jnp.int32(2))

    @pl.when(parity == 0)
    def _():
        step(aw0_s, aw1_s)

    @pl.when(parity == 1)
    def _():
        step(aw1_s, aw0_s)

    @pl.when(jnp.logical_and(jp == nst - 1, s > 0))
    def _():
        x2 = x1_ref[...] + acc_s[...].T
        ms = jnp.mean(x2 * x2, axis=-1, keepdims=True)
        o_ref[...] = x2 * lax.rsqrt(ms + NORM_EPS) * nw_ref[...]


def _peer(h2t, u_bf16, vt_bf16, r2, e2, n1, c1, x1, final_w, te=2048):
    ntile, _, tm = h2t.shape
    T = ntile * tm
    group = max(PK_NKEYS, PEER_PIECE_ELEMS // tm)
    nst = PK_EXPERTS // te
    tile_ab = lambda s: jnp.minimum(s // nst, ntile - 1)
    tile_c = lambda s: jnp.maximum(s - 1, 0) // nst
    rt = lambda: pl.BlockSpec((1, PK_HEADS, PK_NKEYS, tm), lambda s: (tile_ab(s), 0, 0, 0))
    rows = lambda: pl.BlockSpec((1, PK_HEADS, te // PK_NKEYS, tm), lambda s: (tile_ab(s), 0, s % nst, 0))
    return pl.pallas_call(
        functools.partial(_peer_kernel, te=te, nst=nst),
        grid=(ntile * nst + 1,),
        in_specs=[
            pl.BlockSpec((1, D_MODEL, tm), lambda s: (tile_ab(s), 0, 0)),
            pl.BlockSpec((te, D_MODEL), lambda s: (s % nst, 0)),
            pl.BlockSpec((D_MODEL, te), lambda s: (0, jnp.maximum(s - 1, 0) % nst)),
            rt(), rt(), rows(), rows(),
            pl.BlockSpec((tm, D_MODEL), lambda s: (tile_c(s), 0)),
            pl.BlockSpec((1, D_MODEL), lambda s: (0, 0)),
        ],
        out_specs=pl.BlockSpec((tm, D_MODEL), lambda s: (tile_c(s), 0)),
        out_shape=jax.ShapeDtypeStruct((T, D_MODEL), F32),
        scratch_shapes=[pltpu.VMEM((D_MODEL, tm), F32),
                        pltpu.VMEM((te, tm), BF16), pltpu.VMEM((te, tm), BF16),
                        pltpu.VMEM((group, tm), BF16), pltpu.VMEM((group, tm), BF16)],
        compiler_params=pltpu.CompilerParams(
            dimension_semantics=("arbitrary",), vmem_limit_bytes=VMEM_LIMIT),
        name="peer",
    )(h2t, u_bf16, vt_bf16, r2, e2, n1, c1, x1, final_w)


def _lora_blockdiag(w2, a2):
    z = jnp.zeros_like(w2)
    return jnp.concatenate([jnp.concatenate([w2, z], axis=1), jnp.concatenate([z, a2], axis=1)], axis=0)


def kernel(x, norm_mix_w, w_in, shift_mu, w0, w2, a0, a2, g2, k_k, k_a, r_k, lnx_w, lnx_b, lam_q1, lam_k1, lam_q2, lam_k2, subln_w, proj_a, proj_b, w_out, norm_ffn_w, peer_wq, peer_keys, peer_u, peer_v, final_norm_w):
    B, S, D = x.shape
    T = B * S
    depth = w_in.shape[0]
    assert depth == 1 and D == D_MODEL
    l = 0
    x2 = x.reshape(T, D)
    zs, zqkv, zg = _inproj(x2, norm_mix_w[l][None], w_in[l].astype(BF16))
    ya = _rwkv(zs.reshape(B, S, SHIFT_COLS), shift_mu[l][None], w0[l][None], a0[l][None], k_k[l][None],
               k_a[l][None], r_k[l].reshape(1, RW_WIDTH), lnx_w[l][None], lnx_b[l][None],
               _lora_blockdiag(w2[l], a2[l]).astype(BF16), g2[l].astype(BF16))
    yb = _attn(zqkv.reshape(B, S, QKV_COLS), lam_q1[l][None], lam_k1[l][None], lam_q2[l][None],
               lam_k2[l][None], subln_w[l][:, None])
    x1, h2t, q = _merge(ya.reshape(T, RW_WIDTH), yb.reshape(T, DA_WIDTH), zg, x2,
                       proj_a[l].astype(BF16), proj_b[l].astype(BF16), w_out[l].astype(BF16),
                       norm_ffn_w[l][None], peer_wq[l].astype(BF16))
    r2, e2, n1, c1 = _route(q, peer_keys[l].astype(BF16))
    out = _peer(h2t, peer_u[l].astype(BF16), peer_v[l].T.astype(BF16), r2, e2, n1, c1, x1,
                final_norm_w[None])
    return out.reshape(B, S, D)
```

```python
import functools
import math

import jax
import jax.numpy as jnp
from jax import lax
from jax.experimental import pallas as pl
from jax.experimental.pallas import tpu as pltpu

F32 = jnp.float32
BF16 = jnp.bfloat16

D_MODEL = 1024
RW_HEADS = 8
RW_HEAD = 64
RW_WIDTH = RW_HEADS * RW_HEAD
DECAY_LORA = 64
ICLR_LORA = 64
GATE_LORA = 128
DA_HEADS = 4
DA_HEAD = 64
DA_VDIM = 2 * DA_HEAD
DA_QK_WIDTH = DA_HEADS * 2 * DA_HEAD
DA_WIDTH = DA_HEADS * DA_VDIM
PK_HEADS = 8
PK_NKEYS = 128
PK_QDIM = 256
PK_TOPK = 16
PK_EXPERTS = PK_NKEYS * PK_NKEYS
NORM_EPS = 1e-6
GN_EPS = 64e-5
SUBLN_EPS = 1e-5
SHIFT_COLS = 3 * RW_WIDTH + DECAY_LORA + ICLR_LORA + GATE_LORA
QKV_COLS = 2 * DA_QK_WIDTH + DA_WIDTH
GATE_COLS = 2 * D_MODEL
LAM_INIT = 0.8 - 0.6 * math.exp(0.0)

LANES = 128
BF16_SUBLANES = 16
MXU_WIDTH = 256
CHUNK = 64
PAIR = 2 * RW_HEAD
PEER_PIECE_ELEMS = 128 * 1024
PEER_TM = 512
NEG_BIG = -1e30
VMEM_LIMIT = 56 * 1024 * 1024

NT_DIMS = (((1,), (1,)), ((), ()))
TN_DIMS = (((0,), (0,)), ((), ()))


def _dot(a, b):
    return jnp.dot(a, b, preferred_element_type=F32)


def _dot_nt(a, b):
    return lax.dot_general(a, b, NT_DIMS, preferred_element_type=F32)


def _dot_tn(a, b):
    return lax.dot_general(a, b, TN_DIMS, preferred_element_type=F32)


def _sigmoid(x):
    return 1.0 / (1.0 + jnp.exp(-x))


def _split_terms(x, terms):
    parts = []
    rest = x
    for i in range(terms):
        p = rest.astype(BF16)
        parts.append(p)
        if i + 1 < terms:
            rest = rest - p.astype(F32)
    return parts


def _dot_exact_rhs(a_bf16, x, terms=3):
    n = x.shape[1]
    y = _dot(a_bf16, jnp.concatenate(_split_terms(x, terms), axis=1))
    return sum(y[:, i * n:(i + 1) * n] for i in range(terms))


def _dot_exact_lhs(x, b_bf16, terms=2):
    m, width = x.shape
    kb = b_bf16.shape[0]
    lhs = jnp.concatenate(_split_terms(x, terms), axis=0)
    y = jnp.concatenate([_dot(lhs[:, c:c + kb], b_bf16) for c in range(0, width, kb)], axis=1)
    return sum(y[i * m:(i + 1) * m] for i in range(terms))


def _inproj_kernel(x_ref, nw_ref, w_ref, zs_ref, zqkv_ref, zg_ref, *, col_chunk):
    x = x_ref[...]
    ms = jnp.mean(x * x, axis=-1, keepdims=True)
    h = (x * lax.rsqrt(ms + NORM_EPS) * nw_ref[...]).astype(BF16)
    c0 = 0
    for out_ref in (zs_ref, zqkv_ref, zg_ref):
        width = out_ref.shape[-1]
        for j in range(0, width, col_chunk):
            z = _dot(h, w_ref[:, c0 + j:c0 + j + col_chunk])
            out_ref[:, j:j + col_chunk] = z.astype(out_ref.dtype)
        c0 += width


def _inproj(x2, norm_w, w_in_bf16, tm=256, col_chunk=256):
    T = x2.shape[0]
    in_cols = w_in_bf16.shape[1]
    return pl.pallas_call(
        functools.partial(_inproj_kernel, col_chunk=col_chunk),
        grid=(T // tm,),
        in_specs=[
            pl.BlockSpec((tm, D_MODEL), lambda i: (i, 0)),
            pl.BlockSpec((1, D_MODEL), lambda i: (0, 0)),
            pl.BlockSpec((D_MODEL, in_cols), lambda i: (0, 0)),
        ],
        out_specs=[
            pl.BlockSpec((tm, SHIFT_COLS), lambda i: (i, 0)),
            pl.BlockSpec((tm, QKV_COLS), lambda i: (i, 0)),
            pl.BlockSpec((tm, GATE_COLS), lambda i: (i, 0)),
        ],
        out_shape=[
            jax.ShapeDtypeStruct((T, SHIFT_COLS), F32),
            jax.ShapeDtypeStruct((T, QKV_COLS), BF16),
            jax.ShapeDtypeStruct((T, GATE_COLS), F32),
        ],
        compiler_params=pltpu.CompilerParams(
            dimension_semantics=("parallel",), vmem_limit_bytes=VMEM_LIMIT),
        name="inproj",
    )(x2, norm_w, w_in_bf16)


def _rwkv_kernel(zs_ref, mu_ref, w0_ref, a0_ref, kk_ref, ka_ref, rk_ref, lnw_ref, lnb_ref,
                 wlora_ref, g2_ref, ones_ref, tri_ref, ya_ref,
                 state_ref, prev_ref, r_s, k_s, v_s, a_s, b_s, ld_s, l_s, y_s, *, ts, group):
    t = pl.program_id(1)

    @pl.when(t == 0)
    def _():
        state_ref[...] = jnp.zeros_like(state_ref)
        prev_ref[...] = jnp.zeros_like(prev_ref)

    z = zs_ref[0]
    row = lax.broadcasted_iota(jnp.int32, (ts, 1), 0)
    zprev = jnp.where(row == 0, prev_ref[...], pltpu.roll(z, 1, axis=0))
    prev_ref[...] = z[ts - 1:ts, :]
    zz = z + (zprev - z) * mu_ref[...]

    W = RW_WIDTH
    r = zz[:, 0:W]
    k = zz[:, W:2 * W]
    v = zz[:, 2 * W:3 * W]
    wa = zz[:, 3 * W:3 * W + LANES]
    gl = zz[:, 3 * W + LANES:3 * W + 2 * LANES]
    lane = lax.broadcasted_iota(jnp.int32, (1, LANES), 1)
    wa_act = jnp.where(lane < DECAY_LORA, jnp.tanh(wa), wa).astype(BF16)
    lora = _dot(wa_act, wlora_ref[...])
    ld = -math.exp(-0.5) * _sigmoid(w0_ref[...] + lora[:, 0:W])
    a = _sigmoid(a0_ref[...] + lora[:, W:2 * W])
    g = _dot(_sigmoid(gl).astype(BF16), g2_ref[...])

    ones_blk = ones_ref[...]
    kk = k * kk_ref[...]
    ss = _dot_exact_lhs(kk * kk, ones_blk)
    kk = kk * lax.rsqrt(jnp.maximum(ss, 1e-24))
    k2 = k * (1.0 + (a - 1.0) * ka_ref[...])
    bonus = _dot_exact_lhs(r * k2 * rk_ref[...], ones_blk) * v

    r_s[...] = r
    k_s[...] = k2
    v_s[...] = v
    a_s[...] = -kk
    b_s[...] = kk * a
    ld_s[...] = ld
    l_s[...] = _dot_exact_rhs(tri_ref[...], ld)

    rowi = lax.broadcasted_iota(jnp.int32, (PAIR, PAIR), 0)
    coli = lax.broadcasted_iota(jnp.int32, (PAIR, PAIR), 1)
    strict_lower = rowi > coli
    lower = rowi >= coli
    eye = rowi == coli
    head0 = lane < RW_HEAD

    def stack(xp):
        return jnp.concatenate([jnp.where(head0, xp, 0.0), jnp.where(head0, 0.0, xp)], axis=0)

    def unstack(xs):
        return xs[0:CHUNK] + xs[CHUNK:2 * CHUNK]

    mid = CHUNK // 2 - 1

    def chunk_terms(r0):
        rows = pl.ds(r0, CHUNK)
        lc = l_s[rows, :]
        cm = l_s[pl.ds(r0 + mid, 1), :]
        lend = l_s[pl.ds(r0 + CHUNK - 1, 1), :]
        e_pos = jnp.exp(lc - cm)
        e_neg = jnp.exp(cm - lc)
        e_cm = jnp.exp(cm)
        e_end = e_neg * jnp.exp(lend - cm)
        r_cen = r_s[rows, :] * e_pos
        a_cen = a_s[rows, :] * jnp.exp(lc - ld_s[rows, :] - cm)
        bc = b_s[rows, :]
        kc = k_s[rows, :]
        return dict(rows=rows, p_end=jnp.exp(lend), v=v_s[rows, :],
                    r_cen=r_cen, r_tru=r_cen * e_cm, a_cen=a_cen, a_tru=a_cen * e_cm,
                    b_cen=bc * e_neg, k_cen=kc * e_neg, b_end=bc * e_end, k_end=kc * e_end)

    def chunk_body(c, carry):
        terms = [chunk_terms(pl.multiple_of((c * group + ci) * CHUNK, CHUNK)) for ci in range(group)]
        items = [(t, slice(p * PAIR, (p + 1) * PAIR)) for t in terms for p in range(RW_HEADS // 2)]
        idx = range(len(items))
        sc = [_dot_nt(jnp.concatenate([stack(t["a_cen"][:, ls]), stack(t["r_cen"][:, ls])], axis=0).astype(BF16),
                      jnp.concatenate([stack(t["b_cen"][:, ls]), stack(t["k_cen"][:, ls])], axis=0).astype(BF16))
              for t, ls in items]
        a_ab = [jnp.where(strict_lower, s_[0:PAIR, 0:PAIR], 0.0) for s_ in sc]
        a_ak = [jnp.where(strict_lower, s_[0:PAIR, PAIR:2 * PAIR], 0.0).astype(BF16) for s_ in sc]
        m_rb = [jnp.where(lower, s_[PAIR:2 * PAIR, 0:PAIR], 0.0).astype(BF16) for s_ in sc]
        m_rk = [jnp.where(lower, s_[PAIR:2 * PAIR, PAIR:2 * PAIR], 0.0).astype(BF16) for s_ in sc]
        v_st = [stack(t["v"][:, ls]).astype(BF16) for t, ls in items]
        x = [jnp.concatenate([stack(t["a_tru"][:, ls]), _dot(a_ak[i], v_st[i])], axis=1)
             for i, (t, ls) in enumerate(items)]
        n = a_ab
        steps = int(math.log2(CHUNK))
        for k in range(steps):
            nb = [n_.astype(BF16) for n_ in n]
            if k + 1 < steps:
                both = [_dot(nb[i], jnp.concatenate([x[i].astype(BF16), nb[i]], axis=1)) for i in idx]
                x = [x[i] + both[i][:, 0:2 * PAIR] for i in idx]
                n = [b_[:, 2 * PAIR:3 * PAIR] for b_ in both]
            else:
                x = [x[i] + _dot(nb[i], x[i].astype(BF16)) for i in idx]
        xb = [x_.astype(BF16) for x_ in x]
        ry = [_dot(m_rb[i], xb[i]) for i in idx]
        r_new = [unstack(stack(t["r_tru"][:, ls]) + ry[i][:, 0:PAIR]).astype(BF16)
                 for i, (t, ls) in enumerate(items)]
        y0 = [unstack(ry[i][:, PAIR:2 * PAIR] + _dot(m_rk[i], v_st[i])) for i in idx]
        b_st = [stack(t["b_end"][:, ls]).astype(BF16) for t, ls in items]
        k_st = [stack(t["k_end"][:, ls]).astype(BF16) for t, ls in items]
        gh = [_dot_tn(xb[i], b_st[i]) for i in idx]
        h_t = [gh[i][PAIR:2 * PAIR] + _dot_tn(v_st[i], k_st[i]) for i in idx]
        g_t = [gh[i][0:PAIR].astype(BF16) for i in idx]
        for i, (t, ls) in enumerate(items):
            p = i % (RW_HEADS // 2)
            s_old = state_ref[p]
            sb = s_old.astype(BF16)
            y_s[t["rows"], ls] = _dot_nt(r_new[i], sb) + y0[i]
            state_ref[p] = s_old * t["p_end"][:, ls] + _dot(sb, g_t[i]) + h_t[i]
        return carry

    lax.fori_loop(0, ts // (CHUNK * group), chunk_body, 0)

    y = y_s[...]
    inv_n = 1.0 / RW_HEAD
    mean = _dot_exact_lhs(y, ones_blk) * inv_n
    yc = y - mean
    var = _dot_exact_lhs(yc * yc, ones_blk) * inv_n
    yn = yc * lax.rsqrt(var + GN_EPS) * lnw_ref[...] + lnb_ref[...]
    ya_ref[0] = ((yn + bonus) * g).astype(ya_ref.dtype)


def _rwkv(zs3, mu, w0, a0, k_k, k_a, r_k, lnx_w, lnx_b, wlora, g2, ts=256):
    B, S, _ = zs3.shape
    W = RW_WIDTH
    seg = jnp.arange(MXU_WIDTH) // RW_HEAD
    ones_blk = (seg[:, None] == seg[None, :]).astype(BF16)
    ti = jnp.arange(ts)
    tri = ((ti[:, None] // CHUNK == ti[None, :] // CHUNK) & (ti[:, None] >= ti[None, :])).astype(BF16)
    vec = lambda n: pl.BlockSpec((1, n), lambda b, t: (0, 0))
    full = lambda a: pl.BlockSpec(a.shape, lambda b, t: (0,) * a.ndim)
    return pl.pallas_call(
        functools.partial(_rwkv_kernel, ts=ts, group=2),
        grid=(B, S // ts),
        in_specs=[
            pl.BlockSpec((1, ts, SHIFT_COLS), lambda b, t: (b, t, 0)),
            vec(SHIFT_COLS), vec(W), vec(W), vec(W), vec(W), vec(W), vec(W), vec(W),
            full(wlora), full(g2), full(ones_blk), full(tri),
        ],
        out_specs=pl.BlockSpec((1, ts, W), lambda b, t: (b, t, 0)),
        out_shape=jax.ShapeDtypeStruct((B, S, W), BF16),
        scratch_shapes=[
            pltpu.VMEM((RW_HEADS // 2, PAIR, PAIR), F32),
            pltpu.VMEM((1, SHIFT_COLS), F32),
        ] + [pltpu.VMEM((ts, W), F32) for _ in range(8)],
        compiler_params=pltpu.CompilerParams(
            dimension_semantics=("parallel", "arbitrary"), vmem_limit_bytes=VMEM_LIMIT),
        name="rwkv",
    )(zs3, mu, w0, a0, k_k, k_a, r_k, lnx_w, lnx_b, wlora, g2, ones_blk, tri)


def _attn_kernel(slope_ref, lq1_ref, lk1_ref, lq2_ref, lk2_ref, sw_ref, q_ref, k_ref, v_ref, o_ref,
                 m_s, l_s, acc_s, bias_s, *, tq, tk):
    h = pl.program_id(1)
    qi = pl.program_id(2)
    slope = slope_ref[h]
    lam = (jnp.exp(jnp.sum(lq1_ref[...] * lk1_ref[...], axis=-1, keepdims=True))
           - jnp.exp(jnp.sum(lq2_ref[...] * lk2_ref[...], axis=-1, keepdims=True)) + LAM_INIT)

    lane = lax.broadcasted_iota(jnp.int32, (1, LANES), 1)
    map0 = lane < DA_HEAD
    q = q_ref[0] * (1.0 / math.sqrt(DA_HEAD))
    zero = jnp.zeros_like(q)
    qst = jnp.concatenate([jnp.where(map0, q, zero), jnp.where(map0, zero, q)], axis=0)

    m_s[...] = jnp.full_like(m_s, NEG_BIG)
    l_s[...] = jnp.zeros_like(l_s)
    acc_s[...] = jnp.zeros_like(acc_s)

    @pl.when(qi == 0)
    def _():
        krow = lax.broadcasted_iota(jnp.int32, (tk, 2 * tq), 0)
        qcol = lax.broadcasted_iota(jnp.int32, (tk, 2 * tq), 1)
        qcol = jnp.where(qcol >= tq, qcol - tq, qcol)
        bias_s[...] = -slope * (qcol - krow).astype(F32)

    def block(j, diagonal):
        k0 = pl.multiple_of(j * tk, tk)
        kb = k_ref[0, pl.ds(k0, tk), :]
        vb = v_ref[0, pl.ds(k0, tk), :]
        shift = slope * ((qi - j) * tq).astype(F32)
        b = bias_s[...]
        x = _dot_nt(kb, qst) + b
        if diagonal:
            x = jnp.where(b <= 0.0, x, NEG_BIG)
        m_old = m_s[...]
        m_new = jnp.maximum(m_old, jnp.max(x, axis=0, keepdims=True) - shift)
        alpha = jnp.exp(m_old - m_new)
        p = jnp.exp(x - (m_new + shift))
        l_s[...] = alpha * l_s[...] + jnp.sum(p, axis=0, keepdims=True)
        acc_s[...] = alpha * acc_s[...] + _dot_tn(vb, p.astype(BF16))
        m_s[...] = m_new

    def body(j, carry):
        block(j, False)
        return carry

    lax.fori_loop(0, qi, body, 0)
    block(qi, True)

    o = acc_s[...] / l_s[...]
    o = o[:, 0:tq] - lam * o[:, tq:2 * tq]
    o = o * lax.rsqrt(jnp.mean(o * o, axis=0, keepdims=True) + SUBLN_EPS) * sw_ref[...]
    o_ref[0] = (o * (1.0 - LAM_INIT)).T.astype(o_ref.dtype)


def _attn(zqkv3, lam_q1, lam_k1, lam_q2, lam_k2, subln_w, tq=512, tk=512):
    B, S, _ = zqkv3.shape
    assert tq == tk, "the kernel masks only the diagonal block of aligned square tiles"
    slopes = jnp.asarray([2.0 ** (-8.0 * (i + 1) / DA_HEADS) for i in range(DA_HEADS)], F32)
    nqk = DA_QK_WIDTH // LANES
    vec = lambda n: pl.BlockSpec((1, n), lambda b, h, i: (0, 0))
    return pl.pallas_call(
        functools.partial(_attn_kernel, tq=tq, tk=tk),
        grid=(B, DA_HEADS, S // tq),
        in_specs=[
            pl.BlockSpec(memory_space=pltpu.SMEM),
            vec(DA_HEAD), vec(DA_HEAD), vec(DA_HEAD), vec(DA_HEAD),
            pl.BlockSpec((DA_VDIM, 1), lambda b, h, i: (0, 0)),
            pl.BlockSpec((1, tq, LANES), lambda b, h, i: (b, i, h)),
            pl.BlockSpec((1, S, LANES), lambda b, h, i: (b, 0, nqk + h)),
            pl.BlockSpec((1, S, LANES), lambda b, h, i: (b, 0, 2 * nqk + h)),
        ],
        out_specs=pl.BlockSpec((1, tq, LANES), lambda b, h, i: (b, i, h)),
        out_shape=jax.ShapeDtypeStruct((B, S, DA_WIDTH), BF16),
        scratch_shapes=[
            pltpu.VMEM((1, 2 * tq), F32),
            pltpu.VMEM((1, 2 * tq), F32),
            pltpu.VMEM((DA_VDIM, 2 * tq), F32),
            pltpu.VMEM((tk, 2 * tq), F32),
        ],
        compiler_params=pltpu.CompilerParams(
            dimension_semantics=("parallel", "parallel", "arbitrary"), vmem_limit_bytes=VMEM_LIMIT),
        name="attn",
    )(slopes, lam_q1, lam_k1, lam_q2, lam_k2, subln_w, zqkv3, zqkv3, zqkv3)


def _merge_kernel(ya_ref, yb_ref, zg_ref, x_ref, pa_ref, pb_ref, wo_ref, nw_ref, wq_ref,
                  x1_ref, h2t_ref, q_ref):
    pa = _dot(ya_ref[...], pa_ref[...])
    pb = _dot(yb_ref[...], pb_ref[...])
    ga = zg_ref[:, 0:D_MODEL]
    gb = zg_ref[:, D_MODEL:2 * D_MODEL]
    merged = _sigmoid(ga) * pa + _sigmoid(gb) * pb
    x1 = x_ref[...] + _dot(merged.astype(BF16), wo_ref[...])
    x1_ref[...] = x1
    ms = jnp.mean(x1 * x1, axis=-1, keepdims=True)
    h2 = x1 * lax.rsqrt(ms + NORM_EPS) * nw_ref[...]
    h2t_ref[0] = h2.T.astype(BF16)
    q = _dot(h2.astype(BF16), wq_ref[...]).astype(q_ref.dtype)
    for hd in range(PK_HEADS):
        q_ref[hd] = q[:, hd * PK_QDIM:(hd + 1) * PK_QDIM]


def _merge(ya2, yb2, zg, x2, proj_a, proj_b, w_out, norm_w, wq, tm=PEER_TM):
    T = x2.shape[0]
    row = lambda n: pl.BlockSpec((tm, n), lambda i: (i, 0))
    full = lambda a: pl.BlockSpec(a.shape, lambda i: (0,) * a.ndim)
    return pl.pallas_call(
        _merge_kernel,
        grid=(T // tm,),
        in_specs=[row(RW_WIDTH), row(DA_WIDTH), row(GATE_COLS), row(D_MODEL),
                  full(proj_a), full(proj_b), full(w_out), full(norm_w), full(wq)],
        out_specs=[row(D_MODEL),
                   pl.BlockSpec((1, D_MODEL, tm), lambda i: (i, 0, 0)),
                   pl.BlockSpec((PK_HEADS, tm, PK_QDIM), lambda i: (0, i, 0))],
        out_shape=[
            jax.ShapeDtypeStruct((T, D_MODEL), F32),
            jax.ShapeDtypeStruct((T // tm, D_MODEL, tm), BF16),
            jax.ShapeDtypeStruct((PK_HEADS, T, PK_QDIM), BF16),
        ],
        compiler_params=pltpu.CompilerParams(
            dimension_semantics=("parallel",), vmem_limit_bytes=VMEM_LIMIT),
        name="merge",
    )(ya2, yb2, zg, x2, proj_a, proj_b, w_out, norm_w, wq)


_STAIR = tuple(PK_TOPK // (i + 1) for i in range(PK_TOPK))


def _route_kernel(q_ref, keys_ref, r2_ref, e2_ref, n1_ref, c1_ref, v1_s, v2_s):
    half = PK_QDIM // 2
    q = q_ref[0]
    s1 = _dot_nt(keys_ref[0, 0], q[:, 0:half])
    s2 = _dot_nt(keys_ref[0, 1], q[:, half:2 * half])

    rank2 = jnp.full(s2.shape, float(PK_TOPK), F32)
    w1, w2 = s1, s2
    for i in range(PK_TOPK):
        m1 = jnp.max(w1, axis=0, keepdims=True)
        m2 = jnp.max(w2, axis=0, keepdims=True)
        v1_s[i:i + 1, :] = m1
        v2_s[i:i + 1, :] = m2
        w1 = jnp.where(w1 == m1, -jnp.inf, w1)
        hit2 = w2 == m2
        rank2 = jnp.where(hit2, float(i), rank2)
        w2 = jnp.where(hit2, -jnp.inf, w2)
    v1 = v1_s[...]
    v2 = v2_s[...]

    row8 = lax.broadcasted_iota(jnp.int32, (8, 1), 0)

    def stair(i):
        return jnp.where(row8 < _STAIR[i], v1[i:i + 1] + v2[0:8], -jnp.inf)

    v2_lo = jnp.where(row8 < 4, v2[0:8], pltpu.roll(v2[0:8], 4, axis=0))

    def stair_pair(ia, ib):
        v1_sel = jnp.where(row8 < 4, v1[ia:ia + 1], v1[ib:ib + 1])
        ok = (row8 < _STAIR[ia]) | ((row8 >= 4) & (row8 < 4 + _STAIR[ib]))
        return jnp.where(ok, v1_sel + v2_lo, -jnp.inf)

    cand = jnp.concatenate([v1[0:1] + v2, stair(1), stair(2), stair(3), stair_pair(4, 5), stair_pair(6, 7),
                            v1[8:PK_TOPK] + v2[0:1]], axis=0)
    work = cand
    tau = None
    for i in range(PK_TOPK):
        tau = jnp.max(work, axis=0, keepdims=True)
        work = jnp.where(work == tau, -jnp.inf, work)
    cmax = v1[0:1] + v2[0:1]
    z = jnp.sum(jnp.where(cand >= tau, jnp.exp(cand - cmax), 0.0), axis=0, keepdims=True)
    n1 = jnp.zeros_like(s1)
    for jj in range(PK_TOPK):
        reach = v1 + v2[jj:jj + 1] >= tau
        cut = jnp.min(jnp.where(reach, v1, jnp.inf), axis=0, keepdims=True)
        n1 = jnp.where(s1 >= cut, float(jj + 1), n1)
    r2_ref[0, 0] = rank2.astype(r2_ref.dtype)
    e2_ref[0, 0] = jnp.exp(s2 - v2[0:1]).astype(e2_ref.dtype)
    n1_ref[0, 0] = n1
    c1_ref[0, 0] = jnp.exp(s1 - v1[0:1]) * (0.5 / z)


def _route(q3, keys, tm=PEER_TM):
    T = q3.shape[1]
    blk = lambda: pl.BlockSpec((1, 1, PK_NKEYS, tm), lambda i, h: (i, h, 0, 0))
    shp = lambda dt: jax.ShapeDtypeStruct((T // tm, PK_HEADS, PK_NKEYS, tm), dt)
    return pl.pallas_call(
        _route_kernel,
        grid=(T // tm, PK_HEADS),
        in_specs=[
            pl.BlockSpec((1, tm, PK_QDIM), lambda i, h: (h, i, 0)),
            pl.BlockSpec((1, 2, PK_NKEYS, PK_QDIM // 2), lambda i, h: (h, 0, 0, 0)),
        ],
        out_specs=[blk(), blk(), blk(), blk()],
        out_shape=[shp(BF16), shp(BF16), shp(F32), shp(F32)],
        scratch_shapes=[pltpu.VMEM((PK_TOPK, tm), F32), pltpu.VMEM((PK_TOPK, tm), F32)],
        compiler_params=pltpu.CompilerParams(
            dimension_semantics=("parallel", "arbitrary"), vmem_limit_bytes=VMEM_LIMIT),
        name="route",
    )(q3, keys)


def _peer_kernel(ht_ref, u_ref, vt_ref, r2_ref, e2_ref, n1_ref, c1_ref, x1_ref, nw_ref, o_ref,
                 acc_s, aw0_s, aw1_s, act0_s, act1_s, *, te, nst):
    s = pl.program_id(0)
    jp = lax.rem(s + (nst - 1), jnp.int32(nst))

    @pl.when(s == 0)
    def _():
        acc_s[...] = jnp.zeros_like(acc_s)
        aw1_s[...] = jnp.zeros_like(aw1_s)

    group = act0_s.shape[0]
    npiece = te // group
    e_per_piece = group // PK_NKEYS
    tm = ht_ref.shape[2]
    drain_every = 2
    drows = D_MODEL * drain_every // npiece
    restart = jp == 0
    act_bufs = (act0_s, act1_s)
    tile = (BF16_SUBLANES, tm)
    ntile = PK_NKEYS // BF16_SUBLANES

    def u_proj(i):
        act = _dot(u_ref[i * group:(i + 1) * group, :], ht_ref[0])
        act_bufs[i % 2][...] = act.astype(BF16)

    def step(aw_build, aw_drain):
        u_proj(0)
        for i in range(npiece):
            if i + 1 < npiece:
                u_proj(i + 1)
            for half in range(e_per_piece):
                e = i * e_per_piece + half
                a = act_bufs[i % 2][half * PK_NKEYS:(half + 1) * PK_NKEYS, :]
                gelu2 = a * (1.0 + lax.erf(a * (1.0 / math.sqrt(2.0))))
                w = None
                for hd in range(PK_HEADS):
                    n = jnp.broadcast_to(n1_ref[0, hd, e:e + 1, :], tile).astype(BF16)
                    c = jnp.broadcast_to(c1_ref[0, hd, e:e + 1, :], tile).astype(BF16)
                    r2 = r2_ref[0, hd].reshape(ntile, *tile)
                    e2 = e2_ref[0, hd].reshape(ntile, *tile)
                    wh = jnp.where(r2 < n[None], e2 * c[None], jnp.zeros(r2.shape, BF16))
                    w = wh if w is None else w + wh
                aw_build[e * PK_NKEYS:(e + 1) * PK_NKEYS, :] = gelu2 * w.reshape(a.shape)
            if (i + 1) % drain_every == 0:
                d = (i + 1) // drain_every - 1
                rs = slice(d * drows, (d + 1) * drows)
                part = _dot(vt_ref[rs, :], aw_drain[...])
                acc_s[rs, :] = jnp.where(restart, 0.0, acc_s[rs, :]) + part

    parity = lax.rem(s, jnp.int32(2))

    @pl.when(parity == 0)
    def _():
        step(aw0_s, aw1_s)

    @pl.when(parity == 1)
    def _():
        step(aw1_s, aw0_s)

    @pl.when(jnp.logical_and(jp == nst - 1, s > 0))
    def _():
        x2 = x1_ref[...] + acc_s[...].T
        ms = jnp.mean(x2 * x2, axis=-1, keepdims=True)
        o_ref[...] = x2 * lax.rsqrt(ms + NORM_EPS) * nw_ref[...]


def _peer(h2t, u_bf16, vt_bf16, r2, e2, n1, c1, x1, final_w, te=2048):
    ntile, _, tm = h2t.shape
    T = ntile * tm
    group = max(PK_NKEYS, PEER_PIECE_ELEMS // tm)
    nst = PK_EXPERTS // te
    tile_ab = lambda s: jnp.minimum(s // nst, ntile - 1)
    tile_c = lambda s: jnp.maximum(s - 1, 0) // nst
    rt = lambda: pl.BlockSpec((1, PK_HEADS, PK_NKEYS, tm), lambda s: (tile_ab(s), 0, 0, 0))
    rows = lambda: pl.BlockSpec((1, PK_HEADS, te // PK_NKEYS, tm), lambda s: (tile_ab(s), 0, s % nst, 0))
    return pl.pallas_call(
        functools.partial(_peer_kernel, te=te, nst=nst),
        grid=(ntile * nst + 1,),
        in_specs=[
            pl.BlockSpec((1, D_MODEL, tm), lambda s: (tile_ab(s), 0, 0)),
            pl.BlockSpec((te, D_MODEL), lambda s: (s % nst, 0)),
            pl.BlockSpec((D_MODEL, te), lambda s: (0, jnp.maximum(s - 1, 0) % nst)),
            rt(), rt(), rows(), rows(),
            pl.BlockSpec((tm, D_MODEL), lambda s: (tile_c(s), 0)),
            pl.BlockSpec((1, D_MODEL), lambda s: (0, 0)),
        ],
        out_specs=pl.BlockSpec((tm, D_MODEL), lambda s: (tile_c(s), 0)),
        out_shape=jax.ShapeDtypeStruct((T, D_MODEL), F32),
        scratch_shapes=[pltpu.VMEM((D_MODEL, tm), F32),
                        pltpu.VMEM((te, tm), BF16), pltpu.VMEM((te, tm), BF16),
                        pltpu.VMEM((group, tm), BF16), pltpu.VMEM((group, tm), BF16)],
        compiler_params=pltpu.CompilerParams(
            dimension_semantics=("arbitrary",), vmem_limit_bytes=VMEM_LIMIT),
        name="peer",
    )(h2t, u_bf16, vt_bf16, r2, e2, n1, c1, x1, final_w)


def _lora_blockdiag(w2, a2):
    z = jnp.zeros_like(w2)
    return jnp.concatenate([jnp.concatenate([w2, z], axis=1), jnp.concatenate([z, a2], axis=1)], axis=0)


def kernel(x, norm_mix_w, w_in, shift_mu, w0, w2, a0, a2, g2, k_k, k_a, r_k, lnx_w, lnx_b, lam_q1, lam_k1, lam_q2, lam_k2, subln_w, proj_a, proj_b, w_out, norm_ffn_w, peer_wq, peer_keys, peer_u, peer_v, final_norm_w):
    B, S, D = x.shape
    T = B * S
    depth = w_in.shape[0]
    assert depth == 1 and D == D_MODEL
    l = 0
    x2 = x.reshape(T, D)
    zs, zqkv, zg = _inproj(x2, norm_mix_w[l][None], w_in[l].astype(BF16))
    ya = _rwkv(zs.reshape(B, S, SHIFT_COLS), shift_mu[l][None], w0[l][None], a0[l][None], k_k[l][None],
               k_a[l][None], r_k[l].reshape(1, RW_WIDTH), lnx_w[l][None], lnx_b[l][None],
               _lora_blockdiag(w2[l], a2[l]).astype(BF16), g2[l].astype(BF16))
    yb = _attn(zqkv.reshape(B, S, QKV_COLS), lam_q1[l][None], lam_k1[l][None], lam_q2[l][None],
               lam_k2[l][None], subln_w[l][:, None])
    x1, h2t, q = _merge(ya.reshape(T, RW_WIDTH), yb.reshape(T, DA_WIDTH), zg, x2,
                       proj_a[l].astype(BF16), proj_b[l].astype(BF16), w_out[l].astype(BF16),
                       norm_ffn_w[l][None], peer_wq[l].astype(BF16))
    r2, e2, n1, c1 = _route(q, peer_keys[l].astype(BF16))
    out = _peer(h2t, peer_u[l].astype(BF16), peer_v[l].T.astype(BF16), r2, e2, n1, c1, x1,
                final_norm_w[None])
    return out.reshape(B, S, D)
```

```python
import functools
import math

import jax
import jax.numpy as jnp
from jax import lax
from jax.experimental import pallas as pl
from jax.experimental.pallas import tpu as pltpu

F32 = jnp.float32
BF16 = jnp.bfloat16

D_MODEL = 1024
RW_HEADS = 8
RW_HEAD = 64
RW_WIDTH = RW_HEADS * RW_HEAD
DECAY_LORA = 64
ICLR_LORA = 64
GATE_LORA = 128
DA_HEADS = 4
DA_HEAD = 64
DA_VDIM = 2 * DA_HEAD
DA_QK_WIDTH = DA_HEADS * 2 * DA_HEAD
DA_WIDTH = DA_HEADS * DA_VDIM
PK_HEADS = 8
PK_NKEYS = 128
PK_QDIM = 256
PK_TOPK = 16
PK_EXPERTS = PK_NKEYS * PK_NKEYS
NORM_EPS = 1e-6
GN_EPS = 64e-5
SUBLN_EPS = 1e-5
SHIFT_COLS = 3 * RW_WIDTH + DECAY_LORA + ICLR_LORA + GATE_LORA
QKV_COLS = 2 * DA_QK_WIDTH + DA_WIDTH
GATE_COLS = 2 * D_MODEL
LAM_INIT = 0.8 - 0.6 * math.exp(0.0)

LANES = 128
BF16_SUBLANES = 16
MXU_WIDTH = 256
CHUNK = 64
PAIR = 2 * RW_HEAD
PEER_PIECE_ELEMS = 128 * 1024
PEER_TM = 512
NEG_BIG = -1e30
VMEM_LIMIT = 56 * 1024 * 1024

NT_DIMS = (((1,), (1,)), ((), ()))
TN_DIMS = (((0,), (0,)), ((), ()))


def _dot(a, b):
    return jnp.dot(a, b, preferred_element_type=F32)


def _dot_nt(a, b):
    return lax.dot_general(a, b, NT_DIMS, preferred_element_type=F32)


def _dot_tn(a, b):
    return lax.dot_general(a, b, TN_DIMS, preferred_element_type=F32)


def _sigmoid(x):
    return 1.0 / (1.0 + jnp.exp(-x))


def _split_terms(x, terms):
    parts = []
    rest = x
    for i in range(terms):
        p = rest.astype(BF16)
        parts.append(p)
        if i + 1 < terms:
            rest = rest - p.astype(F32)
    return parts


def _dot_exact_rhs(a_bf16, x, terms=3):
    n = x.shape[1]
    y = _dot(a_bf16, jnp.concatenate(_split_terms(x, terms), axis=1))
    return sum(y[:, i * n:(i + 1) * n] for i in range(terms))


def _dot_exact_lhs(x, b_bf16, terms=2):
    m, width = x.shape
    kb = b_bf16.shape[0]
    lhs = jnp.concatenate(_split_terms(x, terms), axis=0)
    y = jnp.concatenate([_dot(lhs[:, c:c + kb], b_bf16) for c in range(0, width, kb)], axis=1)
    return sum(y[i * m:(i + 1) * m] for i in range(terms))


def _inproj_kernel(x_ref, nw_ref, w_ref, zs_ref, zqkv_ref, zg_ref, *, col_chunk):
    x = x_ref[...]
    ms = jnp.mean(x * x, axis=-1, keepdims=True)
    h = (x * lax.rsqrt(ms + NORM_EPS) * nw_ref[...]).astype(BF16)
    c0 = 0
    for out_ref in (zs_ref, zqkv_ref, zg_ref):
        width = out_ref.shape[-1]
        for j in range(0, width, col_chunk):
            z = _dot(h, w_ref[:, c0 + j:c0 + j + col_chunk])
            out_ref[:, j:j + col_chunk] = z.astype(out_ref.dtype)
        c0 += width


def _inproj(x2, norm_w, w_in_bf16, tm=256, col_chunk=256):
    T = x2.shape[0]
    in_cols = w_in_bf16.shape[1]
    return pl.pallas_call(
        functools.partial(_inproj_kernel, col_chunk=col_chunk),
        grid=(T // tm,),
        in_specs=[
            pl.BlockSpec((tm, D_MODEL), lambda i: (i, 0)),
            pl.BlockSpec((1, D_MODEL), lambda i: (0, 0)),
            pl.BlockSpec((D_MODEL, in_cols), lambda i: (0, 0)),
        ],
        out_specs=[
            pl.BlockSpec((tm, SHIFT_COLS), lambda i: (i, 0)),
            pl.BlockSpec((tm, QKV_COLS), lambda i: (i, 0)),
            pl.BlockSpec((tm, GATE_COLS), lambda i: (i, 0)),
        ],
        out_shape=[
            jax.ShapeDtypeStruct((T, SHIFT_COLS), F32),
            jax.ShapeDtypeStruct((T, QKV_COLS), BF16),
            jax.ShapeDtypeStruct((T, GATE_COLS), F32),
        ],
        compiler_params=pltpu.CompilerParams(
            dimension_semantics=("parallel",), vmem_limit_bytes=VMEM_LIMIT),
        name="inproj",
    )(x2, norm_w, w_in_bf16)


def _rwkv_kernel(zs_ref, mu_ref, w0_ref, a0_ref, kk_ref, ka_ref, rk_ref, lnw_ref, lnb_ref,
                 wlora_ref, g2_ref, ones_ref, tri_ref, ya_ref,
                 state_ref, prev_ref, r_s, k_s, v_s, a_s, b_s, ld_s, l_s, y_s, *, ts, group):
    t = pl.program_id(1)

    @pl.when(t == 0)
    def _():
        state_ref[...] = jnp.zeros_like(state_ref)
        prev_ref[...] = jnp.zeros_like(prev_ref)

    z = zs_ref[0]
    row = lax.broadcasted_iota(jnp.int32, (ts, 1), 0)
    zprev = jnp.where(row == 0, prev_ref[...], pltpu.roll(z, 1, axis=0))
    prev_ref[...] = z[ts - 1:ts, :]
    zz = z + (zprev - z) * mu_ref[...]

    W = RW_WIDTH
    r = zz[:, 0:W]
    k = zz[:, W:2 * W]
    v = zz[:, 2 * W:3 * W]
    wa = zz[:, 3 * W:3 * W + LANES]
    gl = zz[:, 3 * W + LANES:3 * W + 2 * LANES]
    lane = lax.broadcasted_iota(jnp.int32, (1, LANES), 1)
    wa_act = jnp.where(lane < DECAY_LORA, jnp.tanh(wa), wa).astype(BF16)
    lora = _dot(wa_act, wlora_ref[...])
    ld = -math.exp(-0.5) * _sigmoid(w0_ref[...] + lora[:, 0:W])
    a = _sigmoid(a0_ref[...] + lora[:, W:2 * W])
    g = _dot(_sigmoid(gl).astype(BF16), g2_ref[...])

    ones_blk = ones_ref[...]
    kk = k * kk_ref[...]
    ss = _dot_exact_lhs(kk * kk, ones_blk)
    kk = kk * lax.rsqrt(jnp.maximum(ss, 1e-24))
    k2 = k * (1.0 + (a - 1.0) * ka_ref[...])
    bonus = _dot_exact_lhs(r * k2 * rk_ref[...], ones_blk) * v

    r_s[...] = r
    k_s[...] = k2
    v_s[...] = v
    a_s[...] = -kk
    b_s[...] = kk * a
    ld_s[...] = ld
    l_s[...] = _dot_exact_rhs(tri_ref[...], ld)

    rowi = lax.broadcasted_iota(jnp.int32, (PAIR, PAIR), 0)
    coli = lax.broadcasted_iota(jnp.int32, (PAIR, PAIR), 1)
    strict_lower = rowi > coli
    lower = rowi >= coli
    eye = rowi == coli
    head0 = lane < RW_HEAD

    def stack(xp):
        return jnp.concatenate([jnp.where(head0, xp, 0.0), jnp.where(head0, 0.0, xp)], axis=0)

    def unstack(xs):
        return xs[0:CHUNK] + xs[CHUNK:2 * CHUNK]

    mid = CHUNK // 2 - 1

    def chunk_terms(r0):
        rows = pl.ds(r0, CHUNK)
        lc = l_s[rows, :]
        cm = l_s[pl.ds(r0 + mid, 1), :]
        lend = l_s[pl.ds(r0 + CHUNK - 1, 1), :]
        e_pos = jnp.exp(lc - cm)
        e_neg = jnp.exp(cm - lc)
        e_cm = jnp.exp(cm)
        e_end = e_neg * jnp.exp(lend - cm)
        r_cen = r_s[rows, :] * e_pos
        a_cen = a_s[rows, :] * jnp.exp(lc - ld_s[rows, :] - cm)
        bc = b_s[rows, :]
        kc = k_s[rows, :]
        return dict(rows=rows, p_end=jnp.exp(lend), v=v_s[rows, :],
                    r_cen=r_cen, r_tru=r_cen * e_cm, a_cen=a_cen, a_tru=a_cen * e_cm,
                    b_cen=bc * e_neg, k_cen=kc * e_neg, b_end=bc * e_end, k_end=kc * e_end)

    def chunk_body(c, carry):
        terms = [chunk_terms(pl.multiple_of((c * group + ci) * CHUNK, CHUNK)) for ci in range(group)]
        items = [(t, slice(p * PAIR, (p + 1) * PAIR)) for t in terms for p in range(RW_HEADS // 2)]
        idx = range(len(items))
        sc = [_dot_nt(jnp.concatenate([stack(t["a_cen"][:, ls]), stack(t["r_cen"][:, ls])], axis=0).astype(BF16),
                      jnp.concatenate([stack(t["b_cen"][:, ls]), stack(t["k_cen"][:, ls])], axis=0).astype(BF16))
              for t, ls in items]
        a_ab = [jnp.where(strict_lower, s_[0:PAIR, 0:PAIR], 0.0) for s_ in sc]
        a_ak = [jnp.where(strict_lower, s_[0:PAIR, PAIR:2 * PAIR], 0.0).astype(BF16) for s_ in sc]
        m_rb = [jnp.where(lower, s_[PAIR:2 * PAIR, 0:PAIR], 0.0).astype(BF16) for s_ in sc]
        m_rk = [jnp.where(lower, s_[PAIR:2 * PAIR, PAIR:2 * PAIR], 0.0).astype(BF16) for s_ in sc]
        v_st = [stack(t["v"][:, ls]).astype(BF16) for t, ls in items]
        x = [jnp.concatenate([stack(t["a_tru"][:, ls]), _dot(a_ak[i], v_st[i])], axis=1)
             for i, (t, ls) in enumerate(items)]
        n = a_ab
        steps = int(math.log2(CHUNK))
        for k in range(steps):
            nb = [n_.astype(BF16) for n_ in n]
            if k + 1 < steps:
                both = [_dot(nb[i], jnp.concatenate([x[i].astype(BF16), nb[i]], axis=1)) for i in idx]
                x = [x[i] + both[i][:, 0:2 * PAIR] for i in idx]
                n = [b_[:, 2 * PAIR:3 * PAIR] for b_ in both]
            else:
                x = [x[i] + _dot(nb[i], x[i].astype(BF16)) for i in idx]
        xb = [x_.astype(BF16) for x_ in x]
        ry = [_dot(m_rb[i], xb[i]) for i in idx]
        r_new = [unstack(stack(t["r_tru"][:, ls]) + ry[i][:, 0:PAIR]).astype(BF16)
                 for i, (t, ls) in enumerate(items)]
        y0 = [unstack(ry[i][:, PAIR:2 * PAIR] + _dot(m_rk[i], v_st[i])) for i in idx]
        b_st = [stack(t["b_end"][:, ls]).astype(BF16) for t, ls in items]
        k_st = [stack(t["k_end"][:, ls]).astype(BF16) for t, ls in items]
        gh = [_dot_tn(xb[i], b_st[i]) for i in idx]
        h_t = [gh[i][PAIR:2 * PAIR] + _dot_tn(v_st[i], k_st[i]) for i in idx]
        g_t = [gh[i][0:PAIR].astype(BF16) for i in idx]
        for i, (t, ls) in enumerate(items):
            p = i % (RW_HEADS // 2)
            s_old = state_ref[p]
            sb = s_old.astype(BF16)
            y_s[t["rows"], ls] = _dot_nt(r_new[i], sb) + y0[i]
            state_ref[p] = s_old * t["p_end"][:, ls] + _dot(sb, g_t[i]) + h_t[i]
        return carry

    lax.fori_loop(0, ts // (CHUNK * group), chunk_body, 0)

    y = y_s[...]
    inv_n = 1.0 / RW_HEAD
    mean = _dot_exact_lhs(y, ones_blk) * inv_n
    yc = y - mean
    var = _dot_exact_lhs(yc * yc, ones_blk) * inv_n
    yn = yc * lax.rsqrt(var + GN_EPS) * lnw_ref[...] + lnb_ref[...]
    ya_ref[0] = ((yn + bonus) * g).astype(ya_ref.dtype)


def _rwkv(zs3, mu, w0, a0, k_k, k_a, r_k, lnx_w, lnx_b, wlora, g2, ts=256):
    B, S, _ = zs3.shape
    W = RW_WIDTH
    seg = jnp.arange(MXU_WIDTH) // RW_HEAD
    ones_blk = (seg[:, None] == seg[None, :]).astype(BF16)
    ti = jnp.arange(ts)
    tri = ((ti[:, None] // CHUNK == ti[None, :] // CHUNK) & (ti[:, None] >= ti[None, :])).astype(BF16)
    vec = lambda n: pl.BlockSpec((1, n), lambda b, t: (0, 0))
    full = lambda a: pl.BlockSpec(a.shape, lambda b, t: (0,) * a.ndim)
    return pl.pallas_call(
        functools.partial(_rwkv_kernel, ts=ts, group=2),
        grid=(B, S // ts),
        in_specs=[
            pl.BlockSpec((1, ts, SHIFT_COLS), lambda b, t: (b, t, 0)),
            vec(SHIFT_COLS), vec(W), vec(W), vec(W), vec(W), vec(W), vec(W), vec(W),
            full(wlora), full(g2), full(ones_blk), full(tri),
        ],
        out_specs=pl.BlockSpec((1, ts, W), lambda b, t: (b, t, 0)),
        out_shape=jax.ShapeDtypeStruct((B, S, W), BF16),
        scratch_shapes=[
            pltpu.VMEM((RW_HEADS // 2, PAIR, PAIR), F32),
            pltpu.VMEM((1, SHIFT_COLS), F32),
        ] + [pltpu.VMEM((ts, W), F32) for _ in range(8)],
        compiler_params=pltpu.CompilerParams(
            dimension_semantics=("parallel", "arbitrary"), vmem_limit_bytes=VMEM_LIMIT),
        name="rwkv",
    )(zs3, mu, w0, a0, k_k, k_a, r_k, lnx_w, lnx_b, wlora, g2, ones_blk, tri)


def _attn_kernel(slope_ref, lq1_ref, lk1_ref, lq2_ref, lk2_ref, sw_ref, q_ref, k_ref, v_ref, o_ref,
                 m_s, l_s, acc_s, bias_s, *, tq, tk):
    h = pl.program_id(1)
    qi = pl.program_id(2)
    slope = slope_ref[h]
    lam = (jnp.exp(jnp.sum(lq1_ref[...] * lk1_ref[...], axis=-1, keepdims=True))
           - jnp.exp(jnp.sum(lq2_ref[...] * lk2_ref[...], axis=-1, keepdims=True)) + LAM_INIT)

    lane = lax.broadcasted_iota(jnp.int32, (1, LANES), 1)
    map0 = lane < DA_HEAD
    q = q_ref[0] * (1.0 / math.sqrt(DA_HEAD))
    zero = jnp.zeros_like(q)
    qst = jnp.concatenate([jnp.where(map0, q, zero), jnp.where(map0, zero, q)], axis=0)

    m_s[...] = jnp.full_like(m_s, NEG_BIG)
    l_s[...] = jnp.zeros_like(l_s)
    acc_s[...] = jnp.zeros_like(acc_s)

    @pl.when(qi == 0)
    def _():
        krow = lax.broadcasted_iota(jnp.int32, (tk, 2 * tq), 0)
        qcol = lax.broadcasted_iota(jnp.int32, (tk, 2 * tq), 1)
        qcol = jnp.where(qcol >= tq, qcol - tq, qcol)
        bias_s[...] = -slope * (qcol - krow).astype(F32)

    def block(j, diagonal):
        k0 = pl.multiple_of(j * tk, tk)
        kb = k_ref[0, pl.ds(k0, tk), :]
        vb = v_ref[0, pl.ds(k0, tk), :]
        shift = slope * ((qi - j) * tq).astype(F32)
        b = bias_s[...]
        x = _dot_nt(kb, qst) + b
        if diagonal:
            x = jnp.where(b <= 0.0, x, NEG_BIG)
        m_old = m_s[...]
        m_new = jnp.maximum(m_old, jnp.max(x, axis=0, keepdims=True) - shift)
        alpha = jnp.exp(m_old - m_new)
        p = jnp.exp(x - (m_new + shift))
        l_s[...] = alpha * l_s[...] + jnp.sum(p, axis=0, keepdims=True)
        acc_s[...] = alpha * acc_s[...] + _dot_tn(vb, p.astype(BF16))
        m_s[...] = m_new

    def body(j, carry):
        block(j, False)
        return carry

    lax.fori_loop(0, qi, body, 0)
    block(qi, True)

    o = acc_s[...] / l_s[...]
    o = o[:, 0:tq] - lam * o[:, tq:2 * tq]
    o = o * lax.rsqrt(jnp.mean(o * o, axis=0, keepdims=True) + SUBLN_EPS) * sw_ref[...]
    o_ref[0] = (o * (1.0 - LAM_INIT)).T.astype(o_ref.dtype)


def _attn(zqkv3, lam_q1, lam_k1, lam_q2, lam_k2, subln_w, tq=512, tk=512):
    B, S, _ = zqkv3.shape
    assert tq == tk, "the kernel masks only the diagonal block of aligned square tiles"
    slopes = jnp.asarray([2.0 ** (-8.0 * (i + 1) / DA_HEADS) for i in range(DA_HEADS)], F32)
    nqk = DA_QK_WIDTH // LANES
    vec = lambda n: pl.BlockSpec((1, n), lambda b, h, i: (0, 0))
    return pl.pallas_call(
        functools.partial(_attn_kernel, tq=tq, tk=tk),
        grid=(B, DA_HEADS, S // tq),
        in_specs=[
            pl.BlockSpec(memory_space=pltpu.SMEM),
            vec(DA_HEAD), vec(DA_HEAD), vec(DA_HEAD), vec(DA_HEAD),
            pl.BlockSpec((DA_VDIM, 1), lambda b, h, i: (0, 0)),
            pl.BlockSpec((1, tq, LANES), lambda b, h, i: (b, i, h)),
            pl.BlockSpec((1, S, LANES), lambda b, h, i: (b, 0, nqk + h)),
            pl.BlockSpec((1, S, LANES), lambda b, h, i: (b, 0, 2 * nqk + h)),
        ],
        out_specs=pl.BlockSpec((1, tq, LANES), lambda b, h, i: (b, i, h)),
        out_shape=jax.ShapeDtypeStruct((B, S, DA_WIDTH), BF16),
        scratch_shapes=[
            pltpu.VMEM((1, 2 * tq), F32),
            pltpu.VMEM((1, 2 * tq), F32),
            pltpu.VMEM((DA_VDIM, 2 * tq), F32),
            pltpu.VMEM((tk, 2 * tq), F32),
        ],
        compiler_params=pltpu.CompilerParams(
            dimension_semantics=("parallel", "parallel", "arbitrary"), vmem_limit_bytes=VMEM_LIMIT),
        name="attn",
    )(slopes, lam_q1, lam_k1, lam_q2, lam_k2, subln_w, zqkv3, zqkv3, zqkv3)


def _merge_kernel(ya_ref, yb_ref, zg_ref, x_ref, pa_ref, pb_ref, wo_ref, nw_ref, wq_ref,
                  x1_ref, h2t_ref, q_ref):
    pa = _dot(ya_ref[...], pa_ref[...])
    pb = _dot(yb_ref[...], pb_ref[...])
    ga = zg_ref[:, 0:D_MODEL]
    gb = zg_ref[:, D_MODEL:2 * D_MODEL]
    merged = _sigmoid(ga) * pa + _sigmoid(gb) * pb
    x1 = x_ref[...] + _dot(merged.astype(BF16), wo_ref[...])
    x1_ref[...] = x1
    ms = jnp.mean(x1 * x1, axis=-1, keepdims=True)
    h2 = x1 * lax.rsqrt(ms + NORM_EPS) * nw_ref[...]
    h2t_ref[0] = h2.T.astype(BF16)
    q = _dot(h2.astype(BF16), wq_ref[...]).astype(q_ref.dtype)
    for hd in range(PK_HEADS):
        q_ref[hd] = q[:, hd * PK_QDIM:(hd + 1) * PK_QDIM]


def _merge(ya2, yb2, zg, x2, proj_a, proj_b, w_out, norm_w, wq, tm=PEER_TM):
    T = x2.shape[0]
    row = lambda n: pl.BlockSpec((tm, n), lambda i: (i, 0))
    full = lambda a: pl.BlockSpec(a.shape, lambda i: (0,) * a.ndim)
    return pl.pallas_call(
        _merge_kernel,
        grid=(T // tm,),
        in_specs=[row(RW_WIDTH), row(DA_WIDTH), row(GATE_COLS), row(D_MODEL),
                  full(proj_a), full(proj_b), full(w_out), full(norm_w), full(wq)],
        out_specs=[row(D_MODEL),
                   pl.BlockSpec((1, D_MODEL, tm), lambda i: (i, 0, 0)),
                   pl.BlockSpec((PK_HEADS, tm, PK_QDIM), lambda i: (0, i, 0))],
        out_shape=[
            jax.ShapeDtypeStruct((T, D_MODEL), F32),
            jax.ShapeDtypeStruct((T // tm, D_MODEL, tm), BF16),
            jax.ShapeDtypeStruct((PK_HEADS, T, PK_QDIM), BF16),
        ],
        compiler_params=pltpu.CompilerParams(
            dimension_semantics=("parallel",), vmem_limit_bytes=VMEM_LIMIT),
        name="merge",
    )(ya2, yb2, zg, x2, proj_a, proj_b, w_out, norm_w, wq)


_STAIR = tuple(PK_TOPK // (i + 1) for i in range(PK_TOPK))


def _route_kernel(q_ref, keys_ref, r2_ref, e2_ref, n1_ref, c1_ref, v1_s, v2_s):
    half = PK_QDIM // 2
    heads = range(q_ref.shape[0])
    s1 = [_dot_nt(keys_ref[h, 0], q_ref[h][:, 0:half]) for h in heads]
    s2 = [_dot_nt(keys_ref[h, 1], q_ref[h][:, half:2 * half]) for h in heads]

    rank2 = [jnp.full(x.shape, float(PK_TOPK), F32) for x in s2]
    w1, w2 = list(s1), list(s2)
    for i in range(PK_TOPK):
        m1 = [jnp.max(x, axis=0, keepdims=True) for x in w1]
        m2 = [jnp.max(x, axis=0, keepdims=True) for x in w2]
        for h in heads:
            v1_s[h, i:i + 1, :] = m1[h]
            v2_s[h, i:i + 1, :] = m2[h]
        w1 = [jnp.where(x == m, -jnp.inf, x) for x, m in zip(w1, m1)]
        hit2 = [x == m for x, m in zip(w2, m2)]
        rank2 = [jnp.where(hit, float(i), r) for hit, r in zip(hit2, rank2)]
        w2 = [jnp.where(hit, -jnp.inf, x) for hit, x in zip(hit2, w2)]

    row8 = lax.broadcasted_iota(jnp.int32, (8, 1), 0)
    for h in heads:
        v1 = v1_s[h]
        v2 = v2_s[h]

        def stair(i):
            return jnp.where(row8 < _STAIR[i], v1[i:i + 1] + v2[0:8], -jnp.inf)

        v2_lo = jnp.where(row8 < 4, v2[0:8], pltpu.roll(v2[0:8], 4, axis=0))

        def stair_pair(ia, ib):
            v1_sel = jnp.where(row8 < 4, v1[ia:ia + 1], v1[ib:ib + 1])
            ok = (row8 < _STAIR[ia]) | ((row8 >= 4) & (row8 < 4 + _STAIR[ib]))
            return jnp.where(ok, v1_sel + v2_lo, -jnp.inf)

        cand = jnp.concatenate([v1[0:1] + v2, stair(1), stair(2), stair(3), stair_pair(4, 5), stair_pair(6, 7),
                                v1[8:PK_TOPK] + v2[0:1]], axis=0)
        work = cand
        tau = None
        for i in range(PK_TOPK):
            tau = jnp.max(work, axis=0, keepdims=True)
            work = jnp.where(work == tau, -jnp.inf, work)
        cmax = v1[0:1] + v2[0:1]
        z = jnp.sum(jnp.where(cand >= tau, jnp.exp(cand - cmax), 0.0), axis=0, keepdims=True)
        n1 = jnp.zeros_like(s1[h])
        for jj in range(PK_TOPK):
            reach = v1 + v2[jj:jj + 1] >= tau
            cut = jnp.min(jnp.where(reach, v1, jnp.inf), axis=0, keepdims=True)
            n1 = jnp.where(s1[h] >= cut, float(jj + 1), n1)
        r2_ref[0, h] = rank2[h].astype(r2_ref.dtype)
        e2_ref[0, h] = jnp.exp(s2[h] - v2[0:1]).astype(e2_ref.dtype)
        n1_ref[0, h] = n1
        c1_ref[0, h] = jnp.exp(s1[h] - v1[0:1]) * (0.5 / z)


def _route(q3, keys, tm=PEER_TM, nh=4):
    T = q3.shape[1]
    blk = lambda: pl.BlockSpec((1, nh, PK_NKEYS, tm), lambda i, h: (i, h, 0, 0))
    shp = lambda dt: jax.ShapeDtypeStruct((T // tm, PK_HEADS, PK_NKEYS, tm), dt)
    return pl.pallas_call(
        _route_kernel,
        grid=(T // tm, PK_HEADS // nh),
        in_specs=[
            pl.BlockSpec((nh, tm, PK_QDIM), lambda i, h: (h, i, 0)),
            pl.BlockSpec((nh, 2, PK_NKEYS, PK_QDIM // 2), lambda i, h: (h, 0, 0, 0)),
        ],
        out_specs=[blk(), blk(), blk(), blk()],
        out_shape=[shp(BF16), shp(BF16), shp(F32), shp(F32)],
        scratch_shapes=[pltpu.VMEM((nh, PK_TOPK, tm), F32), pltpu.VMEM((nh, PK_TOPK, tm), F32)],
        compiler_params=pltpu.CompilerParams(
            dimension_semantics=("parallel", "arbitrary"), vmem_limit_bytes=VMEM_LIMIT),
        name="route",
    )(q3, keys)


def _peer_kernel(ht_ref, u_ref, vt_ref, r2_ref, e2_ref, n1_ref, c1_ref, x1_ref, nw_ref, o_ref,
                 acc_s, aw0_s, aw1_s, act0_s, act1_s, *, te, nst):
    s = pl.program_id(0)
    jp = lax.rem(s + (nst - 1), jnp.int32(nst))

    @pl.when(s == 0)
    def _():
        acc_s[...] = jnp.zeros_like(acc_s)
        aw1_s[...] = jnp.zeros_like(aw1_s)

    group = act0_s.shape[0]
    npiece = te // group
    e_per_piece = group // PK_NKEYS
    tm = ht_ref.shape[2]
    drain_every = 2
    drows = D_MODEL * drain_every // npiece
    restart = jp == 0
    act_bufs = (act0_s, act1_s)
    tile = (BF16_SUBLANES, tm)
    ntile = PK_NKEYS // BF16_SUBLANES

    def u_proj(i):
        act = _dot(u_ref[i * group:(i + 1) * group, :], ht_ref[0])
        act_bufs[i % 2][...] = act.astype(BF16)

    def step(aw_build, aw_drain):
        u_proj(0)
        for i in range(npiece):
            if i + 1 < npiece:
                u_proj(i + 1)
            for half in range(e_per_piece):
                e = i * e_per_piece + half
                a = act_bufs[i % 2][half * PK_NKEYS:(half + 1) * PK_NKEYS, :]
                gelu2 = a * (1.0 + lax.erf(a * (1.0 / math.sqrt(2.0))))
                w = None
                for hd in range(PK_HEADS):
                    n = jnp.broadcast_to(n1_ref[0, hd, e:e + 1, :], tile).astype(BF16)
                    c = jnp.broadcast_to(c1_ref[0, hd, e:e + 1, :], tile).astype(BF16)
                    r2 = r2_ref[0, hd].reshape(ntile, *tile)
                    e2 = e2_ref[0, hd].reshape(ntile, *tile)
                    wh = jnp.where(r2 < n[None], e2 * c[None], jnp.zeros(r2.shape, BF16))
                    w = wh if w is None else w + wh
                aw_build[e * PK_NKEYS:(e + 1) * PK_NKEYS, :] = gelu2 * w.reshape(a.shape)
            if (i + 1) % drain_every == 0:
                d = (i + 1) // drain_every - 1
                rs = slice(d * drows, (d + 1) * drows)
                part = _dot(vt_ref[rs, :], aw_drain[...])
                acc_s[rs, :] = jnp.where(restart, 0.0, acc_s[rs, :]) + part

    parity = lax.rem(s, jnp.int32(2))

    @pl.when(parity == 0)
    def _():
        step(aw0_s, aw1_s)

    @pl.when(parity == 1)
    def _():
        step(aw1_s, aw0_s)

    @pl.when(jnp.logical_and(jp == nst - 1, s > 0))
    def _():
        x2 = x1_ref[...] + acc_s[...].T
        ms = jnp.mean(x2 * x2, axis=-1, keepdims=True)
        o_ref[...] = x2 * lax.rsqrt(ms + NORM_EPS) * nw_ref[...]


def _peer(h2t, u_bf16, vt_bf16, r2, e2, n1, c1, x1, final_w, te=2048):
    ntile, _, tm = h2t.shape
    T = ntile * tm
    group = max(PK_NKEYS, PEER_PIECE_ELEMS // tm)
    nst = PK_EXPERTS // te
    tile_ab = lambda s: jnp.minimum(s // nst, ntile - 1)
    tile_c = lambda s: jnp.maximum(s - 1, 0) // nst
    rt = lambda: pl.BlockSpec((1, PK_HEADS, PK_NKEYS, tm), lambda s: (tile_ab(s), 0, 0, 0))
    rows = lambda: pl.BlockSpec((1, PK_HEADS, te // PK_NKEYS, tm), lambda s: (tile_ab(s), 0, s % nst, 0))
    return pl.pallas_call(
        functools.partial(_peer_kernel, te=te, nst=nst),
        grid=(ntile * nst + 1,),
        in_specs=[
            pl.BlockSpec((1, D_MODEL, tm), lambda s: (tile_ab(s), 0, 0)),
            pl.BlockSpec((te, D_MODEL), lambda s: (s % nst, 0)),
            pl.BlockSpec((D_MODEL, te), lambda s: (0, jnp.maximum(s - 1, 0) % nst)),
            rt(), rt(), rows(), rows(),
            pl.BlockSpec((tm, D_MODEL), lambda s: (tile_c(s), 0)),
            pl.BlockSpec((1, D_MODEL), lambda s: (0, 0)),
        ],
        out_specs=pl.BlockSpec((tm, D_MODEL), lambda s: (tile_c(s), 0)),
        out_shape=jax.ShapeDtypeStruct((T, D_MODEL), F32),
        scratch_shapes=[pltpu.VMEM((D_MODEL, tm), F32),
                        pltpu.VMEM((te, tm), BF16), pltpu.VMEM((te, tm), BF16),
                        pltpu.VMEM((group, tm), BF16), pltpu.VMEM((group, tm), BF16)],
        compiler_params=pltpu.CompilerParams(
            dimension_semantics=("arbitrary",), vmem_limit_bytes=VMEM_LIMIT),
        name="peer",
    )(h2t, u_bf16, vt_bf16, r2, e2, n1, c1, x1, final_w)


def _lora_blockdiag(w2, a2):
    z = jnp.zeros_like(w2)
    return jnp.concatenate([jnp.concatenate([w2, z], axis=1), jnp.concatenate([z, a2], axis=1)], axis=0)


def kernel(x, norm_mix_w, w_in, shift_mu, w0, w2, a0, a2, g2, k_k, k_a, r_k, lnx_w, lnx_b, lam_q1, lam_k1, lam_q2, lam_k2, subln_w, proj_a, proj_b, w_out, norm_ffn_w, peer_wq, peer_keys, peer_u, peer_v, final_norm_w):
    B, S, D = x.shape
    T = B * S
    depth = w_in.shape[0]
    assert depth == 1 and D == D_MODEL
    l = 0
    x2 = x.reshape(T, D)
    zs, zqkv, zg = _inproj(x2, norm_mix_w[l][None], w_in[l].astype(BF16))
    ya = _rwkv(zs.reshape(B, S, SHIFT_COLS), shift_mu[l][None], w0[l][None], a0[l][None], k_k[l][None],
               k_a[l][None], r_k[l].reshape(1, RW_WIDTH), lnx_w[l][None], lnx_b[l][None],
               _lora_blockdiag(w2[l], a2[l]).astype(BF16), g2[l].astype(BF16))
    yb = _attn(zqkv.reshape(B, S, QKV_COLS), lam_q1[l][None], lam_k1[l][None], lam_q2[l][None],
               lam_k2[l][None], subln_w[l][:, None])
    x1, h2t, q = _merge(ya.reshape(T, RW_WIDTH), yb.reshape(T, DA_WIDTH), zg, x2,
                       proj_a[l].astype(BF16), proj_b[l].astype(BF16), w_out[l].astype(BF16),
                       norm_ffn_w[l][None], peer_wq[l].astype(BF16))
    r2, e2, n1, c1 = _route(q, peer_keys[l].astype(BF16))
    out = _peer(h2t, peer_u[l].astype(BF16), peer_v[l].T.astype(BF16), r2, e2, n1, c1, x1,
                final_norm_w[None])
    return out.reshape(B, S, D)
```

```python
import functools
import math

import jax
import jax.numpy as jnp
from jax import lax
from jax.experimental import pallas as pl
from jax.experimental.pallas import tpu as pltpu

F32 = jnp.float32
BF16 = jnp.bfloat16

D_MODEL = 1024
RW_HEADS = 8
RW_HEAD = 64
RW_WIDTH = RW_HEADS * RW_HEAD
DECAY_LORA = 64
ICLR_LORA = 64
GATE_LORA = 128
DA_HEADS = 4
DA_HEAD = 64
DA_VDIM = 2 * DA_HEAD
DA_QK_WIDTH = DA_HEADS * 2 * DA_HEAD
DA_WIDTH = DA_HEADS * DA_VDIM
PK_HEADS = 8
PK_NKEYS = 128
PK_QDIM = 256
PK_TOPK = 16
PK_EXPERTS = PK_NKEYS * PK_NKEYS
NORM_EPS = 1e-6
GN_EPS = 64e-5
SUBLN_EPS = 1e-5
SHIFT_COLS = 3 * RW_WIDTH + DECAY_LORA + ICLR_LORA + GATE_LORA
QKV_COLS = 2 * DA_QK_WIDTH + DA_WIDTH
GATE_COLS = 2 * D_MODEL
LAM_INIT = 0.8 - 0.6 * math.exp(0.0)

LANES = 128
BF16_SUBLANES = 16
MXU_WIDTH = 256
CHUNK = 64
PAIR = 2 * RW_HEAD
PEER_PIECE_ELEMS = 128 * 1024
PEER_TM = 512
NEG_BIG = -1e30
VMEM_LIMIT = 56 * 1024 * 1024

NT_DIMS = (((1,), (1,)), ((), ()))
TN_DIMS = (((0,), (0,)), ((), ()))


def _dot(a, b):
    return jnp.dot(a, b, preferred_element_type=F32)


def _dot_nt(a, b):
    return lax.dot_general(a, b, NT_DIMS, preferred_element_type=F32)


def _dot_tn(a, b):
    return lax.dot_general(a, b, TN_DIMS, preferred_element_type=F32)


def _sigmoid(x):
    return 1.0 / (1.0 + jnp.exp(-x))


def _split_terms(x, terms):
    parts = []
    rest = x
    for i in range(terms):
        p = rest.astype(BF16)
        parts.append(p)
        if i + 1 < terms:
            rest = rest - p.astype(F32)
    return parts


def _dot_exact_rhs(a_bf16, x, terms=3):
    n = x.shape[1]
    y = _dot(a_bf16, jnp.concatenate(_split_terms(x, terms), axis=1))
    return sum(y[:, i * n:(i + 1) * n] for i in range(terms))


def _dot_exact_lhs(x, b_bf16, terms=2):
    m, width = x.shape
    kb = b_bf16.shape[0]
    lhs = jnp.concatenate(_split_terms(x, terms), axis=0)
    y = jnp.concatenate([_dot(lhs[:, c:c + kb], b_bf16) for c in range(0, width, kb)], axis=1)
    return sum(y[i * m:(i + 1) * m] for i in range(terms))


def _inproj_kernel(x_ref, nw_ref, w_ref, zs_ref, zqkv_ref, zg_ref, *, col_chunk):
    x = x_ref[...]
    ms = jnp.mean(x * x, axis=-1, keepdims=True)
    h = (x * lax.rsqrt(ms + NORM_EPS) * nw_ref[...]).astype(BF16)
    c0 = 0
    for out_ref in (zs_ref, zqkv_ref, zg_ref):
        width = out_ref.shape[-1]
        for j in range(0, width, col_chunk):
            z = _dot(h, w_ref[:, c0 + j:c0 + j + col_chunk])
            out_ref[:, j:j + col_chunk] = z.astype(out_ref.dtype)
        c0 += width


def _inproj(x2, norm_w, w_in_bf16, tm=512, col_chunk=256):
    T = x2.shape[0]
    in_cols = w_in_bf16.shape[1]
    return pl.pallas_call(
        functools.partial(_inproj_kernel, col_chunk=col_chunk),
        grid=(T // tm,),
        in_specs=[
            pl.BlockSpec((tm, D_MODEL), lambda i: (i, 0)),
            pl.BlockSpec((1, D_MODEL), lambda i: (0, 0)),
            pl.BlockSpec((D_MODEL, in_cols), lambda i: (0, 0)),
        ],
        out_specs=[
            pl.BlockSpec((tm, SHIFT_COLS), lambda i: (i, 0)),
            pl.BlockSpec((tm, QKV_COLS), lambda i: (i, 0)),
            pl.BlockSpec((tm, GATE_COLS), lambda i: (i, 0)),
        ],
        out_shape=[
            jax.ShapeDtypeStruct((T, SHIFT_COLS), F32),
            jax.ShapeDtypeStruct((T, QKV_COLS), BF16),
            jax.ShapeDtypeStruct((T, GATE_COLS), F32),
        ],
        compiler_params=pltpu.CompilerParams(
            dimension_semantics=("parallel",), vmem_limit_bytes=VMEM_LIMIT),
        name="inproj",
    )(x2, norm_w, w_in_bf16)


def _rwkv_kernel(zs_ref, mu_ref, w0_ref, a0_ref, kk_ref, ka_ref, rk_ref, lnw_ref, lnb_ref,
                 wlora_ref, g2_ref, ones_ref, tri_ref, ya_ref,
                 state_ref, prev_ref, r_s, k_s, v_s, a_s, b_s, ld_s, l_s, y_s, *, ts, group):
    t = pl.program_id(1)

    @pl.when(t == 0)
    def _():
        state_ref[...] = jnp.zeros_like(state_ref)
        prev_ref[...] = jnp.zeros_like(prev_ref)

    z = zs_ref[0]
    row = lax.broadcasted_iota(jnp.int32, (ts, 1), 0)
    zprev = jnp.where(row == 0, prev_ref[...], pltpu.roll(z, 1, axis=0))
    prev_ref[...] = z[ts - 1:ts, :]
    zz = z + (zprev - z) * mu_ref[...]

    W = RW_WIDTH
    r = zz[:, 0:W]
    k = zz[:, W:2 * W]
    v = zz[:, 2 * W:3 * W]
    wa = zz[:, 3 * W:3 * W + LANES]
    gl = zz[:, 3 * W + LANES:3 * W + 2 * LANES]
    lane = lax.broadcasted_iota(jnp.int32, (1, LANES), 1)
    wa_act = jnp.where(lane < DECAY_LORA, jnp.tanh(wa), wa).astype(BF16)
    lora = _dot(wa_act, wlora_ref[...])
    ld = -math.exp(-0.5) * _sigmoid(w0_ref[...] + lora[:, 0:W])
    a = _sigmoid(a0_ref[...] + lora[:, W:2 * W])
    g = _dot(_sigmoid(gl).astype(BF16), g2_ref[...])

    ones_blk = ones_ref[...]
    kk = k * kk_ref[...]
    ss = _dot_exact_lhs(kk * kk, ones_blk)
    kk = kk * lax.rsqrt(jnp.maximum(ss, 1e-24))
    k2 = k * (1.0 + (a - 1.0) * ka_ref[...])
    bonus = _dot_exact_lhs(r * k2 * rk_ref[...], ones_blk) * v

    r_s[...] = r
    k_s[...] = k2
    v_s[...] = v
    a_s[...] = -kk
    b_s[...] = kk * a
    ld_s[...] = ld
    l_s[...] = _dot_exact_rhs(tri_ref[...], ld)

    rowi = lax.broadcasted_iota(jnp.int32, (PAIR, PAIR), 0)
    coli = lax.broadcasted_iota(jnp.int32, (PAIR, PAIR), 1)
    strict_lower = rowi > coli
    lower = rowi >= coli
    eye = rowi == coli
    head0 = lane < RW_HEAD

    def stack(xp):
        return jnp.concatenate([jnp.where(head0, xp, 0.0), jnp.where(head0, 0.0, xp)], axis=0)

    def unstack(xs):
        return xs[0:CHUNK] + xs[CHUNK:2 * CHUNK]

    mid = CHUNK // 2 - 1

    def chunk_terms(r0):
        rows = pl.ds(r0, CHUNK)
        lc = l_s[rows, :]
        cm = l_s[pl.ds(r0 + mid, 1), :]
        lend = l_s[pl.ds(r0 + CHUNK - 1, 1), :]
        e_pos = jnp.exp(lc - cm)
        e_neg = jnp.exp(cm - lc)
        e_cm = jnp.exp(cm)
        e_end = e_neg * jnp.exp(lend - cm)
        r_cen = r_s[rows, :] * e_pos
        a_cen = a_s[rows, :] * jnp.exp(lc - ld_s[rows, :] - cm)
        bc = b_s[rows, :]
        kc = k_s[rows, :]
        return dict(rows=rows, p_end=jnp.exp(lend), v=v_s[rows, :],
                    r_cen=r_cen, r_tru=r_cen * e_cm, a_cen=a_cen, a_tru=a_cen * e_cm,
                    b_cen=bc * e_neg, k_cen=kc * e_neg, b_end=bc * e_end, k_end=kc * e_end)

    def chunk_body(c, carry):
        terms = [chunk_terms(pl.multiple_of((c * group + ci) * CHUNK, CHUNK)) for ci in range(group)]
        items = [(t, slice(p * PAIR, (p + 1) * PAIR)) for t in terms for p in range(RW_HEADS // 2)]
        idx = range(len(items))
        sc = [_dot_nt(jnp.concatenate([stack(t["a_cen"][:, ls]), stack(t["r_cen"][:, ls])], axis=0).astype(BF16),
                      jnp.concatenate([stack(t["b_cen"][:, ls]), stack(t["k_cen"][:, ls])], axis=0).astype(BF16))
              for t, ls in items]
        a_ab = [jnp.where(strict_lower, s_[0:PAIR, 0:PAIR], 0.0) for s_ in sc]
        a_ak = [jnp.where(strict_lower, s_[0:PAIR, PAIR:2 * PAIR], 0.0).astype(BF16) for s_ in sc]
        m_rb = [jnp.where(lower, s_[PAIR:2 * PAIR, 0:PAIR], 0.0).astype(BF16) for s_ in sc]
        m_rk = [jnp.where(lower, s_[PAIR:2 * PAIR, PAIR:2 * PAIR], 0.0).astype(BF16) for s_ in sc]
        v_st = [stack(t["v"][:, ls]).astype(BF16) for t, ls in items]
        x = [jnp.concatenate([stack(t["a_tru"][:, ls]), _dot(a_ak[i], v_st[i])], axis=1)
             for i, (t, ls) in enumerate(items)]
        n = a_ab
        steps = int(math.log2(CHUNK))
        for k in range(steps):
            nb = [n_.astype(BF16) for n_ in n]
            if k + 1 < steps:
                both = [_dot(nb[i], jnp.concatenate([x[i].astype(BF16), nb[i]], axis=1)) for i in idx]
                x = [x[i] + both[i][:, 0:2 * PAIR] for i in idx]
                n = [b_[:, 2 * PAIR:3 * PAIR] for b_ in both]
            else:
                x = [x[i] + _dot(nb[i], x[i].astype(BF16)) for i in idx]
        xb = [x_.astype(BF16) for x_ in x]
        ry = [_dot(m_rb[i], xb[i]) for i in idx]
        r_new = [unstack(stack(t["r_tru"][:, ls]) + ry[i][:, 0:PAIR]).astype(BF16)
                 for i, (t, ls) in enumerate(items)]
        y0 = [unstack(ry[i][:, PAIR:2 * PAIR] + _dot(m_rk[i], v_st[i])) for i in idx]
        b_st = [stack(t["b_end"][:, ls]).astype(BF16) for t, ls in items]
        k_st = [stack(t["k_end"][:, ls]).astype(BF16) for t, ls in items]
        gh = [_dot_tn(xb[i], b_st[i]) for i in idx]
        h_t = [gh[i][PAIR:2 * PAIR] + _dot_tn(v_st[i], k_st[i]) for i in idx]
        g_t = [gh[i][0:PAIR].astype(BF16) for i in idx]
        for i, (t, ls) in enumerate(items):
            p = i % (RW_HEADS // 2)
            s_old = state_ref[p]
            sb = s_old.astype(BF16)
            y_s[t["rows"], ls] = _dot_nt(r_new[i], sb) + y0[i]
            state_ref[p] = s_old * t["p_end"][:, ls] + _dot(sb, g_t[i]) + h_t[i]
        return carry

    lax.fori_loop(0, ts // (CHUNK * group), chunk_body, 0)

    y = y_s[...]
    inv_n = 1.0 / RW_HEAD
    mean = _dot_exact_lhs(y, ones_blk) * inv_n
    yc = y - mean
    var = _dot_exact_lhs(yc * yc, ones_blk) * inv_n
    yn = yc * lax.rsqrt(var + GN_EPS) * lnw_ref[...] + lnb_ref[...]
    ya_ref[0] = ((yn + bonus) * g).astype(ya_ref.dtype)


def _rwkv(zs3, mu, w0, a0, k_k, k_a, r_k, lnx_w, lnx_b, wlora, g2, ts=256):
    B, S, _ = zs3.shape
    W = RW_WIDTH
    seg = jnp.arange(MXU_WIDTH) // RW_HEAD
    ones_blk = (seg[:, None] == seg[None, :]).astype(BF16)
    ti = jnp.arange(ts)
    tri = ((ti[:, None] // CHUNK == ti[None, :] // CHUNK) & (ti[:, None] >= ti[None, :])).astype(BF16)
    vec = lambda n: pl.BlockSpec((1, n), lambda b, t: (0, 0))
    full = lambda a: pl.BlockSpec(a.shape, lambda b, t: (0,) * a.ndim)
    return pl.pallas_call(
        functools.partial(_rwkv_kernel, ts=ts, group=4),
        grid=(B, S // ts),
        in_specs=[
            pl.BlockSpec((1, ts, SHIFT_COLS), lambda b, t: (b, t, 0)),
            vec(SHIFT_COLS), vec(W), vec(W), vec(W), vec(W), vec(W), vec(W), vec(W),
            full(wlora), full(g2), full(ones_blk), full(tri),
        ],
        out_specs=pl.BlockSpec((1, ts, W), lambda b, t: (b, t, 0)),
        out_shape=jax.ShapeDtypeStruct((B, S, W), BF16),
        scratch_shapes=[
            pltpu.VMEM((RW_HEADS // 2, PAIR, PAIR), F32),
            pltpu.VMEM((1, SHIFT_COLS), F32),
        ] + [pltpu.VMEM((ts, W), F32) for _ in range(8)],
        compiler_params=pltpu.CompilerParams(
            dimension_semantics=("parallel", "arbitrary"), vmem_limit_bytes=VMEM_LIMIT),
        name="rwkv",
    )(zs3, mu, w0, a0, k_k, k_a, r_k, lnx_w, lnx_b, wlora, g2, ones_blk, tri)


def _attn_kernel(slope_ref, lq1_ref, lk1_ref, lq2_ref, lk2_ref, sw_ref, q_ref, k_ref, v_ref, o_ref,
                 m_s, l_s, acc_s, bias_s, *, tq, tk, nh):
    hg = pl.program_id(1)
    qi = pl.program_id(2)
    heads = range(nh)
    slope = [slope_ref[hg * nh + h] for h in heads]
    lam = (jnp.exp(jnp.sum(lq1_ref[...] * lk1_ref[...], axis=-1, keepdims=True))
           - jnp.exp(jnp.sum(lq2_ref[...] * lk2_ref[...], axis=-1, keepdims=True)) + LAM_INIT)

    lane = lax.broadcasted_iota(jnp.int32, (1, LANES), 1)
    map0 = lane < DA_HEAD
    qst = []
    for h in heads:
        q = q_ref[0, :, h * LANES:(h + 1) * LANES] * (1.0 / math.sqrt(DA_HEAD))
        zero = jnp.zeros_like(q)
        qst.append(jnp.concatenate([jnp.where(map0, q, zero), jnp.where(map0, zero, q)], axis=0))

    m_s[...] = jnp.full_like(m_s, NEG_BIG)
    l_s[...] = jnp.zeros_like(l_s)
    acc_s[...] = jnp.zeros_like(acc_s)

    @pl.when(qi == 0)
    def _():
        krow = lax.broadcasted_iota(jnp.int32, (tk, 2 * tq), 0)
        qcol = lax.broadcasted_iota(jnp.int32, (tk, 2 * tq), 1)
        qcol = jnp.where(qcol >= tq, qcol - tq, qcol)
        rel = (qcol - krow).astype(F32)
        for h in heads:
            bias_s[h] = -slope[h] * rel

    def block(j, diagonal):
        k0 = pl.multiple_of(j * tk, tk)
        kb = [k_ref[0, pl.ds(k0, tk), h * LANES:(h + 1) * LANES] for h in heads]
        vb = [v_ref[0, pl.ds(k0, tk), h * LANES:(h + 1) * LANES] for h in heads]
        off = ((qi - j) * tq).astype(F32)
        shift = [slope[h] * off for h in heads]
        b = [bias_s[h] for h in heads]
        x = [_dot_nt(kb[h], qst[h]) + b[h] for h in heads]
        if diagonal:
            x = [jnp.where(b[h] <= 0.0, x[h], NEG_BIG) for h in heads]
        m_old = [m_s[h] for h in heads]
        m_new = [jnp.maximum(m_old[h], jnp.max(x[h], axis=0, keepdims=True) - shift[h]) for h in heads]
        alpha = [jnp.exp(m_old[h] - m_new[h]) for h in heads]
        p = [jnp.exp(x[h] - (m_new[h] + shift[h])) for h in heads]
        pv = [_dot_tn(vb[h], p[h].astype(BF16)) for h in heads]
        for h in heads:
            l_s[h] = alpha[h] * l_s[h] + jnp.sum(p[h], axis=0, keepdims=True)
            acc_s[h] = alpha[h] * acc_s[h] + pv[h]
            m_s[h] = m_new[h]

    def body(j, carry):
        block(j, False)
        return carry

    lax.fori_loop(0, qi, body, 0)
    block(qi, True)

    for h in heads:
        o = acc_s[h] / l_s[h]
        o = o[:, 0:tq] - lam * o[:, tq:2 * tq]
        o = o * lax.rsqrt(jnp.mean(o * o, axis=0, keepdims=True) + SUBLN_EPS) * sw_ref[...]
        o_ref[0, :, h * LANES:(h + 1) * LANES] = (o * (1.0 - LAM_INIT)).T.astype(o_ref.dtype)


def _attn(zqkv3, lam_q1, lam_k1, lam_q2, lam_k2, subln_w, tq=512, tk=512, nh=4):
    B, S, _ = zqkv3.shape
    assert tq == tk, "the kernel masks only the diagonal block of aligned square tiles"
    slopes = jnp.asarray([2.0 ** (-8.0 * (i + 1) / DA_HEADS) for i in range(DA_HEADS)], F32)
    ng = DA_HEADS // nh
    wide = nh * LANES
    vec = lambda n: pl.BlockSpec((1, n), lambda b, g, i: (0, 0))
    return pl.pallas_call(
        functools.partial(_attn_kernel, tq=tq, tk=tk, nh=nh),
        grid=(B, ng, S // tq),
        in_specs=[
            pl.BlockSpec(memory_space=pltpu.SMEM),
            vec(DA_HEAD), vec(DA_HEAD), vec(DA_HEAD), vec(DA_HEAD),
            pl.BlockSpec((DA_VDIM, 1), lambda b, g, i: (0, 0)),
            pl.BlockSpec((1, tq, wide), lambda b, g, i: (b, i, g)),
            pl.BlockSpec((1, S, wide), lambda b, g, i: (b, 0, ng + g)),
            pl.BlockSpec((1, S, wide), lambda b, g, i: (b, 0, 2 * ng + g)),
        ],
        out_specs=pl.BlockSpec((1, tq, wide), lambda b, g, i: (b, i, g)),
        out_shape=jax.ShapeDtypeStruct((B, S, DA_WIDTH), BF16),
        scratch_shapes=[
            pltpu.VMEM((nh, 1, 2 * tq), F32),
            pltpu.VMEM((nh, 1, 2 * tq), F32),
            pltpu.VMEM((nh, DA_VDIM, 2 * tq), F32),
            pltpu.VMEM((nh, tk, 2 * tq), F32),
        ],
        compiler_params=pltpu.CompilerParams(
            dimension_semantics=("parallel", "parallel", "arbitrary"), vmem_limit_bytes=VMEM_LIMIT),
        name="attn",
    )(slopes, lam_q1, lam_k1, lam_q2, lam_k2, subln_w, zqkv3, zqkv3, zqkv3)


def _merge_kernel(ya_ref, yb_ref, zg_ref, x_ref, pa_ref, pb_ref, wo_ref, nw_ref, wq_ref,
                  x1_ref, h2t_ref, q_ref):
    pa = _dot(ya_ref[...], pa_ref[...])
    pb = _dot(yb_ref[...], pb_ref[...])
    ga = zg_ref[:, 0:D_MODEL]
    gb = zg_ref[:, D_MODEL:2 * D_MODEL]
    merged = _sigmoid(ga) * pa + _sigmoid(gb) * pb
    x1 = x_ref[...] + _dot(merged.astype(BF16), wo_ref[...])
    x1_ref[...] = x1
    ms = jnp.mean(x1 * x1, axis=-1, keepdims=True)
    h2 = x1 * lax.rsqrt(ms + NORM_EPS) * nw_ref[...]
    h2t_ref[0] = h2.T.astype(BF16)
    q = _dot(h2.astype(BF16), wq_ref[...]).astype(q_ref.dtype)
    for hd in range(PK_HEADS):
        q_ref[hd] = q[:, hd * PK_QDIM:(hd + 1) * PK_QDIM]


def _merge(ya2, yb2, zg, x2, proj_a, proj_b, w_out, norm_w, wq, tm=PEER_TM):
    T = x2.shape[0]
    row = lambda n: pl.BlockSpec((tm, n), lambda i: (i, 0))
    full = lambda a: pl.BlockSpec(a.shape, lambda i: (0,) * a.ndim)
    return pl.pallas_call(
        _merge_kernel,
        grid=(T // tm,),
        in_specs=[row(RW_WIDTH), row(DA_WIDTH), row(GATE_COLS), row(D_MODEL),
                  full(proj_a), full(proj_b), full(w_out), full(norm_w), full(wq)],
        out_specs=[row(D_MODEL),
                   pl.BlockSpec((1, D_MODEL, tm), lambda i: (i, 0, 0)),
                   pl.BlockSpec((PK_HEADS, tm, PK_QDIM), lambda i: (0, i, 0))],
        out_shape=[
            jax.ShapeDtypeStruct((T, D_MODEL), F32),
            jax.ShapeDtypeStruct((T // tm, D_MODEL, tm), BF16),
            jax.ShapeDtypeStruct((PK_HEADS, T, PK_QDIM), BF16),
        ],
        compiler_params=pltpu.CompilerParams(
            dimension_semantics=("parallel",), vmem_limit_bytes=VMEM_LIMIT),
        name="merge",
    )(ya2, yb2, zg, x2, proj_a, proj_b, w_out, norm_w, wq)


_STAIR = tuple(PK_TOPK // (i + 1) for i in range(PK_TOPK))


def _route_kernel(q_ref, keys_ref, r2_ref, e2_ref, n1_ref, c1_ref, v1_s, v2_s):
    half = PK_QDIM // 2
    heads = range(q_ref.shape[0])
    s1 = [_dot_nt(keys_ref[h, 0], q_ref[h][:, 0:half]) for h in heads]
    s2 = [_dot_nt(keys_ref[h, 1], q_ref[h][:, half:2 * half]) for h in heads]

    rank2 = [jnp.full(x.shape, float(PK_TOPK), F32) for x in s2]
    w1, w2 = list(s1), list(s2)
    for i in range(PK_TOPK):
        m1 = [jnp.max(x, axis=0, keepdims=True) for x in w1]
        m2 = [jnp.max(x, axis=0, keepdims=True) for x in w2]
        for h in heads:
            v1_s[h, i:i + 1, :] = m1[h]
            v2_s[h, i:i + 1, :] = m2[h]
        w1 = [jnp.where(x == m, -jnp.inf, x) for x, m in zip(w1, m1)]
        hit2 = [x == m for x, m in zip(w2, m2)]
        rank2 = [jnp.where(hit, float(i), r) for hit, r in zip(hit2, rank2)]
        w2 = [jnp.where(hit, -jnp.inf, x) for hit, x in zip(hit2, w2)]

    row8 = lax.broadcasted_iota(jnp.int32, (8, 1), 0)
    for h in heads:
        v1 = v1_s[h]
        v2 = v2_s[h]

        def stair(i):
            return jnp.where(row8 < _STAIR[i], v1[i:i + 1] + v2[0:8], -jnp.inf)

        v2_lo = jnp.where(row8 < 4, v2[0:8], pltpu.roll(v2[0:8], 4, axis=0))

        def stair_pair(ia, ib):
            v1_sel = jnp.where(row8 < 4, v1[ia:ia + 1], v1[ib:ib + 1])
            ok = (row8 < _STAIR[ia]) | ((row8 >= 4) & (row8 < 4 + _STAIR[ib]))
            return jnp.where(ok, v1_sel + v2_lo, -jnp.inf)

        cand = jnp.concatenate([v1[0:1] + v2, stair(1), stair(2), stair(3), stair_pair(4, 5), stair_pair(6, 7),
                                v1[8:PK_TOPK] + v2[0:1]], axis=0)
        work = cand
        tau = None
        for i in range(PK_TOPK):
            tau = jnp.max(work, axis=0, keepdims=True)
            work = jnp.where(work == tau, -jnp.inf, work)
        cmax = v1[0:1] + v2[0:1]
        z = jnp.sum(jnp.where(cand >= tau, jnp.exp(cand - cmax), 0.0), axis=0, keepdims=True)
        n1 = jnp.zeros_like(s1[h])
        for jj in range(PK_TOPK):
            reach = v1 + v2[jj:jj + 1] >= tau
            cut = jnp.min(jnp.where(reach, v1, jnp.inf), axis=0, keepdims=True)
            n1 = jnp.where(s1[h] >= cut, float(jj + 1), n1)
        r2_ref[0, h] = rank2[h].astype(r2_ref.dtype)
        e2_ref[0, h] = jnp.exp(s2[h] - v2[0:1]).astype(e2_ref.dtype)
        n1_ref[0, h] = n1
        c1_ref[0, h] = jnp.exp(s1[h] - v1[0:1]) * (0.5 / z)


def _route(q3, keys, tm=PEER_TM, nh=8):
    T = q3.shape[1]
    blk = lambda: pl.BlockSpec((1, nh, PK_NKEYS, tm), lambda i, h: (i, h, 0, 0))
    shp = lambda dt: jax.ShapeDtypeStruct((T // tm, PK_HEADS, PK_NKEYS, tm), dt)
    return pl.pallas_call(
        _route_kernel,
        grid=(T // tm, PK_HEADS // nh),
        in_specs=[
            pl.BlockSpec((nh, tm, PK_QDIM), lambda i, h: (h, i, 0)),
            pl.BlockSpec((nh, 2, PK_NKEYS, PK_QDIM // 2), lambda i, h: (h, 0, 0, 0)),
        ],
        out_specs=[blk(), blk(), blk(), blk()],
        out_shape=[shp(BF16), shp(BF16), shp(F32), shp(F32)],
        scratch_shapes=[pltpu.VMEM((nh, PK_TOPK, tm), F32), pltpu.VMEM((nh, PK_TOPK, tm), F32)],
        compiler_params=pltpu.CompilerParams(
            dimension_semantics=("parallel", "arbitrary"), vmem_limit_bytes=VMEM_LIMIT),
        name="route",
    )(q3, keys)


def _peer_kernel(ht_ref, u_ref, vt_ref, r2_ref, e2_ref, n1_ref, c1_ref, x1_ref, nw_ref, o_ref,
                 acc_s, aw0_s, aw1_s, act0_s, act1_s, *, te, nst):
    s = pl.program_id(0)
    jp = lax.rem(s + (nst - 1), jnp.int32(nst))

    @pl.when(s == 0)
    def _():
        acc_s[...] = jnp.zeros_like(acc_s)
        aw1_s[...] = jnp.zeros_like(aw1_s)

    group = act0_s.shape[0]
    npiece = te // group
    e_per_piece = group // PK_NKEYS
    tm = ht_ref.shape[2]
    drain_every = 2
    drows = D_MODEL * drain_every // npiece
    restart = jp == 0
    act_bufs = (act0_s, act1_s)
    tile = (BF16_SUBLANES, tm)
    ntile = PK_NKEYS // BF16_SUBLANES

    def u_proj(i):
        act = _dot(u_ref[i * group:(i + 1) * group, :], ht_ref[0])
        act_bufs[i % 2][...] = act.astype(BF16)

    def step(aw_build, aw_drain):
        u_proj(0)
        for i in range(npiece):
            if i + 1 < npiece:
                u_proj(i + 1)
            for half in range(e_per_piece):
                e = i * e_per_piece + half
                a = act_bufs[i % 2][half * PK_NKEYS:(half + 1) * PK_NKEYS, :]
                gelu2 = a * (1.0 + lax.erf(a * (1.0 / math.sqrt(2.0))))
                w = None
                for hd in range(PK_HEADS):
                    n = jnp.broadcast_to(n1_ref[0, hd, e:e + 1, :], tile).astype(BF16)
                    c = jnp.broadcast_to(c1_ref[0, hd, e:e + 1, :], tile).astype(BF16)
                    r2 = r2_ref[0, hd].reshape(ntile, *tile)
                    e2 = e2_ref[0, hd].reshape(ntile, *tile)
                    wh = jnp.where(r2 < n[None], e2 * c[None], jnp.zeros(r2.shape, BF16))
                    w = wh if w is None else w + wh
                aw_build[e * PK_NKEYS:(e + 1) * PK_NKEYS, :] = gelu2 * w.reshape(a.shape)
            if (i + 1) % drain_every == 0:
                d = (i + 1) // drain_every - 1
                rs = slice(d * drows, (d + 1) * drows)
                part = _dot(vt_ref[rs, :], aw_drain[...])
                acc_s[rs, :] = jnp.where(restart, 0.0, acc_s[rs, :]) + part

    parity = lax.rem(s, jnp.int32(2))

    @pl.when(parity == 0)
    def _():
        step(aw0_s, aw1_s)

    @pl.when(parity == 1)
    def _():
        step(aw1_s, aw0_s)

    @pl.when(jnp.logical_and(jp == nst - 1, s > 0))
    def _():
        x2 = x1_ref[...] + acc_s[...].T
        ms = jnp.mean(x2 * x2, axis=-1, keepdims=True)
        o_ref[...] = x2 * lax.rsqrt(ms + NORM_EPS) * nw_ref[...]


def _peer(h2t, u_bf16, vt_bf16, r2, e2, n1, c1, x1, final_w, te=2048):
    ntile, _, tm = h2t.shape
    T = ntile * tm
    group = max(PK_NKEYS, PEER_PIECE_ELEMS // tm)
    nst = PK_EXPERTS // te
    tile_ab = lambda s: jnp.minimum(s // nst, ntile - 1)
    tile_c = lambda s: jnp.maximum(s - 1, 0) // nst
    rt = lambda: pl.BlockSpec((1, PK_HEADS, PK_NKEYS, tm), lambda s: (tile_ab(s), 0, 0, 0))
    rows = lambda: pl.BlockSpec((1, PK_HEADS, te // PK_NKEYS, tm), lambda s: (tile_ab(s), 0, s % nst, 0))
    return pl.pallas_call(
        functools.partial(_peer_kernel, te=te, nst=nst),
        grid=(ntile * nst + 1,),
        in_specs=[
            pl.BlockSpec((1, D_MODEL, tm), lambda s: (tile_ab(s), 0, 0)),
            pl.BlockSpec((te, D_MODEL), lambda s: (s % nst, 0)),
            pl.BlockSpec((D_MODEL, te), lambda s: (0, jnp.maximum(s - 1, 0) % nst)),
            rt(), rt(), rows(), rows(),
            pl.BlockSpec((tm, D_MODEL), lambda s: (tile_c(s), 0)),
            pl.BlockSpec((1, D_MODEL), lambda s: (0, 0)),
        ],
        out_specs=pl.BlockSpec((tm, D_MODEL), lambda s: (tile_c(s), 0)),
        out_shape=jax.ShapeDtypeStruct((T, D_MODEL), F32),
        scratch_shapes=[pltpu.VMEM((D_MODEL, tm), F32),
                        pltpu.VMEM((te, tm), BF16), pltpu.VMEM((te, tm), BF16),
                        pltpu.VMEM((group, tm), BF16), pltpu.VMEM((group, tm), BF16)],
        compiler_params=pltpu.CompilerParams(
            dimension_semantics=("arbitrary",), vmem_limit_bytes=VMEM_LIMIT),
        name="peer",
    )(h2t, u_bf16, vt_bf16, r2, e2, n1, c1, x1, final_w)


def _lora_blockdiag(w2, a2):
    z = jnp.zeros_like(w2)
    return jnp.concatenate([jnp.concatenate([w2, z], axis=1), jnp.concatenate([z, a2], axis=1)], axis=0)


def kernel(x, norm_mix_w, w_in, shift_mu, w0, w2, a0, a2, g2, k_k, k_a, r_k, lnx_w, lnx_b, lam_q1, lam_k1, lam_q2, lam_k2, subln_w, proj_a, proj_b, w_out, norm_ffn_w, peer_wq, peer_keys, peer_u, peer_v, final_norm_w):
    B, S, D = x.shape
    T = B * S
    depth = w_in.shape[0]
    assert depth == 1 and D == D_MODEL
    l = 0
    x2 = x.reshape(T, D)
    zs, zqkv, zg = _inproj(x2, norm_mix_w[l][None], w_in[l].astype(BF16))
    ya = _rwkv(zs.reshape(B, S, SHIFT_COLS), shift_mu[l][None], w0[l][None], a0[l][None], k_k[l][None],
               k_a[l][None], r_k[l].reshape(1, RW_WIDTH), lnx_w[l][None], lnx_b[l][None],
               _lora_blockdiag(w2[l], a2[l]).astype(BF16), g2[l].astype(BF16))
    yb = _attn(zqkv.reshape(B, S, QKV_COLS), lam_q1[l][None], lam_k1[l][None], lam_q2[l][None],
               lam_k2[l][None], subln_w[l][:, None])
    x1, h2t, q = _merge(ya.reshape(T, RW_WIDTH), yb.reshape(T, DA_WIDTH), zg, x2,
                       proj_a[l].astype(BF16), proj_b[l].astype(BF16), w_out[l].astype(BF16),
                       norm_ffn_w[l][None], peer_wq[l].astype(BF16))
    r2, e2, n1, c1 = _route(q, peer_keys[l].astype(BF16))
    out = _peer(h2t, peer_u[l].astype(BF16), peer_v[l].T.astype(BF16), r2, e2, n1, c1, x1,
                final_norm_w[None])
    return out.reshape(B, S, D)
```
